```python
import math
import jax, jax.numpy as jnp
from jax import lax
import numpy as np

D_MODEL = 1024
BATCH = 4
SEQ = 8192
DEPTH = 2
DEC_BATCH = 32
DEC_SEQ = 8
PAST_LEN = 16384
PAGE_SIZE = 128

HD = 64
H_A = 4
DA = HD // 2
H_B = 6
G_B = 2
R_B = H_B // G_B
H_C = 6
MIX_WIDTH = HD * (H_A + H_B + H_C)
CMP_BLOCK = 64
TOPK = 16
WINDOW = 512
FORCE_BONUS = float(R_B + 1)
N_BUCKETS = 32
REL_MAX_DIST = 1024
Q_BLOCK = 128
EPS = 1e-6
SPLIT_SIZES = (H_A * HD, H_A * HD, H_A * HD, H_A * HD,
               H_B * HD, 4 * G_B * HD, 2 * G_B * HD, 3 * H_B, H_B * HD,
               H_C * HD, H_C * HD, H_C * HD, H_C, H_C * HD)
PROJ_WIDTH = sum(SPLIT_SIZES)

kernel_name = 'hymba_diff_nsa_fox_decode_step'


def rms_norm(x, g):
    xf = x.astype(jnp.float32)
    y = xf * lax.rsqrt(jnp.mean(xf * xf, axis=-1, keepdims=True) + EPS)
    return (y * g.astype(jnp.float32)).astype(x.dtype)


def masked_softmax(s, mask):
    s = jnp.where(mask, s, -1e30)
    m = jnp.max(s, axis=-1, keepdims=True)
    p = jnp.exp(s - m) * mask
    return p / jnp.maximum(jnp.sum(p, axis=-1, keepdims=True), 1e-30)


def rel_bucket(rel):
    n = jnp.maximum(rel, 0)
    max_exact = N_BUCKETS // 2
    nf = jnp.maximum(n, 1).astype(jnp.float32)
    large = max_exact + (jnp.log(nf / max_exact) / math.log(REL_MAX_DIST / max_exact)
                         * (N_BUCKETS - max_exact)).astype(jnp.int32)
    return jnp.where(n < max_exact, n, jnp.minimum(large, N_BUCKETS - 1))


def diff_lambda(lam_a, lam_init):
    lf = lam_a.astype(jnp.float32)
    return jnp.exp(jnp.sum(lf[0] * lf[1])) - jnp.exp(jnp.sum(lf[2] * lf[3])) + lam_init


def layer_inputs(x, norm_g, w_in, qk_a, qk_b, qk_c, b_f):
    B, S, _ = x.shape
    h = rms_norm(x, norm_g)
    proj = jnp.einsum('bsd,de->bse', h, w_in)
    points = [int(v) for v in np.cumsum(SPLIT_SIZES)[:-1]]
    qa, ka, va, za, qb, kvb, winb, gb, zb, qc, kc, vc, fc, zc = jnp.split(proj, points, axis=-1)
    qa = rms_norm(qa.reshape(B, S, H_A, 2, DA), qk_a[0])
    ka = rms_norm(ka.reshape(B, S, H_A, 2, DA), qk_a[1])
    va = va.reshape(B, S, H_A, HD)
    qb = rms_norm(qb.reshape(B, S, G_B, R_B, HD), qk_b[0])
    kvb = kvb.reshape(B, S, 4, G_B, HD)
    kvb = jnp.stack([rms_norm(kvb[:, :, 0], qk_b[1]), kvb[:, :, 1],
                     rms_norm(kvb[:, :, 2], qk_b[1]), kvb[:, :, 3]], axis=2)
    winb = winb.reshape(B, S, 2, G_B, HD)
    winb = jnp.stack([rms_norm(winb[:, :, 0], qk_b[1]), winb[:, :, 1]], axis=2)
    gates = jax.nn.sigmoid(gb.reshape(B, S, G_B, R_B, 3))
    qc = rms_norm(qc.reshape(B, S, H_C, HD), qk_c[0])
    kc = rms_norm(kc.reshape(B, S, H_C, HD), qk_c[1])
    vc = vc.reshape(B, S, H_C, HD)
    logf = jax.nn.log_sigmoid((fc + b_f).astype(jnp.float32))
    a_rows = jnp.stack([ka.reshape(B, S, H_A, HD), va], axis=2)
    c_rows = jnp.stack([kc, vc], axis=2)
    return qa, a_rows, za, qb, kvb, winb, gates, zb, qc, c_rows, logf, zc


def diff_attention(q, k, v, qpos, kpos, lam, lam_init, table, onorm):
    s = jnp.einsum('bqhmd,bkhmd->bhmqk', q, k, preferred_element_type=jnp.float32) * DA ** -0.5
    rel = qpos[:, None] - kpos[None, :]
    bias = jnp.transpose(table[rel_bucket(rel)], (2, 0, 1)).astype(jnp.float32)
    p = masked_softmax(s + bias[None, :, None], rel >= 0)
    pd = p[:, :, 0] - lam * p[:, :, 1]
    o = jnp.einsum('bhqk,bkhd->bqhd', pd.astype(v.dtype), v)
    return rms_norm(o, onorm) * (1.0 - lam_init)


def forgetting_attention(q, k, v, cq, ck, qpos, kpos):
    s = jnp.einsum('bqhd,bkhd->bhqk', q, k, preferred_element_type=jnp.float32) * HD ** -0.5
    s = s + jnp.moveaxis(cq, 1, 2)[..., :, None] - jnp.moveaxis(ck, 1, 2)[..., None, :]
    p = masked_softmax(s, kpos[None, :] <= qpos[:, None])
    return jnp.einsum('bhqk,bkhd->bqhd', p.astype(v.dtype), v)


def compress_blocks(rows, w):
    B, L = rows.shape[:2]
    blocks = rows.reshape(B, L // CMP_BLOCK, CMP_BLOCK, G_B, HD)
    wt = jax.nn.softmax(w.astype(jnp.float32)).astype(rows.dtype)
    return jnp.einsum('bnlgd,l->bngd', blocks, wt)


def nsa_attention(q, qpos, kcmp, vcmp, ksl, vsl, kwin, vwin, wpos, gates, table):
    B, Q, G, R, _ = q.shape
    nb = kcmp.shape[1]
    scale = HD ** -0.5
    blk = jnp.arange(nb, dtype=jnp.int32)
    cmask = ((blk + 1) * CMP_BLOCK - 1)[None, :] <= qpos[:, None]
    s_c = jnp.einsum('bqgrd,bngd->bgrqn', q, kcmp, preferred_element_type=jnp.float32) * scale
    p_c = masked_softmax(s_c, cmask)
    o_c = jnp.einsum('bgrqn,bngd->bqgrd', p_c.astype(vcmp.dtype), vcmp)
    qblk = qpos // CMP_BLOCK
    forced = (blk[None] == qblk[:, None]) | (blk[None] == qblk[:, None] - 1) | (blk[None] == 0)
    valid = blk[None] * CMP_BLOCK <= qpos[:, None]
    score = jnp.where(valid, jnp.sum(p_c, axis=2) + jnp.where(forced, FORCE_BONUS, 0.0), -1.0)
    top_s, idx = lax.top_k(score, min(TOPK, nb))
    bi = jnp.arange(B)[:, None, None, None]
    gi = jnp.arange(G)[None, :, None, None]
    kb = jnp.moveaxis(ksl.reshape(B, nb, CMP_BLOCK, G, HD), 3, 1)
    vb = jnp.moveaxis(vsl.reshape(B, nb, CMP_BLOCK, G, HD), 3, 1)
    kg = kb[bi, gi, idx]
    vg = vb[bi, gi, idx]
    kpos = idx[..., None] * CMP_BLOCK + jnp.arange(CMP_BLOCK, dtype=jnp.int32)
    rel_s = qpos[None, None, :, None, None] - kpos
    smask = (top_s >= 0.0)[..., None] & (rel_s >= 0)
    table_gr = jnp.moveaxis(table.reshape(N_BUCKETS, G, R), 0, 1)
    bias_s = jnp.moveaxis(table_gr[gi[..., None], rel_bucket(rel_s)], -1, 2)
    s_s = jnp.einsum('bqgrd,bgqnld->bgrqnl', q, kg, preferred_element_type=jnp.float32) * scale + bias_s
    p_s = masked_softmax(s_s.reshape(B, G, R, Q, -1), smask.reshape(B, G, 1, Q, -1))
    o_s = jnp.einsum('bgrqm,bgqmd->bqgrd', p_s.astype(vg.dtype), vg.reshape(B, G, Q, -1, HD))
    rel_w = qpos[:, None] - wpos[None, :]
    wmask = (rel_w >= 0) & (rel_w < WINDOW) & (wpos[None, :] >= 0)
    bias_w = jnp.transpose(table[rel_bucket(rel_w)].reshape(Q, -1, G, R), (2, 3, 0, 1))
    s_w = jnp.einsum('bqgrd,bwgd->bgrqw', q, kwin, preferred_element_type=jnp.float32) * scale + bias_w
    p_w = masked_softmax(s_w, wmask)
    o_w = jnp.einsum('bgrqw,bwgd->bqgrd', p_w.astype(vwin.dtype), vwin)
    return gates[..., 0:1] * o_c + gates[..., 1:2] * o_s + gates[..., 2:3] * o_w


def merge(x, o_a, o_b, o_c, za, zb, zc, w_out):
    B, S, _ = x.shape
    mixed = jnp.concatenate([o_a.reshape(B, S, -1) * jax.nn.silu(za),
                             o_b.reshape(B, S, -1) * jax.nn.silu(zb),
                             o_c.reshape(B, S, -1) * jax.nn.silu(zc)], axis=-1)
    return x + jnp.einsum('bse,ed->bsd', mixed, w_out)


def prompt_layer(x, rel_bias, lam_init, norm_g, w_in, w_out, qk_a, qk_b, qk_c, onorm_a, lam_a, cmp_w, b_f):
    B, S, _ = x.shape
    qa, a_rows, za, qb, kvb, winb, gates, zb, qc, c_rows, logf, zc = layer_inputs(
        x, norm_g, w_in, qk_a, qk_b, qk_c, b_f)
    pos = jnp.arange(S, dtype=jnp.int32)
    nqb = S // Q_BLOCK
    qpos_blk = pos.reshape(nqb, Q_BLOCK)

    def blocks(t):
        return jnp.moveaxis(t.reshape(B, nqb, Q_BLOCK, *t.shape[2:]), 1, 0)

    def unblocks(t):
        return jnp.moveaxis(t, 0, 1).reshape(B, S, *t.shape[3:])

    ka = a_rows[:, :, 0].reshape(B, S, H_A, 2, DA)
    va = a_rows[:, :, 1]
    lam = diff_lambda(lam_a, lam_init)
    table_a = rel_bias[:, :H_A]
    o_a = unblocks(lax.map(lambda a: diff_attention(a[0], ka, va, a[1], pos, lam, lam_init, table_a, onorm_a),
                           (blocks(qa), qpos_blk)))
    cfull = lax.cumsum(logf, axis=1)
    kc, vc = c_rows[:, :, 0], c_rows[:, :, 1]
    o_c = unblocks(lax.map(lambda a: forgetting_attention(a[0], kc, vc, a[2], cfull, a[1], pos),
                           (blocks(qc), qpos_blk, blocks(cfull))))
    kcmp = compress_blocks(kvb[:, :, 0], cmp_w[0])
    vcmp = compress_blocks(kvb[:, :, 1], cmp_w[1])
    ksl, vsl = kvb[:, :, 2], kvb[:, :, 3]
    wpad = jnp.pad(winb, ((0, 0), (WINDOW, 0), (0, 0), (0, 0), (0, 0)))
    table_b = rel_bias[:, H_A:]

    def nsa_block(a):
        qbk, qp, gk, i = a
        band = lax.dynamic_slice_in_dim(wpad, i * Q_BLOCK, WINDOW + Q_BLOCK, axis=1)
        wpos = i * Q_BLOCK - WINDOW + jnp.arange(WINDOW + Q_BLOCK, dtype=jnp.int32)
        return nsa_attention(qbk, qp, kcmp, vcmp, ksl, vsl, band[:, :, 0], band[:, :, 1], wpos, gk, table_b)

    o_b = unblocks(lax.map(nsa_block, (blocks(qb), qpos_blk, blocks(gates), jnp.arange(nqb, dtype=jnp.int32))))
    y = merge(x, o_a, o_b, o_c, za, zb, zc, w_out)
    win_state = winb[:, S - min(WINDOW, S):]
    return y, a_rows, kvb, win_state, c_rows, logf


def sample_layer(x, past_a, past_b, past_c, past_logf, win_buf, rel_bias, lam_init,
                 norm_g, w_in, w_out, qk_a, qk_b, qk_c, onorm_a, lam_a, cmp_w, b_f):
    B, S, _ = x.shape
    P = past_a.shape[1]
    qa, a_rows, za, qb, kvb, winb, gates, zb, qc, c_rows, logf, zc = layer_inputs(
        x, norm_g, w_in, qk_a, qk_b, qk_c, b_f)
    qpos = P + jnp.arange(S, dtype=jnp.int32)
    kpos = jnp.arange(P + S, dtype=jnp.int32)
    a_all = jnp.concatenate([past_a, a_rows], axis=1)
    o_a = diff_attention(qa, a_all[:, :, 0].reshape(B, P + S, H_A, 2, DA), a_all[:, :, 1], qpos, kpos,
                         diff_lambda(lam_a, lam_init), lam_init, rel_bias[:, :H_A], onorm_a)
    c_all = jnp.concatenate([past_c, c_rows], axis=1)
    lf_past = past_logf.astype(jnp.float32)
    c_past = -(lax.cumsum(lf_past, axis=1, reverse=True) - lf_past)
    c_new = lax.cumsum(logf, axis=1)
    o_c = forgetting_attention(qc, c_all[:, :, 0], c_all[:, :, 1], c_new,
                               jnp.concatenate([c_past, c_new], axis=1), qpos, kpos)
    L = P + S
    nb = -(-L // CMP_BLOCK)
    b_all = jnp.pad(jnp.concatenate([past_b, kvb], axis=1),
                    ((0, 0), (0, nb * CMP_BLOCK - L), (0, 0), (0, 0), (0, 0)))
    kcmp = compress_blocks(b_all[:, :, 0], cmp_w[0])
    vcmp = compress_blocks(b_all[:, :, 1], cmp_w[1])
    wb = win_buf.shape[1]
    win_all = jnp.concatenate([win_buf, winb], axis=1)
    wpos = P - wb + jnp.arange(wb + S, dtype=jnp.int32)
    o_b = nsa_attention(qb, qpos, kcmp, vcmp, b_all[:, :, 2], b_all[:, :, 3],
                        win_all[:, :, 0], win_all[:, :, 1], wpos, gates, rel_bias[:, H_A:])
    y = merge(x, o_a, o_b, o_c, za, zb, zc, w_out)
    return y, a_rows, kvb, win_all[:, -wb:], c_rows, logf


def gather_pages(pool, l, page_table):
    g = pool[l, page_table]
    return g.reshape(g.shape[0], g.shape[1] * g.shape[2], *g.shape[3:])


def setup_inputs(seed: int = 0) -> dict:
    key = jax.random.key(seed)
    ks = jax.random.split(key, 20)
    f32 = jnp.float32
    n_pages = PAST_LEN // PAGE_SIZE
    n_pool = (DEC_BATCH * n_pages * 5) // 4
    win_buf = min(WINDOW, PAST_LEN)

    def nrm(k, shape, s=1.0):
        return s * jax.random.normal(k, shape, f32)

    page_table = jax.random.permutation(ks[7], n_pool)[:DEC_BATCH * n_pages].reshape(
        DEC_BATCH, n_pages).astype(jnp.int32)
    return {
        'x_prompt': nrm(ks[0], (BATCH, SEQ, D_MODEL)),
        'x_sample': nrm(ks[1], (DEC_BATCH, DEC_SEQ, D_MODEL)),
        'cache_a_kv': nrm(ks[2], (DEPTH, n_pool, PAGE_SIZE, 2, H_A, HD)),
        'cache_b_kv': nrm(ks[3], (DEPTH, n_pool, PAGE_SIZE, 4, G_B, HD)),
        'cache_c_kv': nrm(ks[4], (DEPTH, n_pool, PAGE_SIZE, 2, H_C, HD)),
        'cache_c_logf': jax.nn.log_sigmoid(3.0 + nrm(ks[5], (DEPTH, n_pool, PAGE_SIZE, H_C))),
        'state_b_win': nrm(ks[6], (DEPTH, DEC_BATCH, win_buf, 2, G_B, HD)),
        'page_table': page_table,
        'rel_bias': nrm(ks[8], (N_BUCKETS, H_A + H_B), 0.5),
        'norm_g': 1.0 + nrm(ks[9], (DEPTH, D_MODEL), 0.05),
        'w_in': nrm(ks[10], (DEPTH, D_MODEL, PROJ_WIDTH), D_MODEL ** -0.5),
        'w_out': nrm(ks[11], (DEPTH, MIX_WIDTH, D_MODEL), MIX_WIDTH ** -0.5),
        'qk_a': 1.0 + nrm(ks[12], (DEPTH, 2, DA), 0.05),
        'qk_b': 1.0 + nrm(ks[13], (DEPTH, 2, HD), 0.05),
        'qk_c': 1.0 + nrm(ks[14], (DEPTH, 2, HD), 0.05),
        'onorm_a': 1.0 + nrm(ks[15], (DEPTH, HD), 0.05),
        'lam_a': nrm(ks[16], (DEPTH, 4, DA), 0.1),
        'cmp_w': nrm(ks[17], (DEPTH, 2, CMP_BLOCK), 0.1),
        'b_f': 3.0 + nrm(ks[18], (DEPTH, H_C), 0.1),
    }


def reference(x_prompt, x_sample, cache_a_kv, cache_b_kv, cache_c_kv, cache_c_logf, state_b_win, page_table,
              rel_bias, norm_g, w_in, w_out, qk_a, qk_b, qk_c, onorm_a, lam_a, cmp_w, b_f):
    yp, ys = x_prompt, x_sample
    ap, as_, bp, bs, wp, ws, cp, cs, lp, ls = [], [], [], [], [], [], [], [], [], []
    for l in range(DEPTH):
        lam_init = 0.8 - 0.6 * math.exp(-0.3 * l)
        w = (norm_g[l], w_in[l], w_out[l], qk_a[l], qk_b[l], qk_c[l], onorm_a[l], lam_a[l], cmp_w[l], b_f[l])
        yp, a_r, b_r, w_r, c_r, l_r = prompt_layer(yp, rel_bias, lam_init, *w)
        ap.append(a_r); bp.append(b_r); wp.append(w_r); cp.append(c_r); lp.append(l_r)
        ys, a_r, b_r, w_r, c_r, l_r = sample_layer(
            ys, gather_pages(cache_a_kv, l, page_table), gather_pages(cache_b_kv, l, page_table),
            gather_pages(cache_c_kv, l, page_table), gather_pages(cache_c_logf, l, page_table),
            state_b_win[l], rel_bias, lam_init, *w)
        as_.append(a_r); bs.append(b_r); ws.append(w_r); cs.append(c_r); ls.append(l_r)
    return (yp, ys,
            jnp.stack(ap), jnp.stack(as_),
            jnp.stack(bp), jnp.stack(bs),
            jnp.stack(wp), jnp.stack(ws),
            jnp.stack(cp), jnp.stack(cs),
            jnp.stack(lp), jnp.stack(ls))
```

```python
import functools
import math

import numpy as np
import jax
import jax.numpy as jnp
from jax import lax
from jax.experimental import pallas as pl
from jax.experimental.pallas import tpu as pltpu

F32 = jnp.float32
BF16 = jnp.bfloat16

HD = 64
H_A = 4
DA = HD // 2
H_B = 6
G_B = 2
R_B = H_B // G_B
H_C = 6
CMP_BLOCK = 64
TOPK = 16
WINDOW = 512
FORCE_BONUS = float(R_B + 1)
N_BUCKETS = 32
REL_MAX_DIST = 1024
EPS = 1e-6
PAGE = 128

LOG2E = 1.4426950408889634
NEG = -1e30
M_INIT = -0.5e30
LANES = 128
VMEM_LIMIT = 56 * 1024 * 1024

O_QB, O_KVB, O_WIN, O_GB, O_ZB, O_QC, O_FC, O_ZC = 1024, 1408, 1920, 2176, 2194, 2578, 3730, 3736
PROJ_W = 4120
P_QB, P_KVB, P_WIN, P_ZB, P_QC, P_ZC, P_MISC, P_END = 1024, 1408, 1920, 2176, 2560, 3712, 4096, 4224
SLAB_HEADS = [g * R_B + r for r in range(R_B) for g in range(G_B)]


def _cparams(n_grid):
    return pltpu.CompilerParams(dimension_semantics=("arbitrary",) * n_grid, vmem_limit_bytes=VMEM_LIMIT)


def _dot(a, b):
    return jnp.dot(a, b, preferred_element_type=F32)


def _dot_nt(a, b):
    return lax.dot_general(a, b, (((1,), (1,)), ((), ())), preferred_element_type=F32)


def _split3(x):
    x1 = x.astype(BF16)
    r = x - x1.astype(F32)
    x2 = r.astype(BF16)
    x3 = (r - x2.astype(F32)).astype(BF16)
    return x1, x2, x3


def _dot3(x, m):
    x1, x2, x3 = _split3(x)
    return _dot(x1, m) + _dot(x2, m) + _dot(x3, m)


def _dot3_l(m, x):
    x1, x2, x3 = _split3(x)
    return _dot(m, x1) + _dot(m, x2) + _dot(m, x3)


def _rel_bucket(rel):
    n = jnp.maximum(rel, 0)
    max_exact = N_BUCKETS // 2
    nf = jnp.maximum(n, 1).astype(F32)
    large = max_exact + (jnp.log(nf / max_exact) / math.log(REL_MAX_DIST / max_exact)
                         * (N_BUCKETS - max_exact)).astype(jnp.int32)
    return jnp.where(n < max_exact, n, jnp.minimum(large, N_BUCKETS - 1))


def _inproj_kernel(x_ref, w_ref, g_in, ga_q, ga_k, gb_q, gb_k, gc_q, gc_k, bf_row, cmpw_ref, bd32_ref, bd64_ref,
                   qa_o, arow_o, ka_o, va_o, qb_o, kvb_o, ksel_o, vsel_o, kcmp_o, vcmp_o, win_o, kwin_o, vwin_o,
                   qc_o, crow_o, kc_o, vc_o, zs_o, misc_o, *, tm, do_cmp):
    x = x_ref[...]
    ms = jnp.mean(x * x, axis=-1, keepdims=True)
    h = (x * lax.rsqrt(ms + EPS) * g_in[...]).astype(BF16)

    def proj(c0, c1):
        return _dot(h, w_ref[:, c0:c1])

    def segnorm(y, bd, seg):
        ss = _dot((y * y).astype(BF16), bd)
        return y * lax.rsqrt(ss * (1.0 / seg) + EPS)

    def silu(y):
        return y * (1.0 / (1.0 + jnp.exp(-y)))

    bd32 = bd32_ref[...]
    bd64 = bd64_ref[...]
    bd64_1 = bd64[0:128, 0:128]

    y = proj(0, 256)
    qa_o[...] = (segnorm(y, bd32, DA) * ga_q[...]).astype(BF16)
    y = segnorm(proj(256, 512), bd32, DA) * ga_k[...]
    arow_o[:, 0:256] = y
    ka_o[...] = y.astype(BF16)
    y = proj(512, 768)
    arow_o[:, 256:512] = y
    va_o[...] = y.astype(BF16)
    zs_o[:, 0:256] = silu(proj(768, 1024))

    qb_o[...] = (segnorm(proj(P_QB, P_KVB), bd64, HD) * gb_q[...]).astype(BF16)
    kcn = segnorm(proj(P_KVB, P_KVB + 128), bd64_1, HD) * gb_k[...]
    kvb_o[:, 0:128] = kcn
    vcr = proj(P_KVB + 128, P_KVB + 256)
    kvb_o[:, 128:256] = vcr
    y = segnorm(proj(P_KVB + 256, P_KVB + 384), bd64_1, HD) * gb_k[...]
    kvb_o[:, 256:384] = y
    ksel_o[...] = y.astype(BF16)
    y = proj(P_KVB + 384, P_KVB + 512)
    kvb_o[:, 384:512] = y
    vsel_o[...] = y.astype(BF16)
    if do_cmp:
        nbt = tm // CMP_BLOCK
        rowi = lax.broadcasted_iota(jnp.int32, (nbt, tm), 0)
        coli = lax.broadcasted_iota(jnp.int32, (nbt, tm), 1)
        blk = (coli >> 6) == rowi

        def cmp_weights(wl):
            e = jnp.where(blk, jnp.exp(wl - jnp.max(wl, axis=-1, keepdims=True)), 0.0)
            return (e / jnp.sum(e, axis=-1, keepdims=True)).astype(BF16)

        kcmp_o[...] = _dot(cmp_weights(cmpw_ref[0]), kcn.astype(BF16))
        vcmp_o[...] = _dot(cmp_weights(cmpw_ref[1]), vcr.astype(BF16))
    else:
        kcmp_o[...] = jnp.zeros(kcmp_o.shape, F32)
        vcmp_o[...] = jnp.zeros(vcmp_o.shape, F32)
    y = segnorm(proj(P_WIN, P_WIN + 128), bd64_1, HD) * gb_k[...]
    win_o[:, 0:128] = y
    kwin_o[...] = y.astype(BF16)
    y = proj(P_WIN + 128, P_WIN + 256)
    win_o[:, 128:256] = y
    vwin_o[...] = y.astype(BF16)
    zs_o[:, 256:640] = silu(proj(P_ZB, P_QC))

    qc_o[...] = (segnorm(proj(P_QC, P_QC + 384), bd64, HD) * gc_q[...]).astype(BF16)
    y = segnorm(proj(P_QC + 384, P_QC + 768), bd64, HD) * gc_k[...]
    crow_o[:, 0:384] = y
    kc_o[...] = y.astype(BF16)
    y = proj(P_QC + 768, P_ZC)
    crow_o[:, 384:768] = y
    vc_o[...] = y.astype(BF16)
    zs_o[:, 640:1024] = silu(proj(P_ZC, P_MISC))

    y = proj(P_MISC, P_END) + bf_row[...]
    lane = lax.broadcasted_iota(jnp.int32, y.shape, 1)
    sig = 1.0 / (1.0 + jnp.exp(-y))
    lsg = jnp.minimum(y, 0.0) - jnp.log(1.0 + jnp.exp(-jnp.abs(y)))
    misc_o[...] = jnp.where(lane < 18, sig, lsg)


def _inproj(x2, wp, *, do_cmp):
    T = x2.shape[0]
    tm = min(512, T)
    assert T % tm == 0 and (tm == 512 or not do_cmp)
    nbt = 8
    n = T // tm
    full = lambda a: pl.BlockSpec(a.shape, lambda i: (0,) * a.ndim)
    row = lambda w: pl.BlockSpec((tm, w), lambda i: (i, 0))
    consts = [wp["w_in"], wp["g_in"], wp["ga_q"], wp["ga_k"], wp["gb_q"], wp["gb_k"], wp["gc_q"], wp["gc_k"],
              wp["bf_row"], wp["cmpw_in"], wp["bd32"], wp["bd64"]]
    outs = [("qa", 256, BF16), ("arow", 512, F32), ("ka", 256, BF16), ("va", 256, BF16), ("qb", 384, BF16),
            ("kvb", 512, F32), ("ksel", 128, BF16), ("vsel", 128, BF16), ("kcmp", None, F32), ("vcmp", None, F32),
            ("win", 256, F32), ("kwin", 128, BF16), ("vwin", 128, BF16), ("qc", 384, BF16), ("crow", 768, F32),
            ("kc", 384, BF16), ("vc", 384, BF16), ("zs", 1024, F32), ("misc", 128, F32)]
    out_shape, out_specs = [], []
    for _, w, dt in outs:
        if w is None:
            out_shape.append(jax.ShapeDtypeStruct((n * nbt, 128), dt))
            out_specs.append(pl.BlockSpec((nbt, 128), lambda i: (i, 0)))
        else:
            out_shape.append(jax.ShapeDtypeStruct((T, w), dt))
            out_specs.append(row(w))
    res = pl.pallas_call(
        functools.partial(_inproj_kernel, tm=tm, do_cmp=do_cmp),
        grid=(n,),
        in_specs=[row(1024)] + [full(c) for c in consts],
        out_specs=out_specs, out_shape=out_shape,
        compiler_params=_cparams(1), name="inproj",
    )(x2, *consts)
    return {k: v for (k, _, _), v in zip(outs, res)}


def _attend(lhs, k_ref, v_ref, T, far_lo, near_lo, hi, near_add=None, all_add=None):
    def step(kt, carry, near):
        m, l, acc = carry
        ks = pl.multiple_of(kt * T, T)
        s = _dot_nt(lhs, k_ref[pl.ds(ks, T), :])
        if all_add is not None:
            s = all_add(s, ks)
        if near:
            s = near_add(s, kt)
        m_new = jnp.maximum(m, jnp.max(s, axis=-1, keepdims=True))
        p = jnp.exp2(s - m_new)
        alpha = jnp.exp2(m - m_new)
        l = alpha * l + jnp.sum(p, axis=-1, keepdims=True)
        acc = alpha * acc + _dot(p.astype(BF16), v_ref[pl.ds(ks, T), :])
        return m_new, l, acc

    carry = (jnp.full((T, 1), M_INIT, F32), jnp.zeros((T, 1), F32), jnp.zeros((T, v_ref.shape[-1]), F32))
    carry = lax.fori_loop(far_lo, near_lo, lambda kt, c: step(kt, c, False), carry)
    carry = lax.fori_loop(near_lo, hi, lambda kt, c: step(kt, c, True), carry)
    _, l, acc = carry
    return acc / l


def _lane_band(x, lo, width):
    lane = lax.broadcasted_iota(jnp.int32, x.shape, 1)
    return jnp.where((lane >= lo) & (lane < lo + width), x, jnp.zeros_like(x))


def _halves(a, b):
    lane = lax.broadcasted_iota(jnp.int32, a.shape, 1)
    return jnp.where(lane < HD, a, b)


def _diff_lambda(lam_ref, lam_init):
    lf = lam_ref[...]
    a = jnp.sum(lf[0:1] * lf[1:2], axis=-1, keepdims=True)
    b = jnp.sum(lf[2:3] * lf[3:4], axis=-1, keepdims=True)
    return jnp.exp(a) - jnp.exp(b) + lam_init


def _flash_a_kernel(q_ref, k_ref, v_ref, bias_ref, lam_ref, onorm_ref, o_ref, *, T, noff, lam_init):
    qi = pl.program_id(2)
    near_lo = jnp.maximum(qi - (noff - 1), 0)
    q = q_ref[...]
    lam = _diff_lambda(lam_ref, lam_init)
    heads = []
    for hl in range(2):
        maps = []
        for mp in range(2):
            lhs = _lane_band(q, hl * HD + mp * DA, DA)
            maps.append(_attend(lhs, k_ref, v_ref, T, 0, near_lo, qi + 1,
                                near_add=lambda s, kt, hl=hl: s + bias_ref[hl, qi - kt]))
        heads.append(maps[0] - lam * maps[1])
    o = _halves(heads[0], heads[1])
    x2 = o * o
    lane = lax.broadcasted_iota(jnp.int32, o.shape, 1)
    s0 = jnp.sum(jnp.where(lane < HD, x2, 0.0), axis=-1, keepdims=True)
    s1 = jnp.sum(jnp.where(lane < HD, 0.0, x2), axis=-1, keepdims=True)
    ss = jnp.where(lane < HD, s0, s1)
    o_ref[...] = o * lax.rsqrt(ss * (1.0 / HD) + EPS) * onorm_ref[...] * (1.0 - lam_init)


def _flash_c_kernel(q_ref, k_ref, v_ref, ck_ref, cq_ref, mask_ref, o_ref, *, T):
    qi = pl.program_id(2)
    q = q_ref[...]
    heads = []
    for hl in range(2):
        lhs = _lane_band(q, hl * HD, HD)
        cq = cq_ref[:, hl:hl + 1]
        heads.append(_attend(lhs, k_ref, v_ref, T, 0, qi, qi + 1,
                             near_add=lambda s, kt: s + mask_ref[...],
                             all_add=lambda s, ks, hl=hl, cq=cq: s + cq - ck_ref[hl, :, pl.ds(ks, T)]))
    o_ref[...] = _halves(heads[0], heads[1])


def _flash_b_kernel(*refs, T, noff, selected):
    if selected:
        q_ref, sel_ref, k_ref, v_ref, bias_ref, o_ref = refs
    else:
        q_ref, k_ref, v_ref, bias_ref, o_ref = refs
    qi = pl.program_id(2)
    near_lo = jnp.maximum(qi - (noff - 1), 0)
    far_lo = 0 if selected else near_lo
    q = q_ref[...]
    groups = []
    for g in range(G_B):
        lhs = _lane_band(q, g * HD, HD)
        if selected:
            nbp = sel_ref.shape[-1] // G_B
            lhs = jnp.concatenate([lhs, sel_ref[:, g * nbp:(g + 1) * nbp]], axis=1)
        groups.append(_attend(lhs, k_ref, v_ref, T, far_lo, near_lo, qi + 1,
                              near_add=lambda s, kt, g=g: s + bias_ref[g, qi - kt]))
    o_ref[...] = _halves(groups[0], groups[1])


def _flash_call(kernel, name, B, S, T, n_grp, ins, in_specs, out_w):
    return pl.pallas_call(
        kernel, grid=(B, n_grp, S // T), in_specs=in_specs,
        out_specs=pl.BlockSpec((None, T, LANES), lambda b, g, i: (b, i, g)),
        out_shape=jax.ShapeDtypeStruct((B, S, out_w), F32),
        compiler_params=_cparams(3), name=name,
    )(*ins)


def _qspec(T):
    return pl.BlockSpec((None, T, LANES), lambda b, g, i: (b, i, g))


def _kvspec(S, w, per_group):
    if per_group:
        return pl.BlockSpec((None, S, w), lambda b, g, i: (b, 0, g))
    return pl.BlockSpec((None, S, w), lambda b, g, i: (b, 0, 0))


def _biasspec(noff, T):
    return pl.BlockSpec((2, noff, T, T), lambda b, g, i: (g, 0, 0, 0))


def _small(a):
    return pl.BlockSpec(a.shape, lambda b, g, i: (0,) * a.ndim)


def _masked_softmax2(s, mask, axis):
    s = jnp.where(mask, s, NEG)
    m = jnp.max(s, axis=axis, keepdims=True)
    p = jnp.where(mask, jnp.exp2(s - m), 0.0)
    return p / jnp.maximum(jnp.sum(p, axis=axis, keepdims=True), 1e-30)


def _topk_select(score, idx, n_idx, axis):
    def it(_, c):
        sc, sel = c
        mx = jnp.max(sc, axis=axis, keepdims=True)
        first = jnp.min(jnp.where(sc == mx, idx, n_idx), axis=axis, keepdims=True)
        oh = idx == first
        sel = jnp.where(oh, jnp.where(mx >= 0.0, 1.0, sel), sel)
        sc = jnp.where(oh, -2.0, sc)
        return sc, sel

    _, sel = lax.fori_loop(0, TOPK, it, (score, jnp.zeros(score.shape, F32)))
    return sel


def _cmp_kernel(q_ref, kc_ref, vc_ref, oc_ref, selt_ref, *, T, NBP):
    qi = pl.program_id(1)
    q = q_ref[...]
    kc = kc_ref[...]
    vc = vc_ref[...]
    qpos_r = qi * T + lax.broadcasted_iota(jnp.int32, (T, NBP), 0)
    blk_c = lax.broadcasted_iota(jnp.int32, (T, NBP), 1)
    cm = ((blk_c + 1) * CMP_BLOCK - 1) <= qpos_r
    qpos_c = qi * T + lax.broadcasted_iota(jnp.int32, (NBP, T), 1)
    blk_r = lax.broadcasted_iota(jnp.int32, (NBP, T), 0)
    cmt = ((blk_r + 1) * CMP_BLOCK - 1) <= qpos_c
    imp = [jnp.zeros((NBP, T), F32) for _ in range(G_B)]
    for r in range(R_B):
        slab = q[:, r * LANES:(r + 1) * LANES]
        outs = []
        for g in range(G_B):
            qm = _lane_band(slab, g * HD, HD)
            p = _masked_softmax2(_dot_nt(qm, kc), cm, -1)
            outs.append(_dot(p.astype(BF16), vc))
            imp[g] = imp[g] + _masked_softmax2(_dot_nt(kc, qm), cmt, 0)
        oc_ref[:, r * LANES:(r + 1) * LANES] = _halves(outs[0], outs[1])
    qblk = qpos_c >> 6
    forced = (blk_r == qblk) | (blk_r == qblk - 1) | (blk_r == 0)
    valid = blk_r * CMP_BLOCK <= qpos_c
    for g in range(G_B):
        score = jnp.where(valid, imp[g] + jnp.where(forced, FORCE_BONUS, 0.0), -1.0)
        selt_ref[g] = _topk_select(score, blk_r, NBP, 0) - 1.0


def _cmp_call(qb3, kcmp3, vcmp3, T):
    B, S, _ = qb3.shape
    NBP = kcmp3.shape[1]
    return pl.pallas_call(
        functools.partial(_cmp_kernel, T=T, NBP=NBP), grid=(B, S // T),
        in_specs=[pl.BlockSpec((None, T, 384), lambda b, i: (b, i, 0)),
                  pl.BlockSpec((None, NBP, LANES), lambda b, i: (b, 0, 0)),
                  pl.BlockSpec((None, NBP, LANES), lambda b, i: (b, 0, 0))],
        out_specs=[pl.BlockSpec((None, T, 384), lambda b, i: (b, i, 0)),
                   pl.BlockSpec((None, G_B, NBP, T), lambda b, i: (b, 0, 0, i))],
        out_shape=[jax.ShapeDtypeStruct((B, S, 384), F32), jax.ShapeDtypeStruct((B, G_B, NBP, S), F32)],
        compiler_params=_cparams(2), name="nsa_cmp_topk",
    )(qb3, kcmp3, vcmp3)


def _cumsum_kernel(x_ref, mw_ref, mr_ref, o_ref, *, nh, sign):
    mw = mw_ref[...]
    mr = mr_ref[...]
    ones = jnp.ones((LANES, LANES), BF16)
    for h in range(nh):
        x = x_ref[h]
        tot = _dot3(x, ones)
        o_ref[h] = (_dot3(x, mw) + _dot3_l(mr, tot)) * (sign * LOG2E)


def _cumsum_call(x4, mw, mr, sign):
    B, nh, nr, _ = x4.shape
    return pl.pallas_call(
        functools.partial(_cumsum_kernel, nh=nh, sign=sign), grid=(B,),
        in_specs=[pl.BlockSpec((None, nh, nr, LANES), lambda b: (b, 0, 0, 0)),
                  pl.BlockSpec(mw.shape, lambda b: (0, 0)), pl.BlockSpec(mr.shape, lambda b: (0, 0))],
        out_specs=pl.BlockSpec((None, nh, nr, LANES), lambda b: (b, 0, 0, 0)),
        out_shape=jax.ShapeDtypeStruct(x4.shape, F32),
        compiler_params=_cparams(1), name="logf_cumsum",
    )(x4, mw, mr)


def _merge_kernel(x_ref, oa_ref, obc_ref, obs_ref, obw_ref, oc_ref, misc_ref, zs_ref, w_ref, eg_ref, y_ref):
    g = misc_ref[...]
    g1 = g.astype(BF16)
    g2 = (g - g1.astype(F32)).astype(BF16)

    def gate(c):
        return _dot(g1, eg_ref[c]) + _dot(g2, eg_ref[c])

    ob = gate(0) * obc_ref[...] + gate(1) * obs_ref[...] + gate(2) * obw_ref[...]
    ma = (oa_ref[...] * zs_ref[:, 0:256]).astype(BF16)
    mb = (ob * zs_ref[:, 256:640]).astype(BF16)
    mc = (oc_ref[...] * zs_ref[:, 640:1024]).astype(BF16)
    y_ref[...] = x_ref[...] + _dot(ma, w_ref[0:256, :]) + _dot(mb, w_ref[256:640, :]) + _dot(mc, w_ref[640:1024, :])


def _merge_call(x2, oa, obc, obs, obw, oc, misc, zs, w_out, eg):
    T = x2.shape[0]
    tm = min(512, T)
    row = lambda w: pl.BlockSpec((tm, w), lambda i: (i, 0))
    return pl.pallas_call(
        _merge_kernel, grid=(T // tm,),
        in_specs=[row(1024), row(256), row(384), row(384), row(384), row(384), row(128), row(1024),
                  pl.BlockSpec(w_out.shape, lambda i: (0, 0)), pl.BlockSpec(eg.shape, lambda i: (0, 0, 0))],
        out_specs=row(1024), out_shape=jax.ShapeDtypeStruct((T, 1024), F32),
        compiler_params=_cparams(1), name="merge_outproj",
    )(x2, oa, obc, obs, obw, oc, misc, zs, w_out, eg)


def _gather_rows_kernel(pt_ref, *refs, n):
    out = refs[n]
    for i in range(n):
        out[i] = refs[i][...]


def _gather_logf(cache4, layer, page_table):
    B, NP = page_table.shape
    n = min(16, NP)
    assert NP % n == 0
    w = cache4.shape[-1]
    in_specs = [pl.BlockSpec((None, None, 1, w), functools.partial(
        lambda b, j, pt, i: (layer, pt[b, j * n + i], 0, 0), i=i)) for i in range(n)]
    return pl.pallas_call(
        functools.partial(_gather_rows_kernel, n=n),
        grid_spec=pltpu.PrefetchScalarGridSpec(
            num_scalar_prefetch=1, grid=(B, NP // n), in_specs=in_specs,
            out_specs=pl.BlockSpec((None, n, 1, w), lambda b, j, pt: (b, j, 0, 0))),
        out_shape=jax.ShapeDtypeStruct((B, NP, 1, w), F32),
        compiler_params=_cparams(2), name="gather_logf",
    )(page_table, *([cache4] * n))


def _scmp_kernel(pt_ref, *refs, PG, NP, NBP, P):
    pages = refs[:PG]
    cmpw_ref, new_ref, qb_ref, oc_ref, sel_ref, kcs, vcs = refs[PG:]
    jc = pl.program_id(1)
    ns = pl.num_programs(1)
    nrow = 2 * PG

    @pl.when(jc == 0)
    def _():
        kcs[...] = jnp.zeros(kcs.shape, F32)
        vcs[...] = jnp.zeros(vcs.shape, F32)

    rowi = lax.broadcasted_iota(jnp.int32, (nrow, PAGE), 0)
    coli = lax.broadcasted_iota(jnp.int32, (nrow, PAGE), 1)

    def half_softmax(wl):
        e = jnp.exp(wl - jnp.max(wl, axis=-1, keepdims=True))
        return e / (0.5 * jnp.sum(e, axis=-1, keepdims=True))

    wk = half_softmax(cmpw_ref[0])
    wv = half_softmax(cmpw_ref[1])
    kacc = jnp.zeros((nrow, LANES), F32)
    vacc = jnp.zeros((nrow, LANES), F32)
    for i in range(PG):
        sel = (coli >> 6) + 2 * i == rowi
        pg = pages[i][...]
        kacc = kacc + _dot(jnp.where(sel, wk, 0.0).astype(BF16), pg[:, 0:128].astype(BF16))
        vacc = vacc + _dot(jnp.where(sel, wv, 0.0).astype(BF16), pg[:, 128:256].astype(BF16))
    r0 = pl.multiple_of(jc * nrow, nrow)
    kcs[pl.ds(r0, nrow), :] = kacc
    vcs[pl.ds(r0, nrow), :] = vacc

    @pl.when(jc == ns - 1)
    def _():
        first = (rowi == 0) & (coli < CMP_BLOCK)
        new = new_ref[...]
        kcs[2 * NP:2 * NP + nrow, :] = _dot(jnp.where(first, wk, 0.0).astype(BF16), new[:, 0:128].astype(BF16))
        vcs[2 * NP:2 * NP + nrow, :] = _dot(jnp.where(first, wv, 0.0).astype(BF16), new[:, 128:256].astype(BF16))
        kc = kcs[...].astype(BF16)
        vc = vcs[...].astype(BF16)
        R = qb_ref.shape[0]
        s = _dot_nt(qb_ref[...], kc)
        t_r = lax.broadcasted_iota(jnp.int32, (R, NBP), 0) & 7
        blk = lax.broadcasted_iota(jnp.int32, (R, NBP), 1)
        cm = ((blk + 1) * CMP_BLOCK - 1) <= (P + t_r)
        p = _masked_softmax2(s, cm, -1)
        oc_ref[...] = _dot(p.astype(BF16), vc)
        qpos = P + lax.broadcasted_iota(jnp.int32, (8, NBP), 0)
        blk8 = lax.broadcasted_iota(jnp.int32, (8, NBP), 1)
        qblk = qpos >> 6
        forced = (blk8 == qblk) | (blk8 == qblk - 1) | (blk8 == 0)
        valid = blk8 * CMP_BLOCK <= qpos
        for g in range(G_B):
            imp = sum(p[(r * G_B + g) * 8:(r * G_B + g) * 8 + 8] for r in range(R_B))
            score = jnp.where(valid, imp + jnp.where(forced, FORCE_BONUS, 0.0), -1.0)
            sel_ref[g] = _topk_select(score, blk8, NBP, -1) - 1.0


def _scmp_call(cache_b, layer, page_table, cmpw_s, new_pad, qb_rows, P):
    B, NP = page_table.shape
    PG = 4
    assert NP % PG == 0
    NB = 2 * NP + 1
    NBP = -(-(2 * NP + 2 * PG) // LANES) * LANES
    in_specs = [pl.BlockSpec((None, None, PAGE, 256), functools.partial(
        lambda b, j, pt, i: (layer, pt[b, j * PG + i], 0, 0), i=i)) for i in range(PG)]
    in_specs += [pl.BlockSpec(cmpw_s.shape, lambda b, j, pt: (0, 0, 0)),
                 pl.BlockSpec((None, PAGE, 256), lambda b, j, pt: (b, 0, 0)),
                 pl.BlockSpec((None,) + qb_rows.shape[1:], lambda b, j, pt: (b, 0, 0))]
    R = qb_rows.shape[1]
    del NB
    return pl.pallas_call(
        functools.partial(_scmp_kernel, PG=PG, NP=NP, NBP=NBP, P=P),
        grid_spec=pltpu.PrefetchScalarGridSpec(
            num_scalar_prefetch=1, grid=(B, NP // PG), in_specs=in_specs,
            out_specs=[pl.BlockSpec((None, R, LANES), lambda b, j, pt: (b, 0, 0)),
                       pl.BlockSpec((None, G_B, 8, NBP), lambda b, j, pt: (b, 0, 0, 0))],
            scratch_shapes=[pltpu.VMEM((NBP, LANES), F32), pltpu.VMEM((NBP, LANES), F32)]),
        out_shape=[jax.ShapeDtypeStruct((B, R, LANES), F32), jax.ShapeDtypeStruct((B, G_B, 8, NBP), F32)],
        compiler_params=_cparams(2), name="sample_cmp_topk",
    )(page_table, *([cache_b] * PG), cmpw_s, new_pad, qb_rows)


def _decode_kernel(pt_ref, *refs, PG, kc0, KD, vc0, VD, mode, n_near, lam_init):
    pages = refs[:PG]
    rest = list(refs[PG:])
    qb_ref, knew_ref, vnew_ref, addnew_ref = rest[:4]
    rest = rest[4:]
    bias_ref = ck_ref = cnew_ref = cq_ref = amask_ref = lam_ref = onorm_ref = None
    if mode == "a":
        bias_ref, lam_ref, onorm_ref = rest[:3]
        rest = rest[3:]
    elif mode == "c":
        ck_ref, cnew_ref, cq_ref = rest[:3]
        rest = rest[3:]
    elif mode == "bs":
        bias_ref, amask_ref = rest[:2]
        rest = rest[2:]
    else:
        bias_ref = rest[0]
        rest = rest[1:]
    o_ref, m_s, l_s, acc_s = rest
    jc = pl.program_id(1)
    ns = pl.num_programs(1)
    R = qb_ref.shape[0]

    @pl.when(jc == 0)
    def _():
        m_s[...] = jnp.full(m_s.shape, M_INIT, F32)
        l_s[...] = jnp.zeros(l_s.shape, F32)
        acc_s[...] = jnp.zeros(acc_s.shape, F32)

    def rows_of_heads(c):
        return jnp.concatenate([jnp.broadcast_to(c[h:h + 1, :], (8, c.shape[1])) for h in range(R // 8)], axis=0)

    def update(s, v):
        m = m_s[...]
        m_new = jnp.maximum(m, jnp.max(s, axis=-1, keepdims=True))
        p = jnp.exp2(s - m_new)
        alpha = jnp.exp2(m - m_new)
        l_s[...] = alpha * l_s[...] + jnp.sum(p, axis=-1, keepdims=True)
        acc_s[...] = alpha * acc_s[...] + _dot(p.astype(BF16), v)
        m_s[...] = m_new

    q = qb_ref[...]
    kcat = jnp.concatenate([pg[:, kc0:kc0 + KD].astype(BF16) for pg in pages], axis=0)
    vcat = jnp.concatenate([pg[:, vc0:vc0 + VD].astype(BF16) for pg in pages], axis=0)
    s = _dot_nt(q, kcat)
    if mode == "c":
        s = s + cq_ref[...] - rows_of_heads(ck_ref[...])
    if mode == "bs":
        s = s + amask_ref[...].astype(F32)
    if bias_ref is not None:
        e = jc - (ns - n_near)
        s = s + jnp.where(e >= 0, 1.0, 0.0) * bias_ref[jnp.maximum(e, 0)]
    update(s, vcat)

    @pl.when(jc == ns - 1)
    def _():
        s2 = _dot_nt(q, knew_ref[...]) + addnew_ref[...]
        if mode == "c":
            s2 = s2 + cq_ref[...] - rows_of_heads(cnew_ref[...])
        update(s2, vnew_ref[...])
        o = acc_s[...] / l_s[...]
        if mode == "a":
            lam = _diff_lambda(lam_ref, lam_init)
            half = R // 2
            pd = o[0:half] - lam * o[half:R]
            rowi = lax.broadcasted_iota(jnp.int32, pd.shape, 0)
            lanei = lax.broadcasted_iota(jnp.int32, pd.shape, 1)
            x = jnp.where((rowi >> 3) == (lanei >> 6), pd, 0.0)
            ss = jnp.sum(x * x, axis=-1, keepdims=True)
            y = x * lax.rsqrt(ss * (1.0 / HD) + EPS) * onorm_ref[...] * (1.0 - lam_init)
            o_ref[...] = sum(y[h * 8:(h + 1) * 8] for h in range(H_A))
        elif mode == "c":
            rowi = lax.broadcasted_iota(jnp.int32, o.shape, 0)
            lanei = lax.broadcasted_iota(jnp.int32, o.shape, 1)
            x = jnp.where((rowi >> 3) == (lanei >> 6), o, 0.0)
            o_ref[...] = sum(x[h * 8:(h + 1) * 8] for h in range(H_C))
        else:
            rowi = lax.broadcasted_iota(jnp.int32, o.shape, 0)
            lanei = lax.broadcasted_iota(jnp.int32, o.shape, 1)
            x = jnp.where(((rowi >> 3) & 1) == (lanei >> 6), o, 0.0)
            o_ref[...] = jnp.concatenate(
                [x[(2 * r) * 8:(2 * r) * 8 + 8] + x[(2 * r + 1) * 8:(2 * r + 1) * 8 + 8] for r in range(R_B)], axis=1)


def _decode_call(name, cache, lead, page_table, PG, colblk, kc0, KD, vc0, VD, mode, qb_rows, knew, vnew, addnew,
                 extras, extra_specs, n_near, out_w, lam_init=0.0):
    B, NP = page_table.shape
    assert NP % PG == 0
    R = qb_rows.shape[1]
    bw, bi = colblk
    in_specs = [pl.BlockSpec((None,) * len(lead) + (None, PAGE, bw), functools.partial(
        lambda b, j, pt, i: lead + (pt[b, j * PG + i], 0, bi), i=i)) for i in range(PG)]
    bspec = lambda a: pl.BlockSpec((None,) + a.shape[1:], lambda b, j, pt: (b,) + (0,) * (a.ndim - 1))
    in_specs += [bspec(qb_rows), bspec(knew), bspec(vnew), bspec(addnew)] + extra_specs
    return pl.pallas_call(
        functools.partial(_decode_kernel, PG=PG, kc0=kc0, KD=KD, vc0=vc0, VD=VD, mode=mode, n_near=n_near,
                          lam_init=lam_init),
        grid_spec=pltpu.PrefetchScalarGridSpec(
            num_scalar_prefetch=1, grid=(B, NP // PG), in_specs=in_specs,
            out_specs=pl.BlockSpec((None, 8, out_w), lambda b, j, pt: (b, 0, 0)),
            scratch_shapes=[pltpu.VMEM((R, 1), F32), pltpu.VMEM((R, 1), F32), pltpu.VMEM((R, VD), F32)]),
        out_shape=jax.ShapeDtypeStruct((B, 8, out_w), F32),
        compiler_params=_cparams(2), name=name,
    )(page_table, *([cache] * PG), qb_rows, knew, vnew, addnew, *extras)


def _prep_layer(l, norm_g, w_in, w_out, qk_a, qk_b, qk_c, onorm_a, lam_a, cmp_w, b_f):
    def headcols(base):
        return np.concatenate([np.arange(base + h * HD, base + (h + 1) * HD) for h in SLAB_HEADS])

    cols = np.concatenate([np.arange(0, O_QB), headcols(O_QB), np.arange(O_KVB, O_GB), headcols(O_ZB),
                           np.arange(O_QC, O_FC), np.arange(O_ZC, PROJ_W), np.arange(O_GB, O_GB + 18),
                           np.arange(O_FC, O_FC + H_C)])
    w = jnp.take(w_in[l], jnp.asarray(cols, jnp.int32), axis=1)
    w = jnp.pad(w, ((0, 0), (0, P_END - w.shape[1]))).astype(BF16)
    rows = np.concatenate([np.arange(0, 256), 256 + headcols(0), np.arange(640, 1024)])
    wo = jnp.take(w_out[l], jnp.asarray(rows, jnp.int32), axis=0).astype(BF16)
    tile = lambda v, n: jnp.tile(v.astype(F32), n)[None, :]
    bf_row = jnp.zeros((1, LANES), F32).at[0, 18:18 + H_C].set(b_f[l].astype(F32))
    bd = lambda n, seg: jnp.asarray(np.kron(np.eye(n // seg), np.ones((seg, seg))), BF16)
    eg = np.zeros((3, LANES, 384), np.float32)
    for g in range(G_B):
        for r in range(R_B):
            for c in range(3):
                eg[c, g * 9 + r * 3 + c, (r * G_B + g) * HD:(r * G_B + g + 1) * HD] = 1.0
    return dict(
        w_in=w, w_out=wo, g_in=norm_g[l].astype(F32)[None, :],
        ga_q=tile(qk_a[l, 0], 8) * (DA ** -0.5 * LOG2E), ga_k=tile(qk_a[l, 1], 8),
        gb_q=tile(qk_b[l, 0], 6) * (HD ** -0.5 * LOG2E), gb_k=tile(qk_b[l, 1], 2),
        gc_q=tile(qk_c[l, 0], 6) * (HD ** -0.5 * LOG2E), gc_k=tile(qk_c[l, 1], 6),
        bf_row=bf_row, bd32=bd(256, DA), bd64=bd(384, HD),
        cmpw_in=jnp.tile(cmp_w[l].astype(F32)[:, None, :], (1, 8, 8)),
        cmpw_s=jnp.tile(cmp_w[l].astype(F32)[:, None, :], (1, 8, 2)),
        onorm2=tile(onorm_a[l], 2), onorm4=tile(onorm_a[l], 4), lam=lam_a[l].astype(F32),
        eg=jnp.asarray(eg, BF16),
    )


def _toeplitz(tab_h, rel, mask):
    idx = jnp.asarray(np.clip(rel, 0, tab_h.shape[1] - 1), jnp.int32)
    return jnp.where(jnp.asarray(mask), jnp.take(tab_h, idx, axis=1), NEG)


def _prompt_layer(x3, wp, tabs, lam_init, T):
    B, S, D = x3.shape
    pj = _inproj(x3.reshape(B * S, D), wp, do_cmp=True)
    r3 = lambda a: a.reshape(B, S, a.shape[-1])
    nq = S // T
    o_a = _flash_call(
        functools.partial(_flash_a_kernel, T=T, noff=tabs["noff"], lam_init=lam_init), "flash_a", B, S, T, 2,
        [r3(pj["qa"]), r3(pj["ka"]), r3(pj["va"]), tabs["bias_a"], wp["lam"], wp["onorm2"]],
        [_qspec(T), _kvspec(S, LANES, True), _kvspec(S, LANES, True), _biasspec(tabs["noff"], T),
         _small(wp["lam"]), _small(wp["onorm2"])], 256)
    logf = pj["misc"][:, 18:18 + H_C].reshape(B, S, H_C)
    nr = S // LANES
    lf4 = jnp.swapaxes(logf, 1, 2).reshape(B, H_C, nr, LANES)
    mw = jnp.asarray(np.triu(np.ones((LANES, LANES))), BF16)
    mr = jnp.asarray(np.tril(np.ones((nr, nr)), -1), BF16)
    cf = _cumsum_call(lf4, mw, mr, 1.0).reshape(B, H_C, S)
    ck = cf.reshape(B, H_C, 1, S)
    cq = jnp.pad(jnp.swapaxes(cf.reshape(B, 3, 2, S), 2, 3), ((0, 0), (0, 0), (0, 0), (0, 6)))
    o_c = _flash_call(
        functools.partial(_flash_c_kernel, T=T), "flash_c", B, S, T, 3,
        [r3(pj["qc"]), r3(pj["kc"]), r3(pj["vc"]), ck, cq, tabs["causal"]],
        [_qspec(T), _kvspec(S, LANES, True), _kvspec(S, LANES, True),
         pl.BlockSpec((None, 2, 1, S), lambda b, g, i: (b, g, 0, 0)),
         pl.BlockSpec((None, None, T, 8), lambda b, g, i: (b, g, i, 0)), _small(tabs["causal"])], 384)
    NB = S // CMP_BLOCK
    NBP = -(-NB // LANES) * LANES
    padb = lambda a: jnp.pad(a.reshape(B, NB, LANES), ((0, 0), (0, NBP - NB), (0, 0))).astype(BF16)
    o_bc, selt = _cmp_call(r3(pj["qb"]), padb(pj["kcmp"]), padb(pj["vcmp"]), T)
    sel = jnp.swapaxes(selt, 2, 3)
    sel = jnp.swapaxes(sel, 1, 2).reshape(B, S, G_B * NBP).astype(BF16)
    kaug = jnp.concatenate([r3(pj["ksel"]), jnp.broadcast_to(tabs["blockhot"][None], (B, S, NBP))], axis=-1)
    o_bs = _flash_call(
        functools.partial(_flash_b_kernel, T=T, noff=tabs["noff"], selected=True), "flash_bsel", B, S, T, 3,
        [r3(pj["qb"]), sel, kaug, r3(pj["vsel"]), tabs["bias_b"]],
        [_qspec(T), pl.BlockSpec((None, T, G_B * NBP), lambda b, g, i: (b, i, 0)),
         _kvspec(S, LANES + NBP, False), _kvspec(S, LANES, False), _biasspec(tabs["noff"], T)], 384)
    o_bw = _flash_call(
        functools.partial(_flash_b_kernel, T=T, noff=2, selected=False), "flash_bwin", B, S, T, 3,
        [r3(pj["qb"]), r3(pj["kwin"]), r3(pj["vwin"]), tabs["bias_w"]],
        [_qspec(T), _kvspec(S, LANES, False), _kvspec(S, LANES, False), _biasspec(2, T)], 384)
    f2 = lambda a: a.reshape(B * S, a.shape[-1])
    y = _merge_call(x3.reshape(B * S, D), f2(o_a), f2(o_bc), f2(o_bs), f2(o_bw), f2(o_c), pj["misc"], pj["zs"],
                    wp["w_out"], wp["eg"])
    del nq
    w = min(WINDOW, S)
    return (y.reshape(B, S, D), pj["arow"].reshape(B, S, 2, H_A, HD), pj["kvb"].reshape(B, S, 4, G_B, HD),
            pj["win"].reshape(B, S, 2, G_B, HD)[:, S - w:], pj["crow"].reshape(B, S, 2, H_C, HD), logf)


def _prompt_tables(rel_bias, T, S):
    noff = -(-(REL_MAX_DIST - 1) // T) + 1
    noff = min(noff, S // T)
    nt = noff * T + 1
    bucket = _rel_bucket(jnp.arange(nt, dtype=jnp.int32))
    tab = jnp.take(rel_bias.astype(F32), bucket, axis=0).T
    far = rel_bias.astype(F32)[N_BUCKETS - 1][:, None]
    i = np.arange(T)[:, None]
    j = np.arange(T)[None, :]
    rel = np.stack([d * T + i - j for d in range(noff)])
    tab_s = (tab - far) * LOG2E
    heads_b = np.asarray([H_A + h for h in SLAB_HEADS])
    bias_a = _toeplitz(tab_s[:H_A], rel, rel >= 0)
    bias_b = _toeplitz(tab_s[heads_b], rel, rel >= 0)
    relw = np.stack([d * T + i - j for d in range(2)])
    bias_w = _toeplitz((tab * LOG2E)[heads_b], relw, (relw >= 0) & (relw < WINDOW))
    causal = jnp.asarray(np.where(i >= j, 0.0, NEG), F32)
    NB = S // CMP_BLOCK
    NBP = -(-NB // LANES) * LANES
    hot = (np.arange(S)[:, None] // CMP_BLOCK == np.arange(NBP)[None, :]) * 1e30
    return dict(noff=noff, bias_a=bias_a, bias_b=bias_b, bias_w=bias_w, causal=causal,
                blockhot=jnp.asarray(hot, BF16))


def _sample_tables(rel_bias, P, PG):
    NP = P // PAGE
    n_near_pages = min(NP, -(-REL_MAX_DIST // PAGE))
    assert n_near_pages % PG == 0 or NP == n_near_pages
    nt = n_near_pages * PAGE + 16
    bucket = _rel_bucket(jnp.arange(nt, dtype=jnp.int32))
    rb = rel_bias.astype(F32)
    tab = jnp.take(rb, bucket, axis=0).T
    far = rb[N_BUCKETS - 1][:, None]
    tab_s = (tab - far) * LOG2E
    t = np.arange(8)
    head_a = np.tile(np.repeat(np.arange(H_A), 8), 2)
    t_a = np.tile(t, 2 * H_A)
    head_b = np.repeat(np.asarray(SLAB_HEADS), 8) + H_A
    t_b = np.tile(t, H_B)
    jn = np.arange(n_near_pages * PAGE)

    def near(tabx, heads, tq):
        rel = n_near_pages * PAGE + tq[:, None] - jn[None, :]
        return tabx[jnp.asarray(heads)[:, None], jnp.asarray(np.clip(rel, 0, nt - 1), jnp.int32)]

    def new(tabx, heads, tq):
        jj = np.arange(PAGE)
        rel = tq[:, None] - jj[None, :]
        ok = (rel >= 0) & (jj[None, :] < 8)
        vals = tabx[jnp.asarray(heads)[:, None], jnp.asarray(np.clip(rel, 0, nt - 1), jnp.int32)]
        return jnp.where(jnp.asarray(ok), vals, NEG)

    return dict(n_near_pages=n_near_pages,
                near_a=near(tab_s, head_a, t_a), new_a=new(tab_s, head_a, t_a),
                near_b=near(tab_s, head_b, t_b), new_b=new(tab_s, head_b, t_b),
                tab=tab * LOG2E, head_b=head_b, t_b=t_b, nt=nt)


def _rows_from_lanes(q, bands, width):
    lane = np.arange(q.shape[-1])
    m = np.stack([(lane >= lo) & (lane < lo + width) for lo in bands])
    return jnp.where(jnp.asarray(m)[None, :, None, :], q[:, None], jnp.zeros((), q.dtype)).reshape(
        q.shape[0], len(bands) * 8, q.shape[-1])


def _pad_rows(a, n):
    return jnp.pad(a, ((0, 0), (0, n - a.shape[1]), (0, 0)))


def _sample_layer(l, x3, caches, page_table, wp, stabs, wtabs, lam_init):
    cache_a, cache_b, cache_c, cache_lf4, state_win = caches
    B, S8, D = x3.shape
    NP = page_table.shape[1]
    P = NP * PAGE
    pj = _inproj(x3.reshape(B * S8, D), wp, do_cmp=False)
    r3 = lambda a: a.reshape(B, S8, a.shape[-1])
    PG = min(8, NP)
    n_near = stabs["n_near_pages"] // PG if stabs["n_near_pages"] >= PG else 1
    split_steps = lambda a: jnp.swapaxes(a.reshape(a.shape[0], -1, PG * PAGE), 0, 1)
    bcast = lambda a: jnp.broadcast_to(a[None], (B,) + a.shape)

    qa_rows = _rows_from_lanes(r3(pj["qa"]), [h * HD + mp * DA for mp in range(2) for h in range(H_A)], DA)
    arow = r3(pj["arow"])
    near_a = split_steps(stabs["near_a"])
    o_a = _decode_call(
        "decode_a", cache_a, (l,), page_table, PG, (512, 0), 0, 256, 256, 256, "a", qa_rows,
        _pad_rows(arow[..., 0:256], PAGE).astype(BF16), _pad_rows(arow[..., 256:512], PAGE).astype(BF16),
        bcast(stabs["new_a"]), [near_a, wp["lam"], wp["onorm4"]],
        [pl.BlockSpec(near_a.shape, lambda b, j, pt: (0, 0, 0)), pl.BlockSpec(wp["lam"].shape, lambda b, j, pt: (0, 0)),
         pl.BlockSpec(wp["onorm4"].shape, lambda b, j, pt: (0, 0))], n_near, 256, lam_init)

    lf_pages = _gather_logf(cache_lf4, l, page_table)
    lf4 = jnp.swapaxes(lf_pages.reshape(B, P, H_C), 1, 2).reshape(B, H_C, NP, PAGE)
    msu = jnp.asarray(np.tril(np.ones((LANES, LANES)), -1), BF16)
    mpu = jnp.asarray(np.triu(np.ones((NP, NP)), 1), BF16)
    c_past = _cumsum_call(lf4, msu, mpu, -1.0).reshape(B, H_C, P)
    c_past = jnp.pad(c_past, ((0, 0), (0, 8 - H_C), (0, 0)))
    logf = pj["misc"][:, 18:18 + H_C].reshape(B, S8, H_C)
    lfn = jnp.pad(jnp.swapaxes(logf, 1, 2), ((0, 0), (0, 8 - H_C), (0, LANES - S8)))[:, None]
    mw = jnp.asarray(np.triu(np.ones((LANES, LANES))), BF16)
    c_new = _cumsum_call(lfn, mw, jnp.zeros((8, 8), BF16), 1.0)[:, 0]
    cq_rows = c_new[:, :H_C, :S8].reshape(B, H_C * S8, 1)
    qc_rows = _rows_from_lanes(r3(pj["qc"]), [h * HD for h in range(H_C)], HD)
    crow = r3(pj["crow"])
    jj = np.arange(PAGE)
    tq = np.tile(np.arange(8), H_C)
    causal_new = jnp.asarray(np.where((jj[None, :] <= tq[:, None]) & (jj[None, :] < 8), 0.0, NEG), F32)
    o_c = _decode_call(
        "decode_c", cache_c, (l,), page_table, PG, (768, 0), 0, 384, 384, 384, "c", qc_rows,
        _pad_rows(crow[..., 0:384], PAGE).astype(BF16), _pad_rows(crow[..., 384:768], PAGE).astype(BF16),
        bcast(causal_new), [c_past, c_new, cq_rows],
        [pl.BlockSpec((None, 8, PG * PAGE), lambda b, j, pt: (b, 0, j)),
         pl.BlockSpec((None, 8, LANES), lambda b, j, pt: (b, 0, 0)),
         pl.BlockSpec((None, H_C * S8, 1), lambda b, j, pt: (b, 0, 0))], 0, 384)

    gmask = jnp.asarray(np.arange(LANES)[None, :] // HD == np.arange(G_B)[:, None])
    qb5 = jnp.where(gmask[None, None, None], r3(pj["qb"]).reshape(B, S8, R_B, 1, LANES), jnp.zeros((), BF16))
    qb_rows = jnp.transpose(qb5, (0, 2, 3, 1, 4)).reshape(B, H_B * 8, LANES)
    kvb = r3(pj["kvb"])
    kvb_pad = _pad_rows(kvb, PAGE)
    o_bc_rows, selm1 = _scmp_call(cache_b, l, page_table, wp["cmpw_s"], kvb_pad[..., 0:256], qb_rows, P)
    rowi = np.arange(H_B * 8)
    lanei = np.arange(LANES)
    keep = jnp.asarray(((rowi[:, None] >> 3) & 1) == (lanei[None, :] >> 6))
    x = jnp.where(keep[None], o_bc_rows, 0.0).reshape(B, R_B, G_B, 8, LANES)
    o_bc = jnp.transpose(x[:, :, 0] + x[:, :, 1], (0, 2, 1, 3)).reshape(B, 8, R_B * LANES)
    selrows = jnp.broadcast_to(selm1[:, None], (B, R_B, G_B, 8, selm1.shape[-1])).reshape(B, H_B * 8, -1)
    amask = jnp.repeat(selrows[..., :2 * NP], CMP_BLOCK, axis=-1) * 1e30
    amask_new = jnp.broadcast_to(selrows[..., 2 * NP:2 * NP + 1], (B, H_B * 8, PAGE)) * 1e30
    near_b = split_steps(stabs["near_b"])
    o_bs = _decode_call(
        "decode_bsel", cache_b, (l,), page_table, PG, (256, 1), 0, 128, 128, 128, "bs", qb_rows,
        kvb_pad[..., 256:384].astype(BF16), kvb_pad[..., 384:512].astype(BF16),
        stabs["new_b"][None] + amask_new, [near_b, amask.astype(BF16)],
        [pl.BlockSpec(near_b.shape, lambda b, j, pt: (0, 0, 0)),
         pl.BlockSpec((None, H_B * 8, PG * PAGE), lambda b, j, pt: (b, 0, j))], n_near, 384)
    wb = state_win.shape[2]
    npw = wb // PAGE
    win_pool = state_win.reshape(state_win.shape[0], B * npw, PAGE, 2 * G_B * HD)
    pt_w = jnp.arange(B * npw, dtype=jnp.int32).reshape(B, npw)
    win = r3(pj["win"])
    win_pad = _pad_rows(win, PAGE)
    near_w = wtabs["near_w"][None]
    o_bw = _decode_call(
        "decode_bwin", win_pool, (l,), pt_w, npw, (256, 0), 0, 128, 128, 128, "bw", qb_rows,
        win_pad[..., 0:128].astype(BF16), win_pad[..., 128:256].astype(BF16),
        bcast(wtabs["new_w"]), [near_w], [pl.BlockSpec(near_w.shape, lambda b, j, pt: (0, 0, 0))], 1, 384)

    f2 = lambda a: a.reshape(B * S8, a.shape[-1])
    y = _merge_call(x3.reshape(B * S8, D), f2(o_a), f2(o_bc), f2(o_bs), f2(o_bw), f2(o_c), pj["misc"], pj["zs"],
                    wp["w_out"], wp["eg"])
    win_all = jnp.concatenate([state_win[l].reshape(B, wb, 2 * G_B * HD), win], axis=1)[:, -wb:]
    return (y.reshape(B, S8, D), arow.reshape(B, S8, 2, H_A, HD), kvb.reshape(B, S8, 4, G_B, HD),
            win_all.reshape(B, wb, 2, G_B, HD), crow.reshape(B, S8, 2, H_C, HD), logf)


def _window_tables(stabs, wb):
    jn = np.arange(wb)
    t_b = stabs["t_b"]
    rel = wb + t_b[:, None] - jn[None, :]
    ok = (rel >= 0) & (rel < WINDOW)
    heads = jnp.asarray(stabs["head_b"])[:, None]
    nt = stabs["nt"]
    near_w = jnp.where(jnp.asarray(ok), stabs["tab"][heads, jnp.asarray(np.clip(rel, 0, nt - 1), jnp.int32)], NEG)
    jj = np.arange(PAGE)
    reln = t_b[:, None] - jj[None, :]
    okn = (reln >= 0) & (jj[None, :] < 8)
    new_w = jnp.where(jnp.asarray(okn), stabs["tab"][heads, jnp.asarray(np.clip(reln, 0, nt - 1), jnp.int32)], NEG)
    return dict(near_w=near_w, new_w=new_w)


def kernel(x_prompt, x_sample, cache_a_kv, cache_b_kv, cache_c_kv, cache_c_logf, state_b_win, page_table,
           rel_bias, norm_g, w_in, w_out, qk_a, qk_b, qk_c, onorm_a, lam_a, cmp_w, b_f):
    depth = w_in.shape[0]
    B, S, _ = x_prompt.shape
    DB, S8, _ = x_sample.shape
    assert S8 == 8 and S % 512 == 0 and state_b_win.shape[2] == WINDOW
    T = 512
    NP = page_table.shape[1]
    P = NP * PAGE
    n_pool = cache_a_kv.shape[1]
    ptabs = _prompt_tables(rel_bias, T, S)
    stabs = _sample_tables(rel_bias, P, min(8, NP))
    wtabs = _window_tables(stabs, state_b_win.shape[2])
    caches = (cache_a_kv.reshape(depth, n_pool, PAGE, 2 * H_A * HD),
              cache_b_kv.reshape(depth, n_pool, PAGE, 4 * G_B * HD),
              cache_c_kv.reshape(depth, n_pool, PAGE, 2 * H_C * HD),
              cache_c_logf.reshape(depth, n_pool, 1, PAGE * H_C),
              state_b_win)
    page_table = page_table.astype(jnp.int32)
    yp, ys = x_prompt, x_sample
    outs = [[] for _ in range(10)]
    for l in range(depth):
        lam_init = 0.8 - 0.6 * math.exp(-0.3 * l)
        wp = _prep_layer(l, norm_g, w_in, w_out, qk_a, qk_b, qk_c, onorm_a, lam_a, cmp_w, b_f)
        yp, a_r, b_r, w_r, c_r, l_r = _prompt_layer(yp, wp, ptabs, lam_init, T)
        for k, v in zip((0, 2, 4, 6, 8), (a_r, b_r, w_r, c_r, l_r)):
            outs[k].append(v)
        ys, a_r, b_r, w_r, c_r, l_r = _sample_layer(l, ys, caches, page_table, wp, stabs, wtabs, lam_init)
        for k, v in zip((1, 3, 5, 7, 9), (a_r, b_r, w_r, c_r, l_r)):
            outs[k].append(v)
    return (yp, ys) + tuple(jnp.stack(o) for o in outs)
```

```python
import functools
import math

import numpy as np
import jax
import jax.numpy as jnp
from jax import lax
from jax.experimental import pallas as pl
from jax.experimental.pallas import tpu as pltpu

F32 = jnp.float32
BF16 = jnp.bfloat16

HD = 64
H_A = 4
DA = HD // 2
H_B = 6
G_B = 2
R_B = H_B // G_B
H_C = 6
CMP_BLOCK = 64
TOPK = 16
WINDOW = 512
FORCE_BONUS = float(R_B + 1)
N_BUCKETS = 32
REL_MAX_DIST = 1024
EPS = 1e-6
PAGE = 128

LOG2E = 1.4426950408889634
NEG = -1e30
M_INIT = -0.5e30
LANES = 128
VMEM_LIMIT = 56 * 1024 * 1024

O_QB, O_KVB, O_WIN, O_GB, O_ZB, O_QC, O_FC, O_ZC = 1024, 1408, 1920, 2176, 2194, 2578, 3730, 3736
PROJ_W = 4120
P_QB, P_KVB, P_WIN, P_ZB, P_QC, P_ZC, P_MISC, P_END = 1024, 1408, 1920, 2176, 2560, 3712, 4096, 4224
SLAB_HEADS = [g * R_B + r for r in range(R_B) for g in range(G_B)]


def _cparams(n_grid):
    return pltpu.CompilerParams(dimension_semantics=("arbitrary",) * n_grid, vmem_limit_bytes=VMEM_LIMIT)


def _dot(a, b):
    return jnp.dot(a, b, preferred_element_type=F32)


def _dot_nt(a, b):
    return lax.dot_general(a, b, (((1,), (1,)), ((), ())), preferred_element_type=F32)


def _split3(x):
    x1 = x.astype(BF16)
    r = x - x1.astype(F32)
    x2 = r.astype(BF16)
    x3 = (r - x2.astype(F32)).astype(BF16)
    return x1, x2, x3


def _dot3(x, m):
    x1, x2, x3 = _split3(x)
    return _dot(x1, m) + _dot(x2, m) + _dot(x3, m)


def _dot3_l(m, x):
    x1, x2, x3 = _split3(x)
    return _dot(m, x1) + _dot(m, x2) + _dot(m, x3)


def _rel_bucket(rel):
    n = jnp.maximum(rel, 0)
    max_exact = N_BUCKETS // 2
    nf = jnp.maximum(n, 1).astype(F32)
    large = max_exact + (jnp.log(nf / max_exact) / math.log(REL_MAX_DIST / max_exact)
                         * (N_BUCKETS - max_exact)).astype(jnp.int32)
    return jnp.where(n < max_exact, n, jnp.minimum(large, N_BUCKETS - 1))


def _inproj_kernel(x_ref, w_ref, g_in, ga_q, ga_k, gb_q, gb_k, gc_q, gc_k, bf_row, cmpw_ref, bd32_ref, bd64_ref,
                   qa_o, arow_o, ka_o, va_o, qb_o, kvb_o, ksel_o, vsel_o, kcmp_o, vcmp_o, win_o, kwin_o, vwin_o,
                   qc_o, crow_o, kc_o, vc_o, zs_o, misc_o, *, tm, do_cmp):
    x = x_ref[...]
    ms = jnp.mean(x * x, axis=-1, keepdims=True)
    h = (x * lax.rsqrt(ms + EPS) * g_in[...]).astype(BF16)

    def proj(c0, c1):
        return _dot(h, w_ref[:, c0:c1])

    def segnorm(y, bd, seg):
        ss = _dot((y * y).astype(BF16), bd)
        return y * lax.rsqrt(ss * (1.0 / seg) + EPS)

    def silu(y):
        return y * (1.0 / (1.0 + jnp.exp(-y)))

    bd32 = bd32_ref[...]
    bd64 = bd64_ref[...]
    bd64_1 = bd64[0:128, 0:128]

    y = proj(0, 256)
    qa_o[...] = (segnorm(y, bd32, DA) * ga_q[...]).astype(BF16)
    y = segnorm(proj(256, 512), bd32, DA) * ga_k[...]
    arow_o[:, 0:256] = y
    ka_o[...] = y.astype(BF16)
    y = proj(512, 768)
    arow_o[:, 256:512] = y
    va_o[...] = y.astype(BF16)
    zs_o[:, 0:256] = silu(proj(768, 1024))

    qb_o[...] = (segnorm(proj(P_QB, P_KVB), bd64, HD) * gb_q[...]).astype(BF16)
    kcn = segnorm(proj(P_KVB, P_KVB + 128), bd64_1, HD) * gb_k[...]
    kvb_o[:, 0:128] = kcn
    vcr = proj(P_KVB + 128, P_KVB + 256)
    kvb_o[:, 128:256] = vcr
    y = segnorm(proj(P_KVB + 256, P_KVB + 384), bd64_1, HD) * gb_k[...]
    kvb_o[:, 256:384] = y
    ksel_o[...] = y.astype(BF16)
    y = proj(P_KVB + 384, P_KVB + 512)
    kvb_o[:, 384:512] = y
    vsel_o[...] = y.astype(BF16)
    if do_cmp:
        nbt = tm // CMP_BLOCK
        rowi = lax.broadcasted_iota(jnp.int32, (nbt, tm), 0)
        coli = lax.broadcasted_iota(jnp.int32, (nbt, tm), 1)
        blk = (coli >> 6) == rowi

        def cmp_weights(wl):
            e = jnp.where(blk, jnp.exp(wl - jnp.max(wl, axis=-1, keepdims=True)), 0.0)
            return (e / jnp.sum(e, axis=-1, keepdims=True)).astype(BF16)

        kcmp_o[...] = _dot(cmp_weights(cmpw_ref[0]), kcn.astype(BF16))
        vcmp_o[...] = _dot(cmp_weights(cmpw_ref[1]), vcr.astype(BF16))
    else:
        kcmp_o[...] = jnp.zeros(kcmp_o.shape, F32)
        vcmp_o[...] = jnp.zeros(vcmp_o.shape, F32)
    y = segnorm(proj(P_WIN, P_WIN + 128), bd64_1, HD) * gb_k[...]
    win_o[:, 0:128] = y
    kwin_o[...] = y.astype(BF16)
    y = proj(P_WIN + 128, P_WIN + 256)
    win_o[:, 128:256] = y
    vwin_o[...] = y.astype(BF16)
    zs_o[:, 256:640] = silu(proj(P_ZB, P_QC))

    qc_o[...] = (segnorm(proj(P_QC, P_QC + 384), bd64, HD) * gc_q[...]).astype(BF16)
    y = segnorm(proj(P_QC + 384, P_QC + 768), bd64, HD) * gc_k[...]
    crow_o[:, 0:384] = y
    kc_o[...] = y.astype(BF16)
    y = proj(P_QC + 768, P_ZC)
    crow_o[:, 384:768] = y
    vc_o[...] = y.astype(BF16)
    zs_o[:, 640:1024] = silu(proj(P_ZC, P_MISC))

    y = proj(P_MISC, P_END) + bf_row[...]
    lane = lax.broadcasted_iota(jnp.int32, y.shape, 1)
    sig = 1.0 / (1.0 + jnp.exp(-y))
    lsg = jnp.minimum(y, 0.0) - jnp.log(1.0 + jnp.exp(-jnp.abs(y)))
    misc_o[...] = jnp.where(lane < 18, sig, lsg)


def _inproj(x2, wp, *, do_cmp):
    T = x2.shape[0]
    tm = min(512, T)
    assert T % tm == 0 and (tm == 512 or not do_cmp)
    nbt = 8
    n = T // tm
    full = lambda a: pl.BlockSpec(a.shape, lambda i: (0,) * a.ndim)
    row = lambda w: pl.BlockSpec((tm, w), lambda i: (i, 0))
    consts = [wp["w_in"], wp["g_in"], wp["ga_q"], wp["ga_k"], wp["gb_q"], wp["gb_k"], wp["gc_q"], wp["gc_k"],
              wp["bf_row"], wp["cmpw_in"], wp["bd32"], wp["bd64"]]
    outs = [("qa", 256, BF16), ("arow", 512, F32), ("ka", 256, BF16), ("va", 256, BF16), ("qb", 384, BF16),
            ("kvb", 512, F32), ("ksel", 128, BF16), ("vsel", 128, BF16), ("kcmp", None, F32), ("vcmp", None, F32),
            ("win", 256, F32), ("kwin", 128, BF16), ("vwin", 128, BF16), ("qc", 384, BF16), ("crow", 768, F32),
            ("kc", 384, BF16), ("vc", 384, BF16), ("zs", 1024, F32), ("misc", 128, F32)]
    out_shape, out_specs = [], []
    for _, w, dt in outs:
        if w is None:
            out_shape.append(jax.ShapeDtypeStruct((n * nbt, 128), dt))
            out_specs.append(pl.BlockSpec((nbt, 128), lambda i: (i, 0)))
        else:
            out_shape.append(jax.ShapeDtypeStruct((T, w), dt))
            out_specs.append(row(w))
    res = pl.pallas_call(
        functools.partial(_inproj_kernel, tm=tm, do_cmp=do_cmp),
        grid=(n,),
        in_specs=[row(1024)] + [full(c) for c in consts],
        out_specs=out_specs, out_shape=out_shape,
        compiler_params=_cparams(1), name="inproj",
    )(x2, *consts)
    return {k: v for (k, _, _), v in zip(outs, res)}


def _attend(lhs_list, k_ref, v_ref, T, far_lo, near_lo, hi, near_add=None, all_add=None):
    n = len(lhs_list)

    def step(kt, state, near):
        ks = pl.multiple_of(kt * T, T)
        kk = k_ref[pl.ds(ks, T), :]
        vv = v_ref[pl.ds(ks, T), :]
        out = []
        for i in range(n):
            m, l, acc = state[i]
            s = _dot_nt(lhs_list[i], kk)
            if all_add is not None:
                s = all_add[i](s, ks)
            if near:
                s = near_add[i](s, kt)
            m_new = jnp.maximum(m, jnp.max(s, axis=-1, keepdims=True))
            p = jnp.exp2(s - m_new)
            alpha = jnp.exp2(m - m_new)
            l = alpha * l + jnp.sum(p, axis=-1, keepdims=True)
            acc = alpha * acc + _dot(p.astype(BF16), vv)
            out.append((m_new, l, acc))
        return tuple(out)

    one = (jnp.full((T, 1), M_INIT, F32), jnp.zeros((T, 1), F32), jnp.zeros((T, v_ref.shape[-1]), F32))
    state = (one,) * n
    state = lax.fori_loop(far_lo, near_lo, lambda kt, c: step(kt, c, False), state)
    state = lax.fori_loop(near_lo, hi, lambda kt, c: step(kt, c, True), state)
    return [acc / l for _, l, acc in state]


def _lane_band(x, lo, width):
    lane = lax.broadcasted_iota(jnp.int32, x.shape, 1)
    return jnp.where((lane >= lo) & (lane < lo + width), x, jnp.zeros_like(x))


def _halves(a, b):
    lane = lax.broadcasted_iota(jnp.int32, a.shape, 1)
    return jnp.where(lane < HD, a, b)


def _diff_lambda(lam_ref, lam_init):
    lf = lam_ref[...]
    a = jnp.sum(lf[0:1] * lf[1:2], axis=-1, keepdims=True)
    b = jnp.sum(lf[2:3] * lf[3:4], axis=-1, keepdims=True)
    return jnp.exp(a) - jnp.exp(b) + lam_init


def _flash_a_kernel(q_ref, k_ref, v_ref, bias_ref, lam_ref, onorm_ref, o_ref, *, T, noff, lam_init):
    qi = pl.program_id(2)
    near_lo = jnp.maximum(qi - (noff - 1), 0)
    q = q_ref[...]
    lam = _diff_lambda(lam_ref, lam_init)
    lhs = [_lane_band(q, hl * HD + mp * DA, DA) for hl in range(2) for mp in range(2)]
    add = [functools.partial(lambda s, kt, hl: s + bias_ref[hl, qi - kt], hl=i // 2) for i in range(4)]
    o = _attend(lhs, k_ref, v_ref, T, 0, near_lo, qi + 1, near_add=add)
    o = _halves(o[0] - lam * o[1], o[2] - lam * o[3])
    x2 = o * o
    lane = lax.broadcasted_iota(jnp.int32, o.shape, 1)
    s0 = jnp.sum(jnp.where(lane < HD, x2, 0.0), axis=-1, keepdims=True)
    s1 = jnp.sum(jnp.where(lane < HD, 0.0, x2), axis=-1, keepdims=True)
    ss = jnp.where(lane < HD, s0, s1)
    o_ref[...] = o * lax.rsqrt(ss * (1.0 / HD) + EPS) * onorm_ref[...] * (1.0 - lam_init)


def _flash_c_kernel(q_ref, k_ref, v_ref, ck_ref, cq_ref, mask_ref, o_ref, *, T):
    qi = pl.program_id(2)
    q = q_ref[...]
    lhs = [_lane_band(q, hl * HD, HD) for hl in range(2)]
    cq = [cq_ref[:, hl:hl + 1] for hl in range(2)]
    decay = [functools.partial(lambda s, ks, hl: s + cq[hl] - ck_ref[hl, :, pl.ds(ks, T)], hl=hl) for hl in range(2)]
    causal = [lambda s, kt: s + mask_ref[...]] * 2
    o = _attend(lhs, k_ref, v_ref, T, 0, qi, qi + 1, near_add=causal, all_add=decay)
    o_ref[...] = _halves(o[0], o[1])


def _flash_b_kernel(*refs, T, noff, selected):
    if selected:
        q_ref, sel_ref, k_ref, v_ref, bias_ref, o_ref = refs
    else:
        q_ref, k_ref, v_ref, bias_ref, o_ref = refs
    qi = pl.program_id(2)
    near_lo = jnp.maximum(qi - (noff - 1), 0)
    far_lo = 0 if selected else near_lo
    q = q_ref[...]
    lhs = [_lane_band(q, g * HD, HD) for g in range(G_B)]
    if selected:
        nbp = sel_ref.shape[-1] // G_B
        lhs = [jnp.concatenate([lhs[g], sel_ref[:, g * nbp:(g + 1) * nbp]], axis=1) for g in range(G_B)]
    add = [functools.partial(lambda s, kt, g: s + bias_ref[g, qi - kt], g=g) for g in range(G_B)]
    o = _attend(lhs, k_ref, v_ref, T, far_lo, near_lo, qi + 1, near_add=add)
    o_ref[...] = _halves(o[0], o[1])


def _flash_call(kernel, name, B, S, T, n_grp, ins, in_specs, out_w):
    return pl.pallas_call(
        kernel, grid=(B, n_grp, S // T), in_specs=in_specs,
        out_specs=pl.BlockSpec((None, T, LANES), lambda b, g, i: (b, i, g)),
        out_shape=jax.ShapeDtypeStruct((B, S, out_w), F32),
        compiler_params=_cparams(3), name=name,
    )(*ins)


def _qspec(T):
    return pl.BlockSpec((None, T, LANES), lambda b, g, i: (b, i, g))


def _kvspec(S, w, per_group):
    if per_group:
        return pl.BlockSpec((None, S, w), lambda b, g, i: (b, 0, g))
    return pl.BlockSpec((None, S, w), lambda b, g, i: (b, 0, 0))


def _biasspec(noff, T):
    return pl.BlockSpec((2, noff, T, T), lambda b, g, i: (g, 0, 0, 0))


def _small(a):
    return pl.BlockSpec(a.shape, lambda b, g, i: (0,) * a.ndim)


def _masked_softmax2(s, mask, axis):
    s = jnp.where(mask, s, NEG)
    m = jnp.max(s, axis=axis, keepdims=True)
    p = jnp.where(mask, jnp.exp2(s - m), 0.0)
    return p / jnp.maximum(jnp.sum(p, axis=axis, keepdims=True), 1e-30)


def _topk_select(score, idx, n_idx, axis):
    def it(_, c):
        sc, sel = c
        mx = jnp.max(sc, axis=axis, keepdims=True)
        first = jnp.min(jnp.where(sc == mx, idx, n_idx), axis=axis, keepdims=True)
        oh = idx == first
        sel = jnp.where(oh, jnp.where(mx >= 0.0, 1.0, sel), sel)
        sc = jnp.where(oh, -2.0, sc)
        return sc, sel

    _, sel = lax.fori_loop(0, TOPK, it, (score, jnp.zeros(score.shape, F32)))
    return sel


def _cmp_kernel(q_ref, kc_ref, vc_ref, oc_ref, selt_ref, *, T, NBP):
    qi = pl.program_id(1)
    q = q_ref[...]
    kc = kc_ref[...]
    vc = vc_ref[...]
    qpos_r = qi * T + lax.broadcasted_iota(jnp.int32, (T, NBP), 0)
    blk_c = lax.broadcasted_iota(jnp.int32, (T, NBP), 1)
    cm = ((blk_c + 1) * CMP_BLOCK - 1) <= qpos_r
    qpos_c = qi * T + lax.broadcasted_iota(jnp.int32, (NBP, T), 1)
    blk_r = lax.broadcasted_iota(jnp.int32, (NBP, T), 0)
    cmt = ((blk_r + 1) * CMP_BLOCK - 1) <= qpos_c
    imp = [jnp.zeros((NBP, T), F32) for _ in range(G_B)]
    for r in range(R_B):
        slab = q[:, r * LANES:(r + 1) * LANES]
        outs = []
        for g in range(G_B):
            qm = _lane_band(slab, g * HD, HD)
            p = _masked_softmax2(_dot_nt(qm, kc), cm, -1)
            outs.append(_dot(p.astype(BF16), vc))
            imp[g] = imp[g] + _masked_softmax2(_dot_nt(kc, qm), cmt, 0)
        oc_ref[:, r * LANES:(r + 1) * LANES] = _halves(outs[0], outs[1])
    qblk = qpos_c >> 6
    forced = (blk_r == qblk) | (blk_r == qblk - 1) | (blk_r == 0)
    valid = blk_r * CMP_BLOCK <= qpos_c
    for g in range(G_B):
        score = jnp.where(valid, imp[g] + jnp.where(forced, FORCE_BONUS, 0.0), -1.0)
        selt_ref[g] = _topk_select(score, blk_r, NBP, 0) - 1.0


def _cmp_call(qb3, kcmp3, vcmp3, T):
    B, S, _ = qb3.shape
    NBP = kcmp3.shape[1]
    return pl.pallas_call(
        functools.partial(_cmp_kernel, T=T, NBP=NBP), grid=(B, S // T),
        in_specs=[pl.BlockSpec((None, T, 384), lambda b, i: (b, i, 0)),
                  pl.BlockSpec((None, NBP, LANES), lambda b, i: (b, 0, 0)),
                  pl.BlockSpec((None, NBP, LANES), lambda b, i: (b, 0, 0))],
        out_specs=[pl.BlockSpec((None, T, 384), lambda b, i: (b, i, 0)),
                   pl.BlockSpec((None, G_B, NBP, T), lambda b, i: (b, 0, 0, i))],
        out_shape=[jax.ShapeDtypeStruct((B, S, 384), F32), jax.ShapeDtypeStruct((B, G_B, NBP, S), F32)],
        compiler_params=_cparams(2), name="nsa_cmp_topk",
    )(qb3, kcmp3, vcmp3)


def _cumsum_kernel(x_ref, mw_ref, mr_ref, o_ref, *, nh, sign):
    mw = mw_ref[...]
    mr = mr_ref[...]
    ones = jnp.ones((LANES, LANES), BF16)
    for h in range(nh):
        x = x_ref[h]
        tot = _dot3(x, ones)
        o_ref[h] = (_dot3(x, mw) + _dot3_l(mr, tot)) * (sign * LOG2E)


def _cumsum_call(x4, mw, mr, sign):
    B, nh, nr, _ = x4.shape
    return pl.pallas_call(
        functools.partial(_cumsum_kernel, nh=nh, sign=sign), grid=(B,),
        in_specs=[pl.BlockSpec((None, nh, nr, LANES), lambda b: (b, 0, 0, 0)),
                  pl.BlockSpec(mw.shape, lambda b: (0, 0)), pl.BlockSpec(mr.shape, lambda b: (0, 0))],
        out_specs=pl.BlockSpec((None, nh, nr, LANES), lambda b: (b, 0, 0, 0)),
        out_shape=jax.ShapeDtypeStruct(x4.shape, F32),
        compiler_params=_cparams(1), name="logf_cumsum",
    )(x4, mw, mr)


def _merge_kernel(x_ref, oa_ref, obc_ref, obs_ref, obw_ref, oc_ref, misc_ref, zs_ref, w_ref, eg_ref, y_ref):
    g = misc_ref[...]
    g1 = g.astype(BF16)
    g2 = (g - g1.astype(F32)).astype(BF16)

    def gate(c):
        return _dot(g1, eg_ref[c]) + _dot(g2, eg_ref[c])

    ob = gate(0) * obc_ref[...] + gate(1) * obs_ref[...] + gate(2) * obw_ref[...]
    ma = (oa_ref[...] * zs_ref[:, 0:256]).astype(BF16)
    mb = (ob * zs_ref[:, 256:640]).astype(BF16)
    mc = (oc_ref[...] * zs_ref[:, 640:1024]).astype(BF16)
    y_ref[...] = x_ref[...] + _dot(ma, w_ref[0:256, :]) + _dot(mb, w_ref[256:640, :]) + _dot(mc, w_ref[640:1024, :])


def _merge_call(x2, oa, obc, obs, obw, oc, misc, zs, w_out, eg):
    T = x2.shape[0]
    tm = min(512, T)
    row = lambda w: pl.BlockSpec((tm, w), lambda i: (i, 0))
    return pl.pallas_call(
        _merge_kernel, grid=(T // tm,),
        in_specs=[row(1024), row(256), row(384), row(384), row(384), row(384), row(128), row(1024),
                  pl.BlockSpec(w_out.shape, lambda i: (0, 0)), pl.BlockSpec(eg.shape, lambda i: (0, 0, 0))],
        out_specs=row(1024), out_shape=jax.ShapeDtypeStruct((T, 1024), F32),
        compiler_params=_cparams(1), name="merge_outproj",
    )(x2, oa, obc, obs, obw, oc, misc, zs, w_out, eg)


def _gather_rows_kernel(pt_ref, *refs, n):
    out = refs[n]
    for i in range(n):
        out[i] = refs[i][...]


def _gather_logf(cache4, layer, page_table):
    B, NP = page_table.shape
    n = min(16, NP)
    assert NP % n == 0
    w = cache4.shape[-1]
    in_specs = [pl.BlockSpec((None, None, 1, w), functools.partial(
        lambda b, j, pt, i: (layer, pt[b, j * n + i], 0, 0), i=i)) for i in range(n)]
    return pl.pallas_call(
        functools.partial(_gather_rows_kernel, n=n),
        grid_spec=pltpu.PrefetchScalarGridSpec(
            num_scalar_prefetch=1, grid=(B, NP // n), in_specs=in_specs,
            out_specs=pl.BlockSpec((None, n, 1, w), lambda b, j, pt: (b, j, 0, 0))),
        out_shape=jax.ShapeDtypeStruct((B, NP, 1, w), F32),
        compiler_params=_cparams(2), name="gather_logf",
    )(page_table, *([cache4] * n))


def _scmp_kernel(pt_ref, *refs, PG, NP, NBP, P):
    pages = refs[:PG]
    cmpw_ref, new_ref, qb_ref, oc_ref, sel_ref, kcs, vcs = refs[PG:]
    jc = pl.program_id(1)
    ns = pl.num_programs(1)
    nrow = 2 * PG

    @pl.when(jc == 0)
    def _():
        kcs[...] = jnp.zeros(kcs.shape, F32)
        vcs[...] = jnp.zeros(vcs.shape, F32)

    rowi = lax.broadcasted_iota(jnp.int32, (nrow, PAGE), 0)
    coli = lax.broadcasted_iota(jnp.int32, (nrow, PAGE), 1)

    def half_softmax(wl):
        e = jnp.exp(wl - jnp.max(wl, axis=-1, keepdims=True))
        return e / (0.5 * jnp.sum(e, axis=-1, keepdims=True))

    wk = half_softmax(cmpw_ref[0])
    wv = half_softmax(cmpw_ref[1])
    kacc = jnp.zeros((nrow, LANES), F32)
    vacc = jnp.zeros((nrow, LANES), F32)
    for i in range(PG):
        sel = (coli >> 6) + 2 * i == rowi
        pg = pages[i][...]
        kacc = kacc + _dot_nt(jnp.where(sel, wk, 0.0).astype(BF16), pg[0:128, :].astype(BF16))
        vacc = vacc + _dot_nt(jnp.where(sel, wv, 0.0).astype(BF16), pg[128:256, :].astype(BF16))
    r0 = pl.multiple_of(jc * nrow, nrow)
    kcs[pl.ds(r0, nrow), :] = kacc
    vcs[pl.ds(r0, nrow), :] = vacc

    @pl.when(jc == ns - 1)
    def _():
        first = (rowi == 0) & (coli < CMP_BLOCK)
        new = new_ref[...]
        kcs[2 * NP:2 * NP + nrow, :] = _dot(jnp.where(first, wk, 0.0).astype(BF16), new[:, 0:128].astype(BF16))
        vcs[2 * NP:2 * NP + nrow, :] = _dot(jnp.where(first, wv, 0.0).astype(BF16), new[:, 128:256].astype(BF16))
        kc = kcs[...].astype(BF16)
        vc = vcs[...].astype(BF16)
        R = qb_ref.shape[0]
        s = _dot_nt(qb_ref[...], kc)
        t_r = lax.broadcasted_iota(jnp.int32, (R, NBP), 0) & 7
        blk = lax.broadcasted_iota(jnp.int32, (R, NBP), 1)
        cm = ((blk + 1) * CMP_BLOCK - 1) <= (P + t_r)
        p = _masked_softmax2(s, cm, -1)
        oc_ref[...] = _dot(p.astype(BF16), vc)
        qpos = P + lax.broadcasted_iota(jnp.int32, (8, NBP), 0)
        blk8 = lax.broadcasted_iota(jnp.int32, (8, NBP), 1)
        qblk = qpos >> 6
        forced = (blk8 == qblk) | (blk8 == qblk - 1) | (blk8 == 0)
        valid = blk8 * CMP_BLOCK <= qpos
        for g in range(G_B):
            imp = sum(p[(r * G_B + g) * 8:(r * G_B + g) * 8 + 8] for r in range(R_B))
            score = jnp.where(valid, imp + jnp.where(forced, FORCE_BONUS, 0.0), -1.0)
            sel_ref[g] = _topk_select(score, blk8, NBP, -1) - 1.0


def _scmp_call(cache_b, layer, page_table, cmpw_s, new_pad, qb_rows, P):
    B, NP = page_table.shape
    PG = 4
    assert NP % PG == 0
    NB = 2 * NP + 1
    NBP = -(-(2 * NP + 2 * PG) // LANES) * LANES
    in_specs = [pl.BlockSpec((None, None, 256, PAGE), functools.partial(
        lambda b, j, pt, i: (layer, pt[b, j * PG + i], 0, 0), i=i)) for i in range(PG)]
    in_specs += [pl.BlockSpec(cmpw_s.shape, lambda b, j, pt: (0, 0, 0)),
                 pl.BlockSpec((None, PAGE, 256), lambda b, j, pt: (b, 0, 0)),
                 pl.BlockSpec((None,) + qb_rows.shape[1:], lambda b, j, pt: (b, 0, 0))]
    R = qb_rows.shape[1]
    del NB
    return pl.pallas_call(
        functools.partial(_scmp_kernel, PG=PG, NP=NP, NBP=NBP, P=P),
        grid_spec=pltpu.PrefetchScalarGridSpec(
            num_scalar_prefetch=1, grid=(B, NP // PG), in_specs=in_specs,
            out_specs=[pl.BlockSpec((None, R, LANES), lambda b, j, pt: (b, 0, 0)),
                       pl.BlockSpec((None, G_B, 8, NBP), lambda b, j, pt: (b, 0, 0, 0))],
            scratch_shapes=[pltpu.VMEM((NBP, LANES), F32), pltpu.VMEM((NBP, LANES), F32)]),
        out_shape=[jax.ShapeDtypeStruct((B, R, LANES), F32), jax.ShapeDtypeStruct((B, G_B, 8, NBP), F32)],
        compiler_params=_cparams(2), name="sample_cmp_topk",
    )(page_table, *([cache_b] * PG), cmpw_s, new_pad, qb_rows)


def _decode_kernel(pt_ref, *refs, PG, kc0, KD, vc0, VD, mode, n_near, lam_init):
    pages = refs[:PG]
    rest = list(refs[PG:])
    qb_ref, knew_ref, vnew_ref, addnew_ref = rest[:4]
    rest = rest[4:]
    bias_ref = ck_ref = cnew_ref = cq_ref = amask_ref = lam_ref = onorm_ref = None
    if mode == "a":
        bias_ref, lam_ref, onorm_ref = rest[:3]
        rest = rest[3:]
    elif mode == "c":
        ck_ref, cnew_ref, cq_ref = rest[:3]
        rest = rest[3:]
    elif mode == "bs":
        bias_ref, amask_ref = rest[:2]
        rest = rest[2:]
    else:
        bias_ref = rest[0]
        rest = rest[1:]
    o_ref, m_s, l_s, acc_s = rest
    jc = pl.program_id(1)
    ns = pl.num_programs(1)
    R = qb_ref.shape[0]

    @pl.when(jc == 0)
    def _():
        m_s[...] = jnp.full(m_s.shape, M_INIT, F32)
        l_s[...] = jnp.zeros(l_s.shape, F32)
        acc_s[...] = jnp.zeros(acc_s.shape, F32)

    def rows_of_heads(c):
        return jnp.concatenate([jnp.broadcast_to(c[h:h + 1, :], (8, c.shape[1])) for h in range(R // 8)], axis=0)

    def update(s, v, v_feature_major):
        m = m_s[...]
        m_new = jnp.maximum(m, jnp.max(s, axis=-1, keepdims=True))
        p = jnp.exp2(s - m_new)
        alpha = jnp.exp2(m - m_new)
        l_s[...] = alpha * l_s[...] + jnp.sum(p, axis=-1, keepdims=True)
        pb = p.astype(BF16)
        acc_s[...] = alpha * acc_s[...] + (_dot_nt(pb, v) if v_feature_major else _dot(pb, v))
        m_s[...] = m_new

    q = qb_ref[...]
    kcat = jnp.concatenate([pg[kc0:kc0 + KD, :].astype(BF16) for pg in pages], axis=1)
    vcat = jnp.concatenate([pg[vc0:vc0 + VD, :].astype(BF16) for pg in pages], axis=1)
    s = _dot(q, kcat)
    if mode == "c":
        s = s + cq_ref[...] - rows_of_heads(ck_ref[...])
    if mode == "bs":
        s = s + amask_ref[...].astype(F32)
    if bias_ref is not None:
        e = jc - (ns - n_near)
        s = s + jnp.where(e >= 0, 1.0, 0.0) * bias_ref[jnp.maximum(e, 0)]
    update(s, vcat, True)

    @pl.when(jc == ns - 1)
    def _():
        s2 = _dot_nt(q, knew_ref[...]) + addnew_ref[...]
        if mode == "c":
            s2 = s2 + cq_ref[...] - rows_of_heads(cnew_ref[...])
        update(s2, vnew_ref[...], False)
        o = acc_s[...] / l_s[...]
        if mode == "a":
            lam = _diff_lambda(lam_ref, lam_init)
            half = R // 2
            pd = o[0:half] - lam * o[half:R]
            rowi = lax.broadcasted_iota(jnp.int32, pd.shape, 0)
            lanei = lax.broadcasted_iota(jnp.int32, pd.shape, 1)
            x = jnp.where((rowi >> 3) == (lanei >> 6), pd, 0.0)
            ss = jnp.sum(x * x, axis=-1, keepdims=True)
            y = x * lax.rsqrt(ss * (1.0 / HD) + EPS) * onorm_ref[...] * (1.0 - lam_init)
            o_ref[...] = sum(y[h * 8:(h + 1) * 8] for h in range(H_A))
        elif mode == "c":
            rowi = lax.broadcasted_iota(jnp.int32, o.shape, 0)
            lanei = lax.broadcasted_iota(jnp.int32, o.shape, 1)
            x = jnp.where((rowi >> 3) == (lanei >> 6), o, 0.0)
            o_ref[...] = sum(x[h * 8:(h + 1) * 8] for h in range(H_C))
        else:
            rowi = lax.broadcasted_iota(jnp.int32, o.shape, 0)
            lanei = lax.broadcasted_iota(jnp.int32, o.shape, 1)
            x = jnp.where(((rowi >> 3) & 1) == (lanei >> 6), o, 0.0)
            o_ref[...] = jnp.concatenate(
                [x[(2 * r) * 8:(2 * r) * 8 + 8] + x[(2 * r + 1) * 8:(2 * r + 1) * 8 + 8] for r in range(R_B)], axis=1)


def _decode_call(name, cache, page_maps, page_table, rowblk, kc0, KD, vc0, VD, mode, qb_rows, knew, vnew, addnew,
                 extras, extra_specs, n_near, out_w, lam_init=0.0):
    B, NP = page_table.shape
    PG = len(page_maps)
    assert NP % PG == 0
    R = qb_rows.shape[1]
    bh, bi = rowblk

    def page_spec(f):
        def index(b, j, pt):
            d0, d1, lane_blk = f(b, j, pt)
            return (d0, d1, bi, lane_blk)
        return pl.BlockSpec((None, None, bh, PAGE), index)

    in_specs = [page_spec(f) for f in page_maps]
    bspec = lambda a: pl.BlockSpec((None,) + a.shape[1:], lambda b, j, pt: (b,) + (0,) * (a.ndim - 1))
    in_specs += [bspec(qb_rows), bspec(knew), bspec(vnew), bspec(addnew)] + extra_specs
    return pl.pallas_call(
        functools.partial(_decode_kernel, PG=PG, kc0=kc0, KD=KD, vc0=vc0, VD=VD, mode=mode, n_near=n_near,
                          lam_init=lam_init),
        grid_spec=pltpu.PrefetchScalarGridSpec(
            num_scalar_prefetch=1, grid=(B, NP // PG), in_specs=in_specs,
            out_specs=pl.BlockSpec((None, 8, out_w), lambda b, j, pt: (b, 0, 0)),
            scratch_shapes=[pltpu.VMEM((R, 1), F32), pltpu.VMEM((R, 1), F32), pltpu.VMEM((R, VD), F32)]),
        out_shape=jax.ShapeDtypeStruct((B, 8, out_w), F32),
        compiler_params=_cparams(2), name=name,
    )(page_table, *([cache] * PG), qb_rows, knew, vnew, addnew, *extras)


def _prep_layer(l, norm_g, w_in, w_out, qk_a, qk_b, qk_c, onorm_a, lam_a, cmp_w, b_f):
    wl = w_in[l]
    slab_cols = lambda base: [wl[:, base + h * HD:base + (h + 1) * HD] for h in SLAB_HEADS]
    w = jnp.concatenate(
        [wl[:, 0:O_QB]] + slab_cols(O_QB) + [wl[:, O_KVB:O_GB]] + slab_cols(O_ZB)
        + [wl[:, O_QC:O_FC], wl[:, O_ZC:PROJ_W], wl[:, O_GB:O_GB + 18], wl[:, O_FC:O_FC + H_C],
           jnp.zeros((wl.shape[0], P_END - PROJ_W), wl.dtype)], axis=1).astype(BF16)
    wol = w_out[l]
    wo = jnp.concatenate([wol[0:256]] + [wol[256 + h * HD:256 + (h + 1) * HD] for h in SLAB_HEADS]
                         + [wol[640:1024]], axis=0).astype(BF16)
    tile = lambda v, n: jnp.tile(v.astype(F32), n)[None, :]
    bf_row = jnp.zeros((1, LANES), F32).at[0, 18:18 + H_C].set(b_f[l].astype(F32))
    bd = lambda n, seg: jnp.asarray(np.kron(np.eye(n // seg), np.ones((seg, seg))), BF16)
    eg = np.zeros((3, LANES, 384), np.float32)
    for g in range(G_B):
        for r in range(R_B):
            for c in range(3):
                eg[c, g * 9 + r * 3 + c, (r * G_B + g) * HD:(r * G_B + g + 1) * HD] = 1.0
    return dict(
        w_in=w, w_out=wo, g_in=norm_g[l].astype(F32)[None, :],
        ga_q=tile(qk_a[l, 0], 8) * (DA ** -0.5 * LOG2E), ga_k=tile(qk_a[l, 1], 8),
        gb_q=tile(qk_b[l, 0], 6) * (HD ** -0.5 * LOG2E), gb_k=tile(qk_b[l, 1], 2),
        gc_q=tile(qk_c[l, 0], 6) * (HD ** -0.5 * LOG2E), gc_k=tile(qk_c[l, 1], 6),
        bf_row=bf_row, bd32=bd(256, DA), bd64=bd(384, HD),
        cmpw_in=jnp.tile(cmp_w[l].astype(F32)[:, None, :], (1, 8, 8)),
        cmpw_s=jnp.tile(cmp_w[l].astype(F32)[:, None, :], (1, 8, 2)),
        onorm2=tile(onorm_a[l], 2), onorm4=tile(onorm_a[l], 4), lam=lam_a[l].astype(F32),
        eg=jnp.asarray(eg, BF16),
    )


def _toeplitz(tab_h, noff, T, mask):
    H, nt = tab_h.shape
    seg = 2 * T - 1
    padded = jnp.pad(tab_h, ((0, 0), (T - 1, max(0, noff * T - nt))))
    tiles = []
    for d in range(noff):
        u = padded[:, d * T:d * T + seg][:, ::-1]
        flat = jnp.broadcast_to(u[:, None, :], (H, T, seg)).reshape(H, T * seg)
        tiles.append(flat[:, T - 1:T - 1 + T * (seg - 1)].reshape(H, T, seg - 1)[:, :, :T])
    return jnp.where(jnp.asarray(mask)[None], jnp.stack(tiles, axis=1), NEG)


def _shifted_rows(tab_h, width, n_t):
    return jnp.stack([tab_h[:, t + 1:t + 1 + width][:, ::-1] for t in range(n_t)], axis=1)


def _prompt_layer(x3, wp, tabs, lam_init, T):
    B, S, D = x3.shape
    pj = _inproj(x3.reshape(B * S, D), wp, do_cmp=True)
    r3 = lambda a: a.reshape(B, S, a.shape[-1])
    nq = S // T
    o_a = _flash_call(
        functools.partial(_flash_a_kernel, T=T, noff=tabs["noff"], lam_init=lam_init), "flash_a", B, S, T, 2,
        [r3(pj["qa"]), r3(pj["ka"]), r3(pj["va"]), tabs["bias_a"], wp["lam"], wp["onorm2"]],
        [_qspec(T), _kvspec(S, LANES, True), _kvspec(S, LANES, True), _biasspec(tabs["noff"], T),
         _small(wp["lam"]), _small(wp["onorm2"])], 256)
    logf = pj["misc"][:, 18:18 + H_C].reshape(B, S, H_C)
    nr = S // LANES
    lf4 = jnp.swapaxes(logf, 1, 2).reshape(B, H_C, nr, LANES)
    mw = jnp.asarray(np.triu(np.ones((LANES, LANES))), BF16)
    mr = jnp.asarray(np.tril(np.ones((nr, nr)), -1), BF16)
    cf = _cumsum_call(lf4, mw, mr, 1.0).reshape(B, H_C, S)
    ck = cf.reshape(B, H_C, 1, S)
    cq = jnp.pad(jnp.swapaxes(cf.reshape(B, 3, 2, S), 2, 3), ((0, 0), (0, 0), (0, 0), (0, 6)))
    o_c = _flash_call(
        functools.partial(_flash_c_kernel, T=T), "flash_c", B, S, T, 3,
        [r3(pj["qc"]), r3(pj["kc"]), r3(pj["vc"]), ck, cq, tabs["causal"]],
        [_qspec(T), _kvspec(S, LANES, True), _kvspec(S, LANES, True),
         pl.BlockSpec((None, 2, 1, S), lambda b, g, i: (b, g, 0, 0)),
         pl.BlockSpec((None, None, T, 8), lambda b, g, i: (b, g, i, 0)), _small(tabs["causal"])], 384)
    NB = S // CMP_BLOCK
    NBP = -(-NB // LANES) * LANES
    padb = lambda a: jnp.pad(a.reshape(B, NB, LANES), ((0, 0), (0, NBP - NB), (0, 0))).astype(BF16)
    o_bc, selt = _cmp_call(r3(pj["qb"]), padb(pj["kcmp"]), padb(pj["vcmp"]), T)
    sel = jnp.swapaxes(selt, 2, 3)
    sel = jnp.swapaxes(sel, 1, 2).reshape(B, S, G_B * NBP).astype(BF16)
    kaug = jnp.concatenate([r3(pj["ksel"]), jnp.broadcast_to(tabs["blockhot"][None], (B, S, NBP))], axis=-1)
    o_bs = _flash_call(
        functools.partial(_flash_b_kernel, T=T, noff=tabs["noff"], selected=True), "flash_bsel", B, S, T, 3,
        [r3(pj["qb"]), sel, kaug, r3(pj["vsel"]), tabs["bias_b"]],
        [_qspec(T), pl.BlockSpec((None, T, G_B * NBP), lambda b, g, i: (b, i, 0)),
         _kvspec(S, LANES + NBP, False), _kvspec(S, LANES, False), _biasspec(tabs["noff"], T)], 384)
    o_bw = _flash_call(
        functools.partial(_flash_b_kernel, T=T, noff=2, selected=False), "flash_bwin", B, S, T, 3,
        [r3(pj["qb"]), r3(pj["kwin"]), r3(pj["vwin"]), tabs["bias_w"]],
        [_qspec(T), _kvspec(S, LANES, False), _kvspec(S, LANES, False), _biasspec(2, T)], 384)
    f2 = lambda a: a.reshape(B * S, a.shape[-1])
    y = _merge_call(x3.reshape(B * S, D), f2(o_a), f2(o_bc), f2(o_bs), f2(o_bw), f2(o_c), pj["misc"], pj["zs"],
                    wp["w_out"], wp["eg"])
    del nq
    w = min(WINDOW, S)
    return (y.reshape(B, S, D), pj["arow"].reshape(B, S, 2, H_A, HD), pj["kvb"].reshape(B, S, 4, G_B, HD),
            pj["win"].reshape(B, S, 2, G_B, HD)[:, S - w:], pj["crow"].reshape(B, S, 2, H_C, HD), logf)


def _prompt_tables(rel_bias, T, S):
    noff = -(-(REL_MAX_DIST - 1) // T) + 1
    noff = min(noff, S // T)
    nt = noff * T + 1
    bucket = _rel_bucket(jnp.arange(nt, dtype=jnp.int32))
    tab = jnp.take(rel_bias.astype(F32), bucket, axis=0).T
    far = rel_bias.astype(F32)[N_BUCKETS - 1][:, None]
    i = np.arange(T)[:, None]
    j = np.arange(T)[None, :]
    rel = np.stack([d * T + i - j for d in range(noff)])
    tab_s = (tab - far) * LOG2E
    heads_b = np.asarray([H_A + h for h in SLAB_HEADS])
    tab_b = jnp.stack([tab_s[h] for h in heads_b])
    tab_w = jnp.stack([tab[h] for h in heads_b]) * LOG2E
    bias_a = _toeplitz(tab_s[:H_A], noff, T, rel >= 0)
    bias_b = _toeplitz(tab_b, noff, T, rel >= 0)
    relw = np.stack([d * T + i - j for d in range(2)])
    bias_w = _toeplitz(tab_w, 2, T, (relw >= 0) & (relw < WINDOW))
    causal = jnp.asarray(np.where(i >= j, 0.0, NEG), F32)
    NB = S // CMP_BLOCK
    NBP = -(-NB // LANES) * LANES
    hot = (np.arange(S)[:, None] // CMP_BLOCK == np.arange(NBP)[None, :]) * 1e30
    return dict(noff=noff, bias_a=bias_a, bias_b=bias_b, bias_w=bias_w, causal=causal,
                blockhot=jnp.asarray(hot, BF16))


def _sample_tables(rel_bias, P, PG):
    NP = P // PAGE
    n_near_pages = min(NP, -(-REL_MAX_DIST // PAGE))
    assert n_near_pages % PG == 0 or NP == n_near_pages
    nt = n_near_pages * PAGE + 16
    bucket = _rel_bucket(jnp.arange(nt, dtype=jnp.int32))
    rb = rel_bias.astype(F32)
    tab = jnp.take(rb, bucket, axis=0).T
    far = rb[N_BUCKETS - 1][:, None]
    tab_s = (tab - far) * LOG2E
    tab_a = tab_s[:H_A]
    tab_b = jnp.stack([tab_s[H_A + h] for h in SLAB_HEADS])
    wn = n_near_pages * PAGE
    flat = lambda a: a.reshape(a.shape[0] * a.shape[1], a.shape[2])
    near_a = flat(_shifted_rows(tab_a, wn, 8))
    new_a = flat(_new_rows(tab_a))
    return dict(n_near_pages=n_near_pages,
                near_a=jnp.concatenate([near_a, near_a], axis=0), new_a=jnp.concatenate([new_a, new_a], axis=0),
                near_b=flat(_shifted_rows(tab_b, wn, 8)), new_b=flat(_new_rows(tab_b)),
                tab_w=jnp.stack([tab[H_A + h] for h in SLAB_HEADS]) * LOG2E)


def _new_rows(tab_h):
    rows = [jnp.pad(tab_h[:, :t + 1][:, ::-1], ((0, 0), (0, PAGE - t - 1)), constant_values=NEG) for t in range(8)]
    return jnp.stack(rows, axis=1)


def _rows_from_lanes(q, bands, width):
    lane = np.arange(q.shape[-1])
    m = np.stack([(lane >= lo) & (lane < lo + width) for lo in bands])
    return jnp.where(jnp.asarray(m)[None, :, None, :], q[:, None], jnp.zeros((), q.dtype)).reshape(
        q.shape[0], len(bands) * 8, q.shape[-1])


def _pad_rows(a, n):
    return jnp.pad(a, ((0, 0), (0, n - a.shape[1]), (0, 0)))


def _sample_layer(l, x3, caches, page_table, wp, stabs, wtabs, lam_init):
    cache_a, cache_b, cache_c, cache_lf4, state_win = caches
    B, S8, D = x3.shape
    NP = page_table.shape[1]
    P = NP * PAGE
    pj = _inproj(x3.reshape(B * S8, D), wp, do_cmp=False)
    r3 = lambda a: a.reshape(B, S8, a.shape[-1])
    PG = min(8, NP)
    n_near = stabs["n_near_pages"] // PG if stabs["n_near_pages"] >= PG else 1
    split_steps = lambda a: jnp.swapaxes(a.reshape(a.shape[0], -1, PG * PAGE), 0, 1)
    bcast = lambda a: jnp.broadcast_to(a[None], (B,) + a.shape)
    pool_maps = [functools.partial(lambda b, j, pt, i: (l, pt[b, j * PG + i], 0), i=i) for i in range(PG)]

    qa_rows = _rows_from_lanes(r3(pj["qa"]), [h * HD + mp * DA for mp in range(2) for h in range(H_A)], DA)
    arow = r3(pj["arow"])
    near_a = split_steps(stabs["near_a"])
    o_a = _decode_call(
        "decode_a", cache_a, pool_maps, page_table, (512, 0), 0, 256, 256, 256, "a", qa_rows,
        _pad_rows(arow[..., 0:256], PAGE).astype(BF16), _pad_rows(arow[..., 256:512], PAGE).astype(BF16),
        bcast(stabs["new_a"]), [near_a, wp["lam"], wp["onorm4"]],
        [pl.BlockSpec(near_a.shape, lambda b, j, pt: (0, 0, 0)), pl.BlockSpec(wp["lam"].shape, lambda b, j, pt: (0, 0)),
         pl.BlockSpec(wp["onorm4"].shape, lambda b, j, pt: (0, 0))], n_near, 256, lam_init)

    lf_pages = _gather_logf(cache_lf4, l, page_table)
    lf4 = jnp.swapaxes(lf_pages.reshape(B, P, H_C), 1, 2).reshape(B, H_C, NP, PAGE)
    msu = jnp.asarray(np.tril(np.ones((LANES, LANES)), -1), BF16)
    mpu = jnp.asarray(np.triu(np.ones((NP, NP)), 1), BF16)
    c_past = _cumsum_call(lf4, msu, mpu, -1.0).reshape(B, H_C, P)
    c_past = jnp.pad(c_past, ((0, 0), (0, 8 - H_C), (0, 0)))
    logf = pj["misc"][:, 18:18 + H_C].reshape(B, S8, H_C)
    lfn = jnp.pad(jnp.swapaxes(logf, 1, 2), ((0, 0), (0, 8 - H_C), (0, LANES - S8)))[:, None]
    mw = jnp.asarray(np.triu(np.ones((LANES, LANES))), BF16)
    c_new = _cumsum_call(lfn, mw, jnp.zeros((8, 8), BF16), 1.0)[:, 0]
    cq_rows = c_new[:, :H_C, :S8].reshape(B, H_C * S8, 1)
    qc_rows = _rows_from_lanes(r3(pj["qc"]), [h * HD for h in range(H_C)], HD)
    crow = r3(pj["crow"])
    jj = np.arange(PAGE)
    tq = np.tile(np.arange(8), H_C)
    causal_new = jnp.asarray(np.where((jj[None, :] <= tq[:, None]) & (jj[None, :] < 8), 0.0, NEG), F32)
    o_c = _decode_call(
        "decode_c", cache_c, pool_maps, page_table, (768, 0), 0, 384, 384, 384, "c", qc_rows,
        _pad_rows(crow[..., 0:384], PAGE).astype(BF16), _pad_rows(crow[..., 384:768], PAGE).astype(BF16),
        bcast(causal_new), [c_past, c_new, cq_rows],
        [pl.BlockSpec((None, 8, PG * PAGE), lambda b, j, pt: (b, 0, j)),
         pl.BlockSpec((None, 8, LANES), lambda b, j, pt: (b, 0, 0)),
         pl.BlockSpec((None, H_C * S8, 1), lambda b, j, pt: (b, 0, 0))], 0, 384)

    gmask = jnp.asarray(np.arange(LANES)[None, :] // HD == np.arange(G_B)[:, None])
    qb5 = jnp.where(gmask[None, None, None], r3(pj["qb"]).reshape(B, S8, R_B, 1, LANES), jnp.zeros((), BF16))
    qb_rows = jnp.transpose(qb5, (0, 2, 3, 1, 4)).reshape(B, H_B * 8, LANES)
    kvb = r3(pj["kvb"])
    kvb_pad = _pad_rows(kvb, PAGE)
    o_bc_rows, selm1 = _scmp_call(cache_b, l, page_table, wp["cmpw_s"], kvb_pad[..., 0:256], qb_rows, P)
    rowi = np.arange(H_B * 8)
    lanei = np.arange(LANES)
    keep = jnp.asarray(((rowi[:, None] >> 3) & 1) == (lanei[None, :] >> 6))
    x = jnp.where(keep[None], o_bc_rows, 0.0).reshape(B, R_B, G_B, 8, LANES)
    o_bc = jnp.transpose(x[:, :, 0] + x[:, :, 1], (0, 2, 1, 3)).reshape(B, 8, R_B * LANES)
    selrows = jnp.broadcast_to(selm1[:, None], (B, R_B, G_B, 8, selm1.shape[-1])).reshape(B, H_B * 8, -1)
    amask = jnp.repeat(selrows[..., :2 * NP], CMP_BLOCK, axis=-1) * 1e30
    amask_new = jnp.broadcast_to(selrows[..., 2 * NP:2 * NP + 1], (B, H_B * 8, PAGE)) * 1e30
    near_b = split_steps(stabs["near_b"])
    o_bs = _decode_call(
        "decode_bsel", cache_b, pool_maps, page_table, (256, 1), 0, 128, 128, 128, "bs", qb_rows,
        kvb_pad[..., 256:384].astype(BF16), kvb_pad[..., 384:512].astype(BF16),
        stabs["new_b"][None] + amask_new, [near_b, amask.astype(BF16)],
        [pl.BlockSpec(near_b.shape, lambda b, j, pt: (0, 0, 0)),
         pl.BlockSpec((None, H_B * 8, PG * PAGE), lambda b, j, pt: (b, 0, j))], n_near, 384)
    wb = state_win.shape[2]
    npw = wb // PAGE
    win_t = jnp.transpose(state_win, (0, 1, 3, 4, 5, 2)).reshape(state_win.shape[0], B, 2 * G_B * HD, wb)
    pt_w = jnp.zeros((B, npw), jnp.int32)
    win_maps = [functools.partial(lambda b, j, pt, i: (l, b, i), i=i) for i in range(npw)]
    win = r3(pj["win"])
    win_pad = _pad_rows(win, PAGE)
    near_w = wtabs["near_w"][None]
    o_bw = _decode_call(
        "decode_bwin", win_t, win_maps, pt_w, (256, 0), 0, 128, 128, 128, "bw", qb_rows,
        win_pad[..., 0:128].astype(BF16), win_pad[..., 128:256].astype(BF16),
        bcast(wtabs["new_w"]), [near_w], [pl.BlockSpec(near_w.shape, lambda b, j, pt: (0, 0, 0))], 1, 384)

    f2 = lambda a: a.reshape(B * S8, a.shape[-1])
    y = _merge_call(x3.reshape(B * S8, D), f2(o_a), f2(o_bc), f2(o_bs), f2(o_bw), f2(o_c), pj["misc"], pj["zs"],
                    wp["w_out"], wp["eg"])
    win_all = jnp.concatenate([state_win[l].reshape(B, wb, 2 * G_B * HD), win], axis=1)[:, -wb:]
    return (y.reshape(B, S8, D), arow.reshape(B, S8, 2, H_A, HD), kvb.reshape(B, S8, 4, G_B, HD),
            win_all.reshape(B, wb, 2, G_B, HD), crow.reshape(B, S8, 2, H_C, HD), logf)


def _window_tables(stabs, wb):
    tab_w = stabs["tab_w"]
    assert tab_w.shape[1] >= wb + 8
    rel = wb + np.arange(8)[:, None] - np.arange(wb)[None, :]
    near_w = jnp.where(jnp.asarray(rel < WINDOW)[None], _shifted_rows(tab_w, wb, 8), NEG)
    new_w = _new_rows(tab_w)
    flat = lambda a: a.reshape(a.shape[0] * a.shape[1], a.shape[2])
    return dict(near_w=flat(near_w), new_w=flat(new_w))


def kernel(x_prompt, x_sample, cache_a_kv, cache_b_kv, cache_c_kv, cache_c_logf, state_b_win, page_table,
           rel_bias, norm_g, w_in, w_out, qk_a, qk_b, qk_c, onorm_a, lam_a, cmp_w, b_f):
    depth = w_in.shape[0]
    B, S, _ = x_prompt.shape
    DB, S8, _ = x_sample.shape
    assert S8 == 8 and S % 512 == 0 and state_b_win.shape[2] == WINDOW
    T = 512
    NP = page_table.shape[1]
    P = NP * PAGE
    n_pool = cache_a_kv.shape[1]
    ptabs = _prompt_tables(rel_bias, T, S)
    stabs = _sample_tables(rel_bias, P, min(8, NP))
    wtabs = _window_tables(stabs, state_b_win.shape[2])
    fmajor = lambda c: jnp.transpose(c, (0, 1, 3, 4, 5, 2)).reshape(depth, n_pool, -1, PAGE)
    caches = (fmajor(cache_a_kv), fmajor(cache_b_kv), fmajor(cache_c_kv),
              cache_c_logf.reshape(depth, n_pool, 1, PAGE * H_C), state_b_win)
    page_table = page_table.astype(jnp.int32)
    yp, ys = x_prompt, x_sample
    outs = [[] for _ in range(10)]
    for l in range(depth):
        lam_init = 0.8 - 0.6 * math.exp(-0.3 * l)
        wp = _prep_layer(l, norm_g, w_in, w_out, qk_a, qk_b, qk_c, onorm_a, lam_a, cmp_w, b_f)
        yp, a_r, b_r, w_r, c_r, l_r = _prompt_layer(yp, wp, ptabs, lam_init, T)
        for k, v in zip((0, 2, 4, 6, 8), (a_r, b_r, w_r, c_r, l_r)):
            outs[k].append(v)
        ys, a_r, b_r, w_r, c_r, l_r = _sample_layer(l, ys, caches, page_table, wp, stabs, wtabs, lam_init)
        for k, v in zip((1, 3, 5, 7, 9), (a_r, b_r, w_r, c_r, l_r)):
            outs[k].append(v)
    return (yp, ys) + tuple(jnp.stack(o) for o in outs)
```

```python
import functools
import math

import numpy as np
import jax
import jax.numpy as jnp
from jax import lax
from jax.experimental import pallas as pl
from jax.experimental.pallas import tpu as pltpu

F32 = jnp.float32
BF16 = jnp.bfloat16

HD = 64
H_A = 4
DA = HD // 2
H_B = 6
G_B = 2
R_B = H_B // G_B
H_C = 6
CMP_BLOCK = 64
TOPK = 16
WINDOW = 512
FORCE_BONUS = float(R_B + 1)
N_BUCKETS = 32
REL_MAX_DIST = 1024
EPS = 1e-6
PAGE = 128

LOG2E = 1.4426950408889634
NEG = -1e30
M_INIT = -0.5e30
LANES = 128
VMEM_LIMIT = 56 * 1024 * 1024
DECODE_PAGES = 16

O_QB, O_KVB, O_WIN, O_GB, O_ZB, O_QC, O_FC, O_ZC = 1024, 1408, 1920, 2176, 2194, 2578, 3730, 3736
PROJ_W = 4120
P_QB, P_KVB, P_WIN, P_ZB, P_QC, P_ZC, P_MISC, P_END = 1024, 1408, 1920, 2176, 2560, 3712, 4096, 4224
SLAB_HEADS = [g * R_B + r for r in range(R_B) for g in range(G_B)]


def _cparams(n_grid):
    return pltpu.CompilerParams(dimension_semantics=("arbitrary",) * n_grid, vmem_limit_bytes=VMEM_LIMIT)


def _dot(a, b):
    return jnp.dot(a, b, preferred_element_type=F32)


def _dot_nt(a, b):
    return lax.dot_general(a, b, (((1,), (1,)), ((), ())), preferred_element_type=F32)


def _split3(x):
    x1 = x.astype(BF16)
    r = x - x1.astype(F32)
    x2 = r.astype(BF16)
    x3 = (r - x2.astype(F32)).astype(BF16)
    return x1, x2, x3


def _dot3(x, m):
    x1, x2, x3 = _split3(x)
    return _dot(x1, m) + _dot(x2, m) + _dot(x3, m)


def _dot3_l(m, x):
    x1, x2, x3 = _split3(x)
    return _dot(m, x1) + _dot(m, x2) + _dot(m, x3)


def _rel_bucket(rel):
    n = jnp.maximum(rel, 0)
    max_exact = N_BUCKETS // 2
    nf = jnp.maximum(n, 1).astype(F32)
    large = max_exact + (jnp.log(nf / max_exact) / math.log(REL_MAX_DIST / max_exact)
                         * (N_BUCKETS - max_exact)).astype(jnp.int32)
    return jnp.where(n < max_exact, n, jnp.minimum(large, N_BUCKETS - 1))


def _inproj_kernel(x_ref, w_ref, g_in, ga_q, ga_k, gb_q, gb_k, gc_q, gc_k, bf_row, cmpw_ref, bd32_ref, bd64_ref,
                   qa_o, arow_o, ka_o, va_o, qb_o, kvb_o, ksel_o, vsel_o, kcmp_o, vcmp_o, win_o, kwin_o, vwin_o,
                   qc_o, crow_o, kc_o, vc_o, zs_o, misc_o, *, tm, do_cmp):
    x = x_ref[...]
    ms = jnp.mean(x * x, axis=-1, keepdims=True)
    h = (x * lax.rsqrt(ms + EPS) * g_in[...]).astype(BF16)

    def proj(c0, c1):
        return _dot(h, w_ref[:, c0:c1])

    def segnorm(y, bd, seg):
        ss = _dot((y * y).astype(BF16), bd)
        return y * lax.rsqrt(ss * (1.0 / seg) + EPS)

    def silu(y):
        return y * (1.0 / (1.0 + jnp.exp(-y)))

    bd32 = bd32_ref[...]
    bd64 = bd64_ref[...]
    bd64_1 = bd64[0:128, 0:128]

    y = proj(0, 256)
    qa_o[...] = (segnorm(y, bd32, DA) * ga_q[...]).astype(BF16)
    y = segnorm(proj(256, 512), bd32, DA) * ga_k[...]
    arow_o[:, 0:256] = y
    ka_o[...] = y.astype(BF16)
    y = proj(512, 768)
    arow_o[:, 256:512] = y
    va_o[...] = y.astype(BF16)
    zs_o[:, 0:256] = silu(proj(768, 1024))

    qb_o[...] = (segnorm(proj(P_QB, P_KVB), bd64, HD) * gb_q[...]).astype(BF16)
    kcn = segnorm(proj(P_KVB, P_KVB + 128), bd64_1, HD) * gb_k[...]
    kvb_o[:, 0:128] = kcn
    vcr = proj(P_KVB + 128, P_KVB + 256)
    kvb_o[:, 128:256] = vcr
    y = segnorm(proj(P_KVB + 256, P_KVB + 384), bd64_1, HD) * gb_k[...]
    kvb_o[:, 256:384] = y
    ksel_o[...] = y.astype(BF16)
    y = proj(P_KVB + 384, P_KVB + 512)
    kvb_o[:, 384:512] = y
    vsel_o[...] = y.astype(BF16)
    if do_cmp:
        nbt = tm // CMP_BLOCK
        rowi = lax.broadcasted_iota(jnp.int32, (nbt, tm), 0)
        coli = lax.broadcasted_iota(jnp.int32, (nbt, tm), 1)
        blk = (coli >> 6) == rowi

        def cmp_weights(wl):
            e = jnp.where(blk, jnp.exp(wl - jnp.max(wl, axis=-1, keepdims=True)), 0.0)
            return (e / jnp.sum(e, axis=-1, keepdims=True)).astype(BF16)

        kcmp_o[...] = _dot(cmp_weights(cmpw_ref[0]), kcn.astype(BF16))
        vcmp_o[...] = _dot(cmp_weights(cmpw_ref[1]), vcr.astype(BF16))
    else:
        kcmp_o[...] = jnp.zeros(kcmp_o.shape, F32)
        vcmp_o[...] = jnp.zeros(vcmp_o.shape, F32)
    y = segnorm(proj(P_WIN, P_WIN + 128), bd64_1, HD) * gb_k[...]
    win_o[:, 0:128] = y
    kwin_o[...] = y.astype(BF16)
    y = proj(P_WIN + 128, P_WIN + 256)
    win_o[:, 128:256] = y
    vwin_o[...] = y.astype(BF16)
    zs_o[:, 256:640] = silu(proj(P_ZB, P_QC))

    qc_o[...] = (segnorm(proj(P_QC, P_QC + 384), bd64, HD) * gc_q[...]).astype(BF16)
    y = segnorm(proj(P_QC + 384, P_QC + 768), bd64, HD) * gc_k[...]
    crow_o[:, 0:384] = y
    kc_o[...] = y.astype(BF16)
    y = proj(P_QC + 768, P_ZC)
    crow_o[:, 384:768] = y
    vc_o[...] = y.astype(BF16)
    zs_o[:, 640:1024] = silu(proj(P_ZC, P_MISC))

    y = proj(P_MISC, P_END) + bf_row[...]
    lane = lax.broadcasted_iota(jnp.int32, y.shape, 1)
    sig = 1.0 / (1.0 + jnp.exp(-y))
    lsg = jnp.minimum(y, 0.0) - jnp.log(1.0 + jnp.exp(-jnp.abs(y)))
    misc_o[...] = jnp.where(lane < 18, sig, lsg)


def _inproj(x2, wp, *, do_cmp):
    T = x2.shape[0]
    tm = min(512, T)
    assert T % tm == 0 and (tm == 512 or not do_cmp)
    nbt = 8
    n = T // tm
    full = lambda a: pl.BlockSpec(a.shape, lambda i: (0,) * a.ndim)
    row = lambda w: pl.BlockSpec((tm, w), lambda i: (i, 0))
    consts = [wp["w_in"], wp["g_in"], wp["ga_q"], wp["ga_k"], wp["gb_q"], wp["gb_k"], wp["gc_q"], wp["gc_k"],
              wp["bf_row"], wp["cmpw_in"], wp["bd32"], wp["bd64"]]
    outs = [("qa", 256, BF16), ("arow", 512, F32), ("ka", 256, BF16), ("va", 256, BF16), ("qb", 384, BF16),
            ("kvb", 512, F32), ("ksel", 128, BF16), ("vsel", 128, BF16), ("kcmp", None, F32), ("vcmp", None, F32),
            ("win", 256, F32), ("kwin", 128, BF16), ("vwin", 128, BF16), ("qc", 384, BF16), ("crow", 768, F32),
            ("kc", 384, BF16), ("vc", 384, BF16), ("zs", 1024, F32), ("misc", 128, F32)]
    out_shape, out_specs = [], []
    for _, w, dt in outs:
        if w is None:
            out_shape.append(jax.ShapeDtypeStruct((n * nbt, 128), dt))
            out_specs.append(pl.BlockSpec((nbt, 128), lambda i: (i, 0)))
        else:
            out_shape.append(jax.ShapeDtypeStruct((T, w), dt))
            out_specs.append(row(w))
    res = pl.pallas_call(
        functools.partial(_inproj_kernel, tm=tm, do_cmp=do_cmp),
        grid=(n,),
        in_specs=[row(1024)] + [full(c) for c in consts],
        out_specs=out_specs, out_shape=out_shape,
        compiler_params=_cparams(1), name="inproj",
    )(x2, *consts)
    return {k: v for (k, _, _), v in zip(outs, res)}


def _attend(lhs_list, k_ref, v_ref, T, far_lo, near_lo, hi, near_add=None):
    n = len(lhs_list)

    def step(kt, state, near):
        ks = pl.multiple_of(kt * T, T)
        kk = k_ref[pl.ds(ks, T), :]
        vv = v_ref[pl.ds(ks, T), :]
        out = []
        for i in range(n):
            m, l, acc = state[i]
            s = _dot_nt(lhs_list[i], kk)
            if near:
                s = near_add[i](s, kt)
            m_new = jnp.maximum(m, jnp.max(s, axis=-1, keepdims=True))
            p = jnp.exp2(s - m_new)
            alpha = jnp.exp2(m - m_new)
            l = alpha * l + jnp.sum(p, axis=-1, keepdims=True)
            acc = alpha * acc + _dot(p.astype(BF16), vv)
            out.append((m_new, l, acc))
        return tuple(out)

    one = (jnp.full((T, 1), M_INIT, F32), jnp.zeros((T, 1), F32), jnp.zeros((T, v_ref.shape[-1]), F32))
    state = (one,) * n
    npair = (near_lo - far_lo) // 2
    state = lax.fori_loop(
        0, npair, lambda j, c: step(far_lo + 2 * j + 1, step(far_lo + 2 * j, c, False), False), state)
    state = lax.fori_loop(far_lo + 2 * npair, near_lo, lambda kt, c: step(kt, c, False), state)
    state = lax.fori_loop(near_lo, hi, lambda kt, c: step(kt, c, True), state)
    return [acc / l for _, l, acc in state]


def _lane_band(x, lo, width):
    lane = lax.broadcasted_iota(jnp.int32, x.shape, 1)
    return jnp.where((lane >= lo) & (lane < lo + width), x, jnp.zeros_like(x))


def _halves(a, b):
    lane = lax.broadcasted_iota(jnp.int32, a.shape, 1)
    return jnp.where(lane < HD, a, b)


def _diff_lambda(lam_ref, lam_init):
    lf = lam_ref[...]
    a = jnp.sum(lf[0:1] * lf[1:2], axis=-1, keepdims=True)
    b = jnp.sum(lf[2:3] * lf[3:4], axis=-1, keepdims=True)
    return jnp.exp(a) - jnp.exp(b) + lam_init


def _flash_a_kernel(q_ref, k_ref, v_ref, bias_ref, lam_ref, onorm_ref, o_ref, *, T, noff, lam_init):
    qi = pl.program_id(2)
    near_lo = jnp.maximum(qi - (noff - 1), 0)
    q = q_ref[...]
    lam = _diff_lambda(lam_ref, lam_init)
    lhs = [_lane_band(q, hl * HD + mp * DA, DA) for hl in range(2) for mp in range(2)]
    add = [functools.partial(lambda s, kt, hl: s + bias_ref[hl, qi - kt], hl=i // 2) for i in range(4)]
    o = _attend(lhs, k_ref, v_ref, T, 0, near_lo, qi + 1, near_add=add)
    o = _halves(o[0] - lam * o[1], o[2] - lam * o[3])
    x2 = o * o
    lane = lax.broadcasted_iota(jnp.int32, o.shape, 1)
    s0 = jnp.sum(jnp.where(lane < HD, x2, 0.0), axis=-1, keepdims=True)
    s1 = jnp.sum(jnp.where(lane < HD, 0.0, x2), axis=-1, keepdims=True)
    ss = jnp.where(lane < HD, s0, s1)
    o_ref[...] = o * lax.rsqrt(ss * (1.0 / HD) + EPS) * onorm_ref[...] * (1.0 - lam_init)


def _flash_c_kernel(q_ref, qx_ref, k_ref, v_ref, mask_ref, o_ref, *, T):
    qi = pl.program_id(2)
    q = q_ref[...]
    qx = qx_ref[...]
    lane = lax.broadcasted_iota(jnp.int32, qx.shape, 1)
    lhs = []
    for hl in range(2):
        own = ((lane >= 3 * hl) & (lane < 3 * hl + 3)) | ((lane >= 6 + 3 * hl) & (lane < 9 + 3 * hl))
        lhs.append(jnp.concatenate([_lane_band(q, hl * HD, HD), jnp.where(own, qx, jnp.zeros_like(qx))], axis=1))
    causal = [lambda s, kt: s + mask_ref[...]] * 2
    o = _attend(lhs, k_ref, v_ref, T, 0, qi, qi + 1, near_add=causal)
    o_ref[...] = _halves(o[0], o[1])


def _flash_b_kernel(*refs, T, noff, selected):
    if selected:
        q_ref, sel_ref, k_ref, v_ref, bias_ref, o_ref = refs
    else:
        q_ref, k_ref, v_ref, bias_ref, o_ref = refs
    qi = pl.program_id(2)
    near_lo = jnp.maximum(qi - (noff - 1), 0)
    far_lo = 0 if selected else near_lo
    q = q_ref[...]
    lhs = [_lane_band(q, g * HD, HD) for g in range(G_B)]
    if selected:
        nbp = sel_ref.shape[-1] // G_B
        lhs = [jnp.concatenate([lhs[g], sel_ref[:, g * nbp:(g + 1) * nbp]], axis=1) for g in range(G_B)]
    add = [functools.partial(lambda s, kt, g: s + bias_ref[g, qi - kt], g=g) for g in range(G_B)]
    o = _attend(lhs, k_ref, v_ref, T, far_lo, near_lo, qi + 1, near_add=add)
    o_ref[...] = _halves(o[0], o[1])


def _flash_call(kernel, name, B, S, T, n_grp, ins, in_specs, out_w):
    return pl.pallas_call(
        kernel, grid=(B, n_grp, S // T), in_specs=in_specs,
        out_specs=pl.BlockSpec((None, T, LANES), lambda b, g, i: (b, i, g)),
        out_shape=jax.ShapeDtypeStruct((B, S, out_w), F32),
        compiler_params=_cparams(3), name=name,
    )(*ins)


def _qspec(T):
    return pl.BlockSpec((None, T, LANES), lambda b, g, i: (b, i, g))


def _kvspec(S, w, per_group):
    if per_group:
        return pl.BlockSpec((None, S, w), lambda b, g, i: (b, 0, g))
    return pl.BlockSpec((None, S, w), lambda b, g, i: (b, 0, 0))


def _biasspec(noff, T):
    return pl.BlockSpec((2, noff, T, T), lambda b, g, i: (g, 0, 0, 0))


def _small(a):
    return pl.BlockSpec(a.shape, lambda b, g, i: (0,) * a.ndim)


def _masked_softmax2(s, mask, axis):
    s = jnp.where(mask, s, NEG)
    m = jnp.max(s, axis=axis, keepdims=True)
    p = jnp.where(mask, jnp.exp2(s - m), 0.0)
    return p / jnp.maximum(jnp.sum(p, axis=axis, keepdims=True), 1e-30)


def _topk_select(score, idx, n_idx, axis):
    def it(_, c):
        sc, sel = c
        mx = jnp.max(sc, axis=axis, keepdims=True)
        first = jnp.min(jnp.where(sc == mx, idx, n_idx), axis=axis, keepdims=True)
        oh = idx == first
        sel = jnp.where(oh, jnp.where(mx >= 0.0, 1.0, sel), sel)
        sc = jnp.where(oh, -2.0, sc)
        return sc, sel

    _, sel = lax.fori_loop(0, TOPK, it, (score, jnp.zeros(score.shape, F32)))
    return sel


def _cmp_kernel(q_ref, kc_ref, vc_ref, oc_ref, selt_ref, *, T, NBP):
    qi = pl.program_id(1)
    q = q_ref[...]
    kc = kc_ref[...]
    vc = vc_ref[...]
    qpos_r = qi * T + lax.broadcasted_iota(jnp.int32, (T, NBP), 0)
    blk_c = lax.broadcasted_iota(jnp.int32, (T, NBP), 1)
    cm = ((blk_c + 1) * CMP_BLOCK - 1) <= qpos_r
    qpos_c = qi * T + lax.broadcasted_iota(jnp.int32, (NBP, T), 1)
    blk_r = lax.broadcasted_iota(jnp.int32, (NBP, T), 0)
    cmt = ((blk_r + 1) * CMP_BLOCK - 1) <= qpos_c
    imp = [jnp.zeros((NBP, T), F32) for _ in range(G_B)]
    for r in range(R_B):
        slab = q[:, r * LANES:(r + 1) * LANES]
        outs = []
        for g in range(G_B):
            qm = _lane_band(slab, g * HD, HD)
            p = _masked_softmax2(_dot_nt(qm, kc), cm, -1)
            outs.append(_dot(p.astype(BF16), vc))
            imp[g] = imp[g] + _masked_softmax2(_dot_nt(kc, qm), cmt, 0)
        oc_ref[:, r * LANES:(r + 1) * LANES] = _halves(outs[0], outs[1])
    qblk = qpos_c >> 6
    forced = (blk_r == qblk) | (blk_r == qblk - 1) | (blk_r == 0)
    valid = blk_r * CMP_BLOCK <= qpos_c
    for g in range(G_B):
        score = jnp.where(valid, imp[g] + jnp.where(forced, FORCE_BONUS, 0.0), -1.0)
        selt_ref[g] = _topk_select(score, blk_r, NBP, 0) - 1.0


def _cmp_call(qb3, kcmp3, vcmp3, T):
    B, S, _ = qb3.shape
    NBP = kcmp3.shape[1]
    return pl.pallas_call(
        functools.partial(_cmp_kernel, T=T, NBP=NBP), grid=(B, S // T),
        in_specs=[pl.BlockSpec((None, T, 384), lambda b, i: (b, i, 0)),
                  pl.BlockSpec((None, NBP, LANES), lambda b, i: (b, 0, 0)),
                  pl.BlockSpec((None, NBP, LANES), lambda b, i: (b, 0, 0))],
        out_specs=[pl.BlockSpec((None, T, 384), lambda b, i: (b, i, 0)),
                   pl.BlockSpec((None, G_B, NBP, T), lambda b, i: (b, 0, 0, i))],
        out_shape=[jax.ShapeDtypeStruct((B, S, 384), F32), jax.ShapeDtypeStruct((B, G_B, NBP, S), F32)],
        compiler_params=_cparams(2), name="nsa_cmp_topk",
    )(qb3, kcmp3, vcmp3)


def _cumsum_kernel(x_ref, mw_ref, mr_ref, *o_refs, nh, sign, split):
    mw = mw_ref[...]
    mr = mr_ref[...]
    ones = jnp.ones((LANES, LANES), BF16)
    for h in range(nh):
        x = x_ref[h]
        tot = _dot3(x, ones)
        c = (_dot3(x, mw) + _dot3_l(mr, tot)) * (sign * LOG2E)
        if split:
            for o_ref, part in zip(o_refs, _split3(c)):
                o_ref[h] = part
        else:
            o_refs[0][h] = c


def _cumsum_call(x4, mw, mr, sign, split=False):
    B, nh, nr, _ = x4.shape
    spec = pl.BlockSpec((None, nh, nr, LANES), lambda b: (b, 0, 0, 0))
    return pl.pallas_call(
        functools.partial(_cumsum_kernel, nh=nh, sign=sign, split=split), grid=(B,),
        in_specs=[spec, pl.BlockSpec(mw.shape, lambda b: (0, 0)), pl.BlockSpec(mr.shape, lambda b: (0, 0))],
        out_specs=[spec] * 3 if split else spec,
        out_shape=[jax.ShapeDtypeStruct(x4.shape, BF16)] * 3 if split else jax.ShapeDtypeStruct(x4.shape, F32),
        compiler_params=_cparams(1), name="logf_cumsum",
    )(x4, mw, mr)


def _merge_kernel(x_ref, oa_ref, obc_ref, obs_ref, obw_ref, oc_ref, misc_ref, zs_ref, w_ref, eg_ref, y_ref):
    g = misc_ref[...]
    g1 = g.astype(BF16)
    g2 = (g - g1.astype(F32)).astype(BF16)

    def gate(c):
        return _dot(g1, eg_ref[c]) + _dot(g2, eg_ref[c])

    ob = gate(0) * obc_ref[...] + gate(1) * obs_ref[...] + gate(2) * obw_ref[...]
    ma = (oa_ref[...] * zs_ref[:, 0:256]).astype(BF16)
    mb = (ob * zs_ref[:, 256:640]).astype(BF16)
    mc = (oc_ref[...] * zs_ref[:, 640:1024]).astype(BF16)
    y_ref[...] = x_ref[...] + _dot(ma, w_ref[0:256, :]) + _dot(mb, w_ref[256:640, :]) + _dot(mc, w_ref[640:1024, :])


def _merge_call(x2, oa, obc, obs, obw, oc, misc, zs, w_out, eg):
    T = x2.shape[0]
    tm = min(512, T)
    row = lambda w: pl.BlockSpec((tm, w), lambda i: (i, 0))
    return pl.pallas_call(
        _merge_kernel, grid=(T // tm,),
        in_specs=[row(1024), row(256), row(384), row(384), row(384), row(384), row(128), row(1024),
                  pl.BlockSpec(w_out.shape, lambda i: (0, 0)), pl.BlockSpec(eg.shape, lambda i: (0, 0, 0))],
        out_specs=row(1024), out_shape=jax.ShapeDtypeStruct((T, 1024), F32),
        compiler_params=_cparams(1), name="merge_outproj",
    )(x2, oa, obc, obs, obw, oc, misc, zs, w_out, eg)


def _gather_rows_kernel(pt_ref, *refs, n):
    out = refs[n]
    for i in range(n):
        out[i] = refs[i][...]


def _gather_logf(cache4, layer, page_table):
    B, NP = page_table.shape
    n = min(16, NP)
    assert NP % n == 0
    w = cache4.shape[-1]
    in_specs = [pl.BlockSpec((None, None, 1, w), functools.partial(
        lambda b, j, pt, i: (layer, pt[b, j * n + i], 0, 0), i=i)) for i in range(n)]
    return pl.pallas_call(
        functools.partial(_gather_rows_kernel, n=n),
        grid_spec=pltpu.PrefetchScalarGridSpec(
            num_scalar_prefetch=1, grid=(B, NP // n), in_specs=in_specs,
            out_specs=pl.BlockSpec((None, n, 1, w), lambda b, j, pt: (b, j, 0, 0))),
        out_shape=jax.ShapeDtypeStruct((B, NP, 1, w), F32),
        compiler_params=_cparams(2), name="gather_logf",
    )(page_table, *([cache4] * n))


def _scmp_kernel(pt_ref, *refs, PG, NP, NBP, P):
    pages = refs[:PG]
    cmpw_ref, new_ref, qb_ref, oc_ref, sel_ref, kcs, vcs = refs[PG:]
    jc = pl.program_id(1)
    ns = pl.num_programs(1)
    nrow = 2 * PG

    @pl.when(jc == 0)
    def _():
        kcs[...] = jnp.zeros(kcs.shape, F32)
        vcs[...] = jnp.zeros(vcs.shape, F32)

    rowi = lax.broadcasted_iota(jnp.int32, (nrow, PAGE), 0)
    coli = lax.broadcasted_iota(jnp.int32, (nrow, PAGE), 1)

    def half_softmax(wl):
        e = jnp.exp(wl - jnp.max(wl, axis=-1, keepdims=True))
        return e / (0.5 * jnp.sum(e, axis=-1, keepdims=True))

    wk = half_softmax(cmpw_ref[0])
    wv = half_softmax(cmpw_ref[1])
    kacc = jnp.zeros((nrow, LANES), F32)
    vacc = jnp.zeros((nrow, LANES), F32)
    for i in range(PG):
        sel = (coli >> 6) + 2 * i == rowi
        pg = pages[i][...]
        kacc = kacc + _dot_nt(jnp.where(sel, wk, 0.0).astype(BF16), pg[0:128, :].astype(BF16))
        vacc = vacc + _dot_nt(jnp.where(sel, wv, 0.0).astype(BF16), pg[128:256, :].astype(BF16))
    r0 = pl.multiple_of(jc * nrow, nrow)
    kcs[pl.ds(r0, nrow), :] = kacc
    vcs[pl.ds(r0, nrow), :] = vacc

    @pl.when(jc == ns - 1)
    def _():
        first = (rowi == 0) & (coli < CMP_BLOCK)
        new = new_ref[...]
        kcs[2 * NP:2 * NP + nrow, :] = _dot(jnp.where(first, wk, 0.0).astype(BF16), new[:, 0:128].astype(BF16))
        vcs[2 * NP:2 * NP + nrow, :] = _dot(jnp.where(first, wv, 0.0).astype(BF16), new[:, 128:256].astype(BF16))
        kc = kcs[...].astype(BF16)
        vc = vcs[...].astype(BF16)
        R = qb_ref.shape[0]
        s = _dot_nt(qb_ref[...], kc)
        t_r = lax.broadcasted_iota(jnp.int32, (R, NBP), 0) & 7
        blk = lax.broadcasted_iota(jnp.int32, (R, NBP), 1)
        cm = ((blk + 1) * CMP_BLOCK - 1) <= (P + t_r)
        p = _masked_softmax2(s, cm, -1)
        oc_ref[...] = _dot(p.astype(BF16), vc)
        qpos = P + lax.broadcasted_iota(jnp.int32, (8, NBP), 0)
        blk8 = lax.broadcasted_iota(jnp.int32, (8, NBP), 1)
        qblk = qpos >> 6
        forced = (blk8 == qblk) | (blk8 == qblk - 1) | (blk8 == 0)
        valid = blk8 * CMP_BLOCK <= qpos
        for g in range(G_B):
            imp = sum(p[(r * G_B + g) * 8:(r * G_B + g) * 8 + 8] for r in range(R_B))
            score = jnp.where(valid, imp + jnp.where(forced, FORCE_BONUS, 0.0), -1.0)
            sel_ref[g] = _topk_select(score, blk8, NBP, -1) - 1.0


def _scmp_call(cache_b, layer, page_table, cmpw_s, new_pad, qb_rows, P):
    B, NP = page_table.shape
    PG = min(DECODE_PAGES, NP)
    assert NP % PG == 0
    NB = 2 * NP + 1
    NBP = -(-(2 * NP + 2 * PG) // LANES) * LANES
    in_specs = [pl.BlockSpec((None, None, 256, PAGE), functools.partial(
        lambda b, j, pt, i: (layer, pt[b, j * PG + i], 0, 0), i=i)) for i in range(PG)]
    in_specs += [pl.BlockSpec(cmpw_s.shape, lambda b, j, pt: (0, 0, 0)),
                 pl.BlockSpec((None, PAGE, 256), lambda b, j, pt: (b, 0, 0)),
                 pl.BlockSpec((None,) + qb_rows.shape[1:], lambda b, j, pt: (b, 0, 0))]
    R = qb_rows.shape[1]
    del NB
    return pl.pallas_call(
        functools.partial(_scmp_kernel, PG=PG, NP=NP, NBP=NBP, P=P),
        grid_spec=pltpu.PrefetchScalarGridSpec(
            num_scalar_prefetch=1, grid=(B, NP // PG), in_specs=in_specs,
            out_specs=[pl.BlockSpec((None, R, LANES), lambda b, j, pt: (b, 0, 0)),
                       pl.BlockSpec((None, G_B, 8, NBP), lambda b, j, pt: (b, 0, 0, 0))],
            scratch_shapes=[pltpu.VMEM((NBP, LANES), F32), pltpu.VMEM((NBP, LANES), F32)]),
        out_shape=[jax.ShapeDtypeStruct((B, R, LANES), F32), jax.ShapeDtypeStruct((B, G_B, 8, NBP), F32)],
        compiler_params=_cparams(2), name="sample_cmp_topk",
    )(page_table, *([cache_b] * PG), cmpw_s, new_pad, qb_rows)


def _decode_kernel(pt_ref, *refs, PG, kc0, KD, vc0, VD, mode, n_near, lam_init):
    pages = refs[:PG]
    rest = list(refs[PG:])
    qb_ref, knew_ref, vnew_ref, addnew_ref = rest[:4]
    rest = rest[4:]
    bias_ref = ck_ref = cnew_ref = cq_ref = amask_ref = lam_ref = onorm_ref = None
    if mode == "a":
        bias_ref, lam_ref, onorm_ref = rest[:3]
        rest = rest[3:]
    elif mode == "c":
        ck_ref, cnew_ref, cq_ref = rest[:3]
        rest = rest[3:]
    elif mode == "bs":
        bias_ref, amask_ref = rest[:2]
        rest = rest[2:]
    else:
        bias_ref = rest[0]
        rest = rest[1:]
    o_ref, m_s, l_s, acc_s = rest
    jc = pl.program_id(1)
    ns = pl.num_programs(1)
    R = qb_ref.shape[0]

    @pl.when(jc == 0)
    def _():
        m_s[...] = jnp.full(m_s.shape, M_INIT, F32)
        l_s[...] = jnp.zeros(l_s.shape, F32)
        acc_s[...] = jnp.zeros(acc_s.shape, F32)

    def rows_of_heads(c):
        return jnp.concatenate([jnp.broadcast_to(c[h:h + 1, :], (8, c.shape[1])) for h in range(R // 8)], axis=0)

    def update(s, v, v_feature_major):
        m = m_s[...]
        m_new = jnp.maximum(m, jnp.max(s, axis=-1, keepdims=True))
        p = jnp.exp2(s - m_new)
        alpha = jnp.exp2(m - m_new)
        l_s[...] = alpha * l_s[...] + jnp.sum(p, axis=-1, keepdims=True)
        pb = p.astype(BF16)
        acc_s[...] = alpha * acc_s[...] + (_dot_nt(pb, v) if v_feature_major else _dot(pb, v))
        m_s[...] = m_new

    q = qb_ref[...]
    kcat = jnp.concatenate([pg[kc0:kc0 + KD, :].astype(BF16) for pg in pages], axis=1)
    vcat = jnp.concatenate([pg[vc0:vc0 + VD, :].astype(BF16) for pg in pages], axis=1)
    s = _dot(q, kcat)
    if mode == "c":
        s = s + cq_ref[...] - rows_of_heads(ck_ref[...])
    if mode == "bs":
        s = s + amask_ref[...].astype(F32)
    if bias_ref is not None:
        e = jc - (ns - n_near)
        s = s + jnp.where(e >= 0, 1.0, 0.0) * bias_ref[jnp.maximum(e, 0)]
    update(s, vcat, True)

    @pl.when(jc == ns - 1)
    def _():
        s2 = _dot_nt(q, knew_ref[...]) + addnew_ref[...]
        if mode == "c":
            s2 = s2 + cq_ref[...] - rows_of_heads(cnew_ref[...])
        update(s2, vnew_ref[...], False)
        o = acc_s[...] / l_s[...]
        if mode == "a":
            lam = _diff_lambda(lam_ref, lam_init)
            half = R // 2
            pd = o[0:half] - lam * o[half:R]
            rowi = lax.broadcasted_iota(jnp.int32, pd.shape, 0)
            lanei = lax.broadcasted_iota(jnp.int32, pd.shape, 1)
            x = jnp.where((rowi >> 3) == (lanei >> 6), pd, 0.0)
            ss = jnp.sum(x * x, axis=-1, keepdims=True)
            y = x * lax.rsqrt(ss * (1.0 / HD) + EPS) * onorm_ref[...] * (1.0 - lam_init)
            o_ref[...] = sum(y[h * 8:(h + 1) * 8] for h in range(H_A))
        elif mode == "c":
            rowi = lax.broadcasted_iota(jnp.int32, o.shape, 0)
            lanei = lax.broadcasted_iota(jnp.int32, o.shape, 1)
            x = jnp.where((rowi >> 3) == (lanei >> 6), o, 0.0)
            o_ref[...] = sum(x[h * 8:(h + 1) * 8] for h in range(H_C))
        else:
            rowi = lax.broadcasted_iota(jnp.int32, o.shape, 0)
            lanei = lax.broadcasted_iota(jnp.int32, o.shape, 1)
            x = jnp.where(((rowi >> 3) & 1) == (lanei >> 6), o, 0.0)
            o_ref[...] = jnp.concatenate(
                [x[(2 * r) * 8:(2 * r) * 8 + 8] + x[(2 * r + 1) * 8:(2 * r + 1) * 8 + 8] for r in range(R_B)], axis=1)


def _decode_call(name, cache, page_maps, page_table, rowblk, kc0, KD, vc0, VD, mode, qb_rows, knew, vnew, addnew,
                 extras, extra_specs, n_near, out_w, lam_init=0.0):
    B, NP = page_table.shape
    PG = len(page_maps)
    assert NP % PG == 0
    R = qb_rows.shape[1]
    bh, bi = rowblk

    def page_spec(f):
        def index(b, j, pt):
            d0, d1, lane_blk = f(b, j, pt)
            return (d0, d1, bi, lane_blk)
        return pl.BlockSpec((None, None, bh, PAGE), index)

    in_specs = [page_spec(f) for f in page_maps]
    bspec = lambda a: pl.BlockSpec((None,) + a.shape[1:], lambda b, j, pt: (b,) + (0,) * (a.ndim - 1))
    in_specs += [bspec(qb_rows), bspec(knew), bspec(vnew), bspec(addnew)] + extra_specs
    return pl.pallas_call(
        functools.partial(_decode_kernel, PG=PG, kc0=kc0, KD=KD, vc0=vc0, VD=VD, mode=mode, n_near=n_near,
                          lam_init=lam_init),
        grid_spec=pltpu.PrefetchScalarGridSpec(
            num_scalar_prefetch=1, grid=(B, NP // PG), in_specs=in_specs,
            out_specs=pl.BlockSpec((None, 8, out_w), lambda b, j, pt: (b, 0, 0)),
            scratch_shapes=[pltpu.VMEM((R, 1), F32), pltpu.VMEM((R, 1), F32), pltpu.VMEM((R, VD), F32)]),
        out_shape=jax.ShapeDtypeStruct((B, 8, out_w), F32),
        compiler_params=_cparams(2), name=name,
    )(page_table, *([cache] * PG), qb_rows, knew, vnew, addnew, *extras)


def _prep_layer(l, norm_g, w_in, w_out, qk_a, qk_b, qk_c, onorm_a, lam_a, cmp_w, b_f):
    wl = w_in[l]
    slab_cols = lambda base: [wl[:, base + h * HD:base + (h + 1) * HD] for h in SLAB_HEADS]
    w = jnp.concatenate(
        [wl[:, 0:O_QB]] + slab_cols(O_QB) + [wl[:, O_KVB:O_GB]] + slab_cols(O_ZB)
        + [wl[:, O_QC:O_FC], wl[:, O_ZC:PROJ_W], wl[:, O_GB:O_GB + 18], wl[:, O_FC:O_FC + H_C],
           jnp.zeros((wl.shape[0], P_END - PROJ_W), wl.dtype)], axis=1).astype(BF16)
    wol = w_out[l]
    wo = jnp.concatenate([wol[0:256]] + [wol[256 + h * HD:256 + (h + 1) * HD] for h in SLAB_HEADS]
                         + [wol[640:1024]], axis=0).astype(BF16)
    tile = lambda v, n: jnp.tile(v.astype(F32), n)[None, :]
    bf_row = jnp.zeros((1, LANES), F32).at[0, 18:18 + H_C].set(b_f[l].astype(F32))
    bd = lambda n, seg: jnp.asarray(np.kron(np.eye(n // seg), np.ones((seg, seg))), BF16)
    eg = np.zeros((3, LANES, 384), np.float32)
    for g in range(G_B):
        for r in range(R_B):
            for c in range(3):
                eg[c, g * 9 + r * 3 + c, (r * G_B + g) * HD:(r * G_B + g + 1) * HD] = 1.0
    return dict(
        w_in=w, w_out=wo, g_in=norm_g[l].astype(F32)[None, :],
        ga_q=tile(qk_a[l, 0], 8) * (DA ** -0.5 * LOG2E), ga_k=tile(qk_a[l, 1], 8),
        gb_q=tile(qk_b[l, 0], 6) * (HD ** -0.5 * LOG2E), gb_k=tile(qk_b[l, 1], 2),
        gc_q=tile(qk_c[l, 0], 6) * (HD ** -0.5 * LOG2E), gc_k=tile(qk_c[l, 1], 6),
        bf_row=bf_row, bd32=bd(256, DA), bd64=bd(384, HD),
        cmpw_in=jnp.tile(cmp_w[l].astype(F32)[:, None, :], (1, 8, 8)),
        cmpw_s=jnp.tile(cmp_w[l].astype(F32)[:, None, :], (1, 1, 2)),
        onorm2=tile(onorm_a[l], 2), onorm4=tile(onorm_a[l], 4), lam=lam_a[l].astype(F32),
        eg=jnp.asarray(eg, BF16),
    )


def _toeplitz(tab_h, noff, T, mask):
    H, nt = tab_h.shape
    seg = 2 * T - 1
    padded = jnp.pad(tab_h, ((0, 0), (T - 1, max(0, noff * T - nt))))
    tiles = []
    for d in range(noff):
        u = padded[:, d * T:d * T + seg][:, ::-1]
        flat = jnp.broadcast_to(u[:, None, :], (H, T, seg)).reshape(H, T * seg)
        tiles.append(flat[:, T - 1:T - 1 + T * (seg - 1)].reshape(H, T, seg - 1)[:, :, :T])
    return jnp.where(jnp.asarray(mask)[None], jnp.stack(tiles, axis=1), NEG)


def _shifted_rows(tab_h, width, n_t):
    return jnp.stack([tab_h[:, t + 1:t + 1 + width][:, ::-1] for t in range(n_t)], axis=1)


def _prompt_layer(x3, wp, tabs, lam_init, T):
    B, S, D = x3.shape
    pj = _inproj(x3.reshape(B * S, D), wp, do_cmp=True)
    r3 = lambda a: a.reshape(B, S, a.shape[-1])
    nq = S // T
    o_a = _flash_call(
        functools.partial(_flash_a_kernel, T=T, noff=tabs["noff"], lam_init=lam_init), "flash_a", B, S, T, 2,
        [r3(pj["qa"]), r3(pj["ka"]), r3(pj["va"]), tabs["bias_a"], wp["lam"], wp["onorm2"]],
        [_qspec(T), _kvspec(S, LANES, True), _kvspec(S, LANES, True), _biasspec(tabs["noff"], T),
         _small(wp["lam"]), _small(wp["onorm2"])], 256)
    logf = pj["misc"][:, 18:18 + H_C].reshape(B, S, H_C)
    nr = S // LANES
    lf4 = jnp.swapaxes(logf, 1, 2).reshape(B, H_C, nr, LANES)
    mw = jnp.asarray(np.triu(np.ones((LANES, LANES))), BF16)
    mr = jnp.asarray(np.tril(np.ones((nr, nr)), -1), BF16)
    cparts = _cumsum_call(lf4, mw, mr, 1.0, split=True)
    c_split = jnp.stack([p.reshape(B, 3, 2, S) for p in cparts], axis=-1)
    c_split = jnp.transpose(c_split, (0, 3, 1, 2, 4)).reshape(B, S, 3, 6)
    ones6 = jnp.ones((B, S, 3, 6), BF16)
    zpad = jnp.zeros((B, S, 3, LANES - 12), BF16)
    q_extra = jnp.concatenate([-ones6, c_split, zpad], axis=-1).reshape(B, S, 3 * LANES)
    kc_aug = jnp.concatenate([r3(pj["kc"]).reshape(B, S, 3, LANES), c_split, ones6, zpad], axis=-1)
    kc_aug = kc_aug.reshape(B, S, 3 * 2 * LANES)
    o_c = _flash_call(
        functools.partial(_flash_c_kernel, T=T), "flash_c", B, S, T, 3,
        [r3(pj["qc"]), q_extra, kc_aug, r3(pj["vc"]), tabs["causal"]],
        [_qspec(T), _qspec(T), _kvspec(S, 2 * LANES, True), _kvspec(S, LANES, True), _small(tabs["causal"])], 384)
    NB = S // CMP_BLOCK
    NBP = -(-NB // LANES) * LANES
    padb = lambda a: jnp.pad(a.reshape(B, NB, LANES), ((0, 0), (0, NBP - NB), (0, 0))).astype(BF16)
    o_bc, selt = _cmp_call(r3(pj["qb"]), padb(pj["kcmp"]), padb(pj["vcmp"]), T)
    sel = jnp.swapaxes(selt, 2, 3)
    sel = jnp.swapaxes(sel, 1, 2).reshape(B, S, G_B * NBP).astype(BF16)
    kaug = jnp.concatenate([r3(pj["ksel"]), jnp.broadcast_to(tabs["blockhot"][None], (B, S, NBP))], axis=-1)
    o_bs = _flash_call(
        functools.partial(_flash_b_kernel, T=T, noff=tabs["noff"], selected=True), "flash_bsel", B, S, T, 3,
        [r3(pj["qb"]), sel, kaug, r3(pj["vsel"]), tabs["bias_b"]],
        [_qspec(T), pl.BlockSpec((None, T, G_B * NBP), lambda b, g, i: (b, i, 0)),
         _kvspec(S, LANES + NBP, False), _kvspec(S, LANES, False), _biasspec(tabs["noff"], T)], 384)
    o_bw = _flash_call(
        functools.partial(_flash_b_kernel, T=T, noff=2, selected=False), "flash_bwin", B, S, T, 3,
        [r3(pj["qb"]), r3(pj["kwin"]), r3(pj["vwin"]), tabs["bias_w"]],
        [_qspec(T), _kvspec(S, LANES, False), _kvspec(S, LANES, False), _biasspec(2, T)], 384)
    f2 = lambda a: a.reshape(B * S, a.shape[-1])
    y = _merge_call(x3.reshape(B * S, D), f2(o_a), f2(o_bc), f2(o_bs), f2(o_bw), f2(o_c), pj["misc"], pj["zs"],
                    wp["w_out"], wp["eg"])
    del nq
    w = min(WINDOW, S)
    return (y.reshape(B, S, D), pj["arow"].reshape(B, S, 2, H_A, HD), pj["kvb"].reshape(B, S, 4, G_B, HD),
            pj["win"].reshape(B, S, 2, G_B, HD)[:, S - w:], pj["crow"].reshape(B, S, 2, H_C, HD), logf)


def _prompt_tables(rel_bias, T, S):
    noff = -(-(REL_MAX_DIST - 1) // T) + 1
    noff = min(noff, S // T)
    nt = noff * T + 1
    bucket = _rel_bucket(jnp.arange(nt, dtype=jnp.int32))
    tab = jnp.take(rel_bias.astype(F32), bucket, axis=0).T
    far = rel_bias.astype(F32)[N_BUCKETS - 1][:, None]
    i = np.arange(T)[:, None]
    j = np.arange(T)[None, :]
    rel = np.stack([d * T + i - j for d in range(noff)])
    tab_s = (tab - far) * LOG2E
    heads_b = np.asarray([H_A + h for h in SLAB_HEADS])
    tab_b = jnp.stack([tab_s[h] for h in heads_b])
    tab_w = jnp.stack([tab[h] for h in heads_b]) * LOG2E
    bias_a = _toeplitz(tab_s[:H_A], noff, T, rel >= 0)
    bias_b = _toeplitz(tab_b, noff, T, rel >= 0)
    relw = np.stack([d * T + i - j for d in range(2)])
    bias_w = _toeplitz(tab_w, 2, T, (relw >= 0) & (relw < WINDOW))
    causal = jnp.asarray(np.where(i >= j, 0.0, NEG), F32)
    NB = S // CMP_BLOCK
    NBP = -(-NB // LANES) * LANES
    hot = (np.arange(S)[:, None] // CMP_BLOCK == np.arange(NBP)[None, :]) * 1e30
    return dict(noff=noff, bias_a=bias_a, bias_b=bias_b, bias_w=bias_w, causal=causal,
                blockhot=jnp.asarray(hot, BF16))


def _sample_tables(rel_bias, P, PG):
    NP = P // PAGE
    n_near_pages = min(NP, -(-REL_MAX_DIST // (PAGE * PG)) * PG)
    assert n_near_pages % PG == 0 or NP == n_near_pages
    nt = n_near_pages * PAGE + 16
    bucket = _rel_bucket(jnp.arange(nt, dtype=jnp.int32))
    rb = rel_bias.astype(F32)
    tab = jnp.take(rb, bucket, axis=0).T
    far = rb[N_BUCKETS - 1][:, None]
    tab_s = (tab - far) * LOG2E
    tab_a = tab_s[:H_A]
    tab_b = jnp.stack([tab_s[H_A + h] for h in SLAB_HEADS])
    wn = n_near_pages * PAGE
    flat = lambda a: a.reshape(a.shape[0] * a.shape[1], a.shape[2])
    near_a = flat(_shifted_rows(tab_a, wn, 8))
    new_a = flat(_new_rows(tab_a))
    return dict(n_near_pages=n_near_pages,
                near_a=jnp.concatenate([near_a, near_a], axis=0), new_a=jnp.concatenate([new_a, new_a], axis=0),
                near_b=flat(_shifted_rows(tab_b, wn, 8)), new_b=flat(_new_rows(tab_b)),
                tab_w=jnp.stack([tab[H_A + h] for h in SLAB_HEADS]) * LOG2E)


def _new_rows(tab_h):
    rows = [jnp.pad(tab_h[:, :t + 1][:, ::-1], ((0, 0), (0, PAGE - t - 1)), constant_values=NEG) for t in range(8)]
    return jnp.stack(rows, axis=1)


def _rows_from_lanes(q, bands, width):
    lane = np.arange(q.shape[-1])
    m = np.stack([(lane >= lo) & (lane < lo + width) for lo in bands])
    return jnp.where(jnp.asarray(m)[None, :, None, :], q[:, None], jnp.zeros((), q.dtype)).reshape(
        q.shape[0], len(bands) * 8, q.shape[-1])


def _pad_rows(a, n):
    return jnp.pad(a, ((0, 0), (0, n - a.shape[1]), (0, 0)))


def _sample_layer(l, x3, caches, page_table, wp, stabs, wtabs, lam_init):
    cache_a, cache_b, cache_c, cache_lf4, state_win = caches
    B, S8, D = x3.shape
    NP = page_table.shape[1]
    P = NP * PAGE
    pj = _inproj(x3.reshape(B * S8, D), wp, do_cmp=False)
    r3 = lambda a: a.reshape(B, S8, a.shape[-1])
    PG = min(DECODE_PAGES, NP)
    n_near = stabs["n_near_pages"] // PG if stabs["n_near_pages"] >= PG else 1
    split_steps = lambda a: jnp.swapaxes(a.reshape(a.shape[0], -1, PG * PAGE), 0, 1)
    bcast = lambda a: jnp.broadcast_to(a[None], (B,) + a.shape)
    pool_maps = [functools.partial(lambda b, j, pt, i: (l, pt[b, j * PG + i], 0), i=i) for i in range(PG)]

    qa_rows = _rows_from_lanes(r3(pj["qa"]), [h * HD + mp * DA for mp in range(2) for h in range(H_A)], DA)
    arow = r3(pj["arow"])
    near_a = split_steps(stabs["near_a"])
    o_a = _decode_call(
        "decode_a", cache_a, pool_maps, page_table, (512, 0), 0, 256, 256, 256, "a", qa_rows,
        _pad_rows(arow[..., 0:256], PAGE).astype(BF16), _pad_rows(arow[..., 256:512], PAGE).astype(BF16),
        bcast(stabs["new_a"]), [near_a, wp["lam"], wp["onorm4"]],
        [pl.BlockSpec(near_a.shape, lambda b, j, pt: (0, 0, 0)), pl.BlockSpec(wp["lam"].shape, lambda b, j, pt: (0, 0)),
         pl.BlockSpec(wp["onorm4"].shape, lambda b, j, pt: (0, 0))], n_near, 256, lam_init)

    lf_pages = _gather_logf(cache_lf4, l, page_table)
    lf4 = jnp.swapaxes(lf_pages.reshape(B, P, H_C), 1, 2).reshape(B, H_C, NP, PAGE)
    msu = jnp.asarray(np.tril(np.ones((LANES, LANES)), -1), BF16)
    mpu = jnp.asarray(np.triu(np.ones((NP, NP)), 1), BF16)
    c_past = _cumsum_call(lf4, msu, mpu, -1.0).reshape(B, H_C, P)
    c_past = jnp.pad(c_past, ((0, 0), (0, 8 - H_C), (0, 0)))
    logf = pj["misc"][:, 18:18 + H_C].reshape(B, S8, H_C)
    lfn = jnp.pad(jnp.swapaxes(logf, 1, 2), ((0, 0), (0, 8 - H_C), (0, LANES - S8)))[:, None]
    mw = jnp.asarray(np.triu(np.ones((LANES, LANES))), BF16)
    c_new = _cumsum_call(lfn, mw, jnp.zeros((8, 8), BF16), 1.0)[:, 0]
    cq_rows = c_new[:, :H_C, :S8].reshape(B, H_C * S8, 1)
    qc_rows = _rows_from_lanes(r3(pj["qc"]), [h * HD for h in range(H_C)], HD)
    crow = r3(pj["crow"])
    jj = np.arange(PAGE)
    tq = np.tile(np.arange(8), H_C)
    causal_new = jnp.asarray(np.where((jj[None, :] <= tq[:, None]) & (jj[None, :] < 8), 0.0, NEG), F32)
    o_c = _decode_call(
        "decode_c", cache_c, pool_maps, page_table, (768, 0), 0, 384, 384, 384, "c", qc_rows,
        _pad_rows(crow[..., 0:384], PAGE).astype(BF16), _pad_rows(crow[..., 384:768], PAGE).astype(BF16),
        bcast(causal_new), [c_past, c_new, cq_rows],
        [pl.BlockSpec((None, 8, PG * PAGE), lambda b, j, pt: (b, 0, j)),
         pl.BlockSpec((None, 8, LANES), lambda b, j, pt: (b, 0, 0)),
         pl.BlockSpec((None, H_C * S8, 1), lambda b, j, pt: (b, 0, 0))], 0, 384)

    gmask = jnp.asarray(np.arange(LANES)[None, :] // HD == np.arange(G_B)[:, None])
    qb5 = jnp.where(gmask[None, None, None], r3(pj["qb"]).reshape(B, S8, R_B, 1, LANES), jnp.zeros((), BF16))
    qb_rows = jnp.transpose(qb5, (0, 2, 3, 1, 4)).reshape(B, H_B * 8, LANES)
    kvb = r3(pj["kvb"])
    kvb_pad = _pad_rows(kvb, PAGE)
    o_bc_rows, selm1 = _scmp_call(cache_b, l, page_table, wp["cmpw_s"], kvb_pad[..., 0:256], qb_rows, P)
    rowi = np.arange(H_B * 8)
    lanei = np.arange(LANES)
    keep = jnp.asarray(((rowi[:, None] >> 3) & 1) == (lanei[None, :] >> 6))
    x = jnp.where(keep[None], o_bc_rows, 0.0).reshape(B, R_B, G_B, 8, LANES)
    o_bc = jnp.transpose(x[:, :, 0] + x[:, :, 1], (0, 2, 1, 3)).reshape(B, 8, R_B * LANES)
    selrows = jnp.broadcast_to(selm1[:, None], (B, R_B, G_B, 8, selm1.shape[-1])).reshape(B, H_B * 8, -1)
    amask = jnp.repeat(selrows[..., :2 * NP], CMP_BLOCK, axis=-1) * 1e30
    amask_new = jnp.broadcast_to(selrows[..., 2 * NP:2 * NP + 1], (B, H_B * 8, PAGE)) * 1e30
    near_b = split_steps(stabs["near_b"])
    o_bs = _decode_call(
        "decode_bsel", cache_b, pool_maps, page_table, (256, 1), 0, 128, 128, 128, "bs", qb_rows,
        kvb_pad[..., 256:384].astype(BF16), kvb_pad[..., 384:512].astype(BF16),
        stabs["new_b"][None] + amask_new, [near_b, amask.astype(BF16)],
        [pl.BlockSpec(near_b.shape, lambda b, j, pt: (0, 0, 0)),
         pl.BlockSpec((None, H_B * 8, PG * PAGE), lambda b, j, pt: (b, 0, j))], n_near, 384)
    wb = state_win.shape[2]
    npw = wb // PAGE
    win_t = jnp.transpose(state_win, (0, 1, 3, 4, 5, 2)).reshape(state_win.shape[0], B, 2 * G_B * HD, wb)
    pt_w = jnp.zeros((B, npw), jnp.int32)
    win_maps = [functools.partial(lambda b, j, pt, i: (l, b, i), i=i) for i in range(npw)]
    win = r3(pj["win"])
    win_pad = _pad_rows(win, PAGE)
    near_w = wtabs["near_w"][None]
    o_bw = _decode_call(
        "decode_bwin", win_t, win_maps, pt_w, (256, 0), 0, 128, 128, 128, "bw", qb_rows,
        win_pad[..., 0:128].astype(BF16), win_pad[..., 128:256].astype(BF16),
        bcast(wtabs["new_w"]), [near_w], [pl.BlockSpec(near_w.shape, lambda b, j, pt: (0, 0, 0))], 1, 384)

    f2 = lambda a: a.reshape(B * S8, a.shape[-1])
    y = _merge_call(x3.reshape(B * S8, D), f2(o_a), f2(o_bc), f2(o_bs), f2(o_bw), f2(o_c), pj["misc"], pj["zs"],
                    wp["w_out"], wp["eg"])
    win_all = jnp.concatenate([state_win[l].reshape(B, wb, 2 * G_B * HD), win], axis=1)[:, -wb:]
    return (y.reshape(B, S8, D), arow.reshape(B, S8, 2, H_A, HD), kvb.reshape(B, S8, 4, G_B, HD),
            win_all.reshape(B, wb, 2, G_B, HD), crow.reshape(B, S8, 2, H_C, HD), logf)


def _window_tables(stabs, wb):
    tab_w = stabs["tab_w"]
    assert tab_w.shape[1] >= wb + 8
    rel = wb + np.arange(8)[:, None] - np.arange(wb)[None, :]
    near_w = jnp.where(jnp.asarray(rel < WINDOW)[None], _shifted_rows(tab_w, wb, 8), NEG)
    new_w = _new_rows(tab_w)
    flat = lambda a: a.reshape(a.shape[0] * a.shape[1], a.shape[2])
    return dict(near_w=flat(near_w), new_w=flat(new_w))


def kernel(x_prompt, x_sample, cache_a_kv, cache_b_kv, cache_c_kv, cache_c_logf, state_b_win, page_table,
           rel_bias, norm_g, w_in, w_out, qk_a, qk_b, qk_c, onorm_a, lam_a, cmp_w, b_f):
    depth = w_in.shape[0]
    B, S, _ = x_prompt.shape
    DB, S8, _ = x_sample.shape
    assert S8 == 8 and S % 512 == 0 and state_b_win.shape[2] == WINDOW
    T = 512
    NP = page_table.shape[1]
    P = NP * PAGE
    n_pool = cache_a_kv.shape[1]
    ptabs = _prompt_tables(rel_bias, T, S)
    stabs = _sample_tables(rel_bias, P, min(DECODE_PAGES, NP))
    wtabs = _window_tables(stabs, state_b_win.shape[2])
    fmajor = lambda c: jnp.transpose(c, (0, 1, 3, 4, 5, 2)).reshape(depth, n_pool, -1, PAGE)
    caches = (fmajor(cache_a_kv), fmajor(cache_b_kv), fmajor(cache_c_kv),
              cache_c_logf.reshape(depth, n_pool, 1, PAGE * H_C), state_b_win)
    page_table = page_table.astype(jnp.int32)
    yp, ys = x_prompt, x_sample
    outs = [[] for _ in range(10)]
    for l in range(depth):
        lam_init = 0.8 - 0.6 * math.exp(-0.3 * l)
        wp = _prep_layer(l, norm_g, w_in, w_out, qk_a, qk_b, qk_c, onorm_a, lam_a, cmp_w, b_f)
        yp, a_r, b_r, w_r, c_r, l_r = _prompt_layer(yp, wp, ptabs, lam_init, T)
        for k, v in zip((0, 2, 4, 6, 8), (a_r, b_r, w_r, c_r, l_r)):
            outs[k].append(v)
        ys, a_r, b_r, w_r, c_r, l_r = _sample_layer(l, ys, caches, page_table, wp, stabs, wtabs, lam_init)
        for k, v in zip((1, 3, 5, 7, 9), (a_r, b_r, w_r, c_r, l_r)):
            outs[k].append(v)
    return (yp, ys) + tuple(jnp.stack(o) for o in outs)
```

```python
import functools
import math

import numpy as np
import jax
import jax.numpy as jnp
from jax import lax
from jax.experimental import pallas as pl
from jax.experimental.pallas import tpu as pltpu

F32 = jnp.float32
BF16 = jnp.bfloat16

HD = 64
H_A = 4
DA = HD // 2
H_B = 6
G_B = 2
R_B = H_B // G_B
H_C = 6
CMP_BLOCK = 64
TOPK = 16
WINDOW = 512
FORCE_BONUS = float(R_B + 1)
N_BUCKETS = 32
REL_MAX_DIST = 1024
EPS = 1e-6
PAGE = 128

LOG2E = 1.4426950408889634
NEG = -1e30
M_INIT = -0.5e30
LANES = 128
VMEM_LIMIT = 56 * 1024 * 1024
DECODE_PAGES = 16

O_QB, O_KVB, O_WIN, O_GB, O_ZB, O_QC, O_FC, O_ZC = 1024, 1408, 1920, 2176, 2194, 2578, 3730, 3736
PROJ_W = 4120
P_QB, P_KVB, P_WIN, P_ZB, P_QC, P_ZC, P_MISC, P_END = 1024, 1408, 1920, 2176, 2560, 3712, 4096, 4224
SLAB_HEADS = [g * R_B + r for r in range(R_B) for g in range(G_B)]


def _cparams(n_grid):
    return pltpu.CompilerParams(dimension_semantics=("arbitrary",) * n_grid, vmem_limit_bytes=VMEM_LIMIT)


def _dot(a, b):
    return jnp.dot(a, b, preferred_element_type=F32)


def _dot_nt(a, b):
    return lax.dot_general(a, b, (((1,), (1,)), ((), ())), preferred_element_type=F32)


def _split3(x):
    x1 = x.astype(BF16)
    r = x - x1.astype(F32)
    x2 = r.astype(BF16)
    x3 = (r - x2.astype(F32)).astype(BF16)
    return x1, x2, x3


def _dot3(x, m):
    x1, x2, x3 = _split3(x)
    return _dot(x1, m) + _dot(x2, m) + _dot(x3, m)


def _dot3_l(m, x):
    x1, x2, x3 = _split3(x)
    return _dot(m, x1) + _dot(m, x2) + _dot(m, x3)


def _rel_bucket(rel):
    n = jnp.maximum(rel, 0)
    max_exact = N_BUCKETS // 2
    nf = jnp.maximum(n, 1).astype(F32)
    large = max_exact + (jnp.log(nf / max_exact) / math.log(REL_MAX_DIST / max_exact)
                         * (N_BUCKETS - max_exact)).astype(jnp.int32)
    return jnp.where(n < max_exact, n, jnp.minimum(large, N_BUCKETS - 1))


def _inproj_kernel(x_ref, w_ref, g_in, ga_q, ga_k, gb_q, gb_k, gc_q, gc_k, bf_row, cmpw_ref, bd32_ref, bd64_ref,
                   qa_o, arow_o, ka_o, va_o, qb_o, kvb_o, ksel_o, vsel_o, kcmp_o, vcmp_o, win_o, kwin_o, vwin_o,
                   qc_o, crow_o, kc_o, vc_o, zs_o, misc_o, *, tm, do_cmp, fmajor):
    x = x_ref[...]
    ms = jnp.mean(x * x, axis=-1, keepdims=True)
    h = (x * lax.rsqrt(ms + EPS) * g_in[...]).astype(BF16)

    def put_wide(ref, y, fill):
        for j in range(y.shape[1] // LANES):
            ref[:, 2 * j * LANES:(2 * j + 1) * LANES] = y[:, j * LANES:(j + 1) * LANES].astype(BF16)
            ref[:, (2 * j + 1) * LANES:(2 * j + 2) * LANES] = jnp.full((y.shape[0], LANES), fill, BF16)

    def put_values(ref, y):
        put_wide(ref, y, 1.0)

    def put_rows(ref, c0, y):
        if fmajor:
            ref[c0:c0 + y.shape[1], :] = y.T
        else:
            ref[:, c0:c0 + y.shape[1]] = y

    def proj(c0, c1):
        return _dot(h, w_ref[:, c0:c1])

    def segnorm(y, bd, seg):
        ss = _dot((y * y).astype(BF16), bd)
        return y * lax.rsqrt(ss * (1.0 / seg) + EPS)

    def silu(y):
        return y * (1.0 / (1.0 + jnp.exp(-y)))

    bd32 = bd32_ref[...]
    bd64 = bd64_ref[...]
    bd64_1 = bd64[0:128, 0:128]

    y = proj(0, 256)
    qa_o[...] = (segnorm(y, bd32, DA) * ga_q[...]).astype(BF16)
    y = segnorm(proj(256, 512), bd32, DA) * ga_k[...]
    put_rows(arow_o, 0, y)
    ka_o[...] = y.astype(BF16)
    y = proj(512, 768)
    put_rows(arow_o, 256, y)
    put_values(va_o, y)
    zs_o[:, 0:256] = silu(proj(768, 1024))

    qb_o[...] = (segnorm(proj(P_QB, P_KVB), bd64, HD) * gb_q[...]).astype(BF16)
    kcn = segnorm(proj(P_KVB, P_KVB + 128), bd64_1, HD) * gb_k[...]
    put_rows(kvb_o, 0, kcn)
    vcr = proj(P_KVB + 128, P_KVB + 256)
    put_rows(kvb_o, 128, vcr)
    y = segnorm(proj(P_KVB + 256, P_KVB + 384), bd64_1, HD) * gb_k[...]
    put_rows(kvb_o, 256, y)
    ksel_o[...] = y.astype(BF16)
    y = proj(P_KVB + 384, P_KVB + 512)
    put_rows(kvb_o, 384, y)
    put_values(vsel_o, y)
    if do_cmp:
        nbt = tm // CMP_BLOCK
        rowi = lax.broadcasted_iota(jnp.int32, (nbt, tm), 0)
        coli = lax.broadcasted_iota(jnp.int32, (nbt, tm), 1)
        blk = (coli >> 6) == rowi

        def cmp_weights(wl):
            e = jnp.where(blk, jnp.exp(wl - jnp.max(wl, axis=-1, keepdims=True)), 0.0)
            return (e / jnp.sum(e, axis=-1, keepdims=True)).astype(BF16)

        kcmp_o[...] = _dot(cmp_weights(cmpw_ref[0]), kcn.astype(BF16))
        vcmp_o[...] = _dot(cmp_weights(cmpw_ref[1]), vcr.astype(BF16))
    else:
        kcmp_o[...] = jnp.zeros(kcmp_o.shape, F32)
        vcmp_o[...] = jnp.zeros(vcmp_o.shape, F32)
    y = segnorm(proj(P_WIN, P_WIN + 128), bd64_1, HD) * gb_k[...]
    win_o[:, 0:128] = y
    kwin_o[...] = y.astype(BF16)
    y = proj(P_WIN + 128, P_WIN + 256)
    win_o[:, 128:256] = y
    put_values(vwin_o, y)
    zs_o[:, 256:640] = silu(proj(P_ZB, P_QC))

    qc_o[...] = (segnorm(proj(P_QC, P_QC + 384), bd64, HD) * gc_q[...]).astype(BF16)
    y = segnorm(proj(P_QC + 384, P_QC + 768), bd64, HD) * gc_k[...]
    put_rows(crow_o, 0, y)
    put_wide(kc_o, y, 0.0)
    y = proj(P_QC + 768, P_ZC)
    put_rows(crow_o, 384, y)
    put_values(vc_o, y)
    zs_o[:, 640:1024] = silu(proj(P_ZC, P_MISC))

    y = proj(P_MISC, P_END) + bf_row[...]
    lane = lax.broadcasted_iota(jnp.int32, y.shape, 1)
    sig = 1.0 / (1.0 + jnp.exp(-y))
    lsg = jnp.minimum(y, 0.0) - jnp.log(1.0 + jnp.exp(-jnp.abs(y)))
    misc_o[...] = jnp.where(lane < 18, sig, lsg)


def _inproj(x2, wp, *, do_cmp, seq=None):
    T = x2.shape[0]
    tm = min(512, T)
    assert T % tm == 0 and (tm == 512 or not do_cmp)
    nbt = 8
    n = T // tm
    fmajor = seq is not None
    if fmajor:
        assert seq[0] * seq[1] == T and seq[1] % tm == 0
        nsb = seq[1] // tm
    full = lambda a: pl.BlockSpec(a.shape, lambda i: (0,) * a.ndim)
    row = lambda w: pl.BlockSpec((tm, w), lambda i: (i, 0))
    consts = [wp["w_in"], wp["g_in"], wp["ga_q"], wp["ga_k"], wp["gb_q"], wp["gb_k"], wp["gc_q"], wp["gc_k"],
              wp["bf_row"], wp["cmpw_in"], wp["bd32"], wp["bd64"]]
    outs = [("qa", 256, BF16), ("arow", 512, F32), ("ka", 256, BF16), ("va", 512, BF16), ("qb", 384, BF16),
            ("kvb", 512, F32), ("ksel", 128, BF16), ("vsel", 256, BF16), ("kcmp", None, F32), ("vcmp", None, F32),
            ("win", 256, F32), ("kwin", 128, BF16), ("vwin", 256, BF16), ("qc", 384, BF16), ("crow", 768, F32),
            ("kc", 768, BF16), ("vc", 768, BF16), ("zs", 1024, F32), ("misc", 128, F32)]
    out_shape, out_specs = [], []
    for name, w, dt in outs:
        if w is None:
            out_shape.append(jax.ShapeDtypeStruct((n * nbt, 128), dt))
            out_specs.append(pl.BlockSpec((nbt, 128), lambda i: (i, 0)))
        elif fmajor and name in ("arow", "kvb", "crow"):
            out_shape.append(jax.ShapeDtypeStruct((seq[0], w, seq[1]), dt))
            out_specs.append(pl.BlockSpec((None, w, tm), lambda i: (i // nsb, 0, i % nsb)))
        else:
            out_shape.append(jax.ShapeDtypeStruct((T, w), dt))
            out_specs.append(row(w))
    res = pl.pallas_call(
        functools.partial(_inproj_kernel, tm=tm, do_cmp=do_cmp, fmajor=fmajor),
        grid=(n,),
        in_specs=[row(1024)] + [full(c) for c in consts],
        out_specs=out_specs, out_shape=out_shape,
        compiler_params=_cparams(1), name="inproj",
    )(x2, *consts)
    return {k: v for (k, _, _), v in zip(outs, res)}


def _attend(lhs_list, k_ref, v_ref, T, far_lo, near_lo, hi, near_add=None):
    n = len(lhs_list)

    def step(kt, state, near):
        ks = pl.multiple_of(kt * T, T)
        kk = k_ref[pl.ds(ks, T), :]
        vv = v_ref[pl.ds(ks, T), :]
        out = []
        for i in range(n):
            m, acc = state[i]
            s = _dot_nt(lhs_list[i], kk)
            if near:
                s = near_add[i](s, kt)
            m_new = jnp.maximum(m, jnp.max(s, axis=-1, keepdims=True))
            p = jnp.exp2((s - m_new).astype(BF16))
            alpha = jnp.exp2(m - m_new)
            acc = alpha * acc + _dot(p, vv)
            out.append((m_new, acc))
        return tuple(out)

    one = (jnp.full((T, 1), M_INIT, F32), jnp.zeros((T, v_ref.shape[-1]), F32))
    state = (one,) * n
    npair = (near_lo - far_lo) // 2
    state = lax.fori_loop(
        0, npair, lambda j, c: step(far_lo + 2 * j + 1, step(far_lo + 2 * j, c, False), False), state)
    state = lax.fori_loop(far_lo + 2 * npair, near_lo, lambda kt, c: step(kt, c, False), state)
    state = lax.fori_loop(near_lo, hi, lambda kt, c: step(kt, c, True), state)
    return [acc[:, :LANES] / acc[:, LANES:] for _, acc in state]


def _lane_band(x, lo, width):
    lane = lax.broadcasted_iota(jnp.int32, x.shape, 1)
    return jnp.where((lane >= lo) & (lane < lo + width), x, jnp.zeros_like(x))


def _halves(a, b):
    lane = lax.broadcasted_iota(jnp.int32, a.shape, 1)
    return jnp.where(lane < HD, a, b)


def _diff_lambda(lam_ref, lam_init):
    lf = lam_ref[...]
    a = jnp.sum(lf[0:1] * lf[1:2], axis=-1, keepdims=True)
    b = jnp.sum(lf[2:3] * lf[3:4], axis=-1, keepdims=True)
    return jnp.exp(a) - jnp.exp(b) + lam_init


def _flash_a_kernel(q_ref, k_ref, v_ref, bias_ref, lam_ref, onorm_ref, o_ref, *, T, noff, lam_init):
    qi = pl.program_id(2)
    near_lo = jnp.maximum(qi - (noff - 1), 0)
    q = q_ref[...]
    lam = _diff_lambda(lam_ref, lam_init)
    lhs = [_lane_band(q, hl * HD + mp * DA, DA) for hl in range(2) for mp in range(2)]
    add = [functools.partial(lambda s, kt, hl: s + bias_ref[hl, qi - kt], hl=i // 2) for i in range(4)]
    o = _attend(lhs, k_ref, v_ref, T, 0, near_lo, qi + 1, near_add=add)
    o = _halves(o[0] - lam * o[1], o[2] - lam * o[3])
    x2 = o * o
    lane = lax.broadcasted_iota(jnp.int32, o.shape, 1)
    s0 = jnp.sum(jnp.where(lane < HD, x2, 0.0), axis=-1, keepdims=True)
    s1 = jnp.sum(jnp.where(lane < HD, 0.0, x2), axis=-1, keepdims=True)
    ss = jnp.where(lane < HD, s0, s1)
    o_ref[...] = o * lax.rsqrt(ss * (1.0 / HD) + EPS) * onorm_ref[...] * (1.0 - lam_init)


def _flash_c_kernel(q_ref, qx_ref, k_ref, v_ref, mask_ref, o_ref, *, T):
    qi = pl.program_id(2)
    q = q_ref[...]
    qx = qx_ref[...]
    lane = lax.broadcasted_iota(jnp.int32, qx.shape, 1)
    lhs = []
    for hl in range(2):
        own = ((lane >= 3 * hl) & (lane < 3 * hl + 3)) | ((lane >= 6 + 3 * hl) & (lane < 9 + 3 * hl))
        lhs.append(jnp.concatenate([_lane_band(q, hl * HD, HD), jnp.where(own, qx, jnp.zeros_like(qx))], axis=1))
    causal = [lambda s, kt: s + mask_ref[...]] * 2
    o = _attend(lhs, k_ref, v_ref, T, 0, qi, qi + 1, near_add=causal)
    o_ref[...] = _halves(o[0], o[1])


def _flash_b_kernel(*refs, T, noff, selected):
    if selected:
        q_ref, sel_ref, k_ref, v_ref, bias_ref, o_ref = refs
    else:
        q_ref, k_ref, v_ref, bias_ref, o_ref = refs
    qi = pl.program_id(2)
    near_lo = jnp.maximum(qi - (noff - 1), 0)
    far_lo = 0 if selected else near_lo
    q = q_ref[...]
    lhs = [_lane_band(q, g * HD, HD) for g in range(G_B)]
    if selected:
        nbp = sel_ref.shape[-1] // G_B
        lhs = [jnp.concatenate([lhs[g], sel_ref[:, g * nbp:(g + 1) * nbp]], axis=1) for g in range(G_B)]
    add = [functools.partial(lambda s, kt, g: s + bias_ref[g, qi - kt], g=g) for g in range(G_B)]
    o = _attend(lhs, k_ref, v_ref, T, far_lo, near_lo, qi + 1, near_add=add)
    o_ref[...] = _halves(o[0], o[1])


def _flash_call(kernel, name, B, S, T, n_grp, ins, in_specs, out_w):
    return pl.pallas_call(
        kernel, grid=(B, n_grp, S // T), in_specs=in_specs,
        out_specs=pl.BlockSpec((None, T, LANES), lambda b, g, i: (b, i, g)),
        out_shape=jax.ShapeDtypeStruct((B, S, out_w), F32),
        compiler_params=_cparams(3), name=name,
    )(*ins)


def _qspec(T):
    return pl.BlockSpec((None, T, LANES), lambda b, g, i: (b, i, g))


def _kvspec(S, w, per_group):
    if per_group:
        return pl.BlockSpec((None, S, w), lambda b, g, i: (b, 0, g))
    return pl.BlockSpec((None, S, w), lambda b, g, i: (b, 0, 0))


def _biasspec(noff, T):
    return pl.BlockSpec((2, noff, T, T), lambda b, g, i: (g, 0, 0, 0))


def _small(a):
    return pl.BlockSpec(a.shape, lambda b, g, i: (0,) * a.ndim)


def _masked_softmax2(s, mask, axis):
    s = jnp.where(mask, s, NEG)
    m = jnp.max(s, axis=axis, keepdims=True)
    p = jnp.where(mask, jnp.exp2(s - m), 0.0)
    return p / jnp.maximum(jnp.sum(p, axis=axis, keepdims=True), 1e-30)


def _topk_select(score, idx, n_idx, axis):
    def it(_, c):
        sc, sel = c
        mx = jnp.max(sc, axis=axis, keepdims=True)
        first = jnp.min(jnp.where(sc == mx, idx, n_idx), axis=axis, keepdims=True)
        oh = idx == first
        sel = jnp.where(oh, jnp.where(mx >= 0.0, 1.0, sel), sel)
        sc = jnp.where(oh, -2.0, sc)
        return sc, sel

    _, sel = lax.fori_loop(0, TOPK, it, (score, jnp.zeros(score.shape, F32)))
    return sel


def _cmp_kernel(q_ref, kc_ref, vc_ref, oc_ref, selt_ref, *, T, NBP):
    qi = pl.program_id(1)
    q = q_ref[...]
    kc = kc_ref[...]
    vc = vc_ref[...]
    qpos_r = qi * T + lax.broadcasted_iota(jnp.int32, (T, NBP), 0)
    blk_c = lax.broadcasted_iota(jnp.int32, (T, NBP), 1)
    cm = ((blk_c + 1) * CMP_BLOCK - 1) <= qpos_r
    qpos_c = qi * T + lax.broadcasted_iota(jnp.int32, (NBP, T), 1)
    blk_r = lax.broadcasted_iota(jnp.int32, (NBP, T), 0)
    cmt = ((blk_r + 1) * CMP_BLOCK - 1) <= qpos_c
    imp = [jnp.zeros((NBP, T), F32) for _ in range(G_B)]
    for r in range(R_B):
        slab = q[:, r * LANES:(r + 1) * LANES]
        outs = []
        for g in range(G_B):
            qm = _lane_band(slab, g * HD, HD)
            p = _masked_softmax2(_dot_nt(qm, kc), cm, -1)
            outs.append(_dot(p.astype(BF16), vc))
            imp[g] = imp[g] + _masked_softmax2(_dot_nt(kc, qm), cmt, 0)
        oc_ref[:, r * LANES:(r + 1) * LANES] = _halves(outs[0], outs[1])
    qblk = qpos_c >> 6
    forced = (blk_r == qblk) | (blk_r == qblk - 1) | (blk_r == 0)
    valid = blk_r * CMP_BLOCK <= qpos_c
    for g in range(G_B):
        score = jnp.where(valid, imp[g] + jnp.where(forced, FORCE_BONUS, 0.0), -1.0)
        selt_ref[g] = _topk_select(score, blk_r, NBP, 0) - 1.0


def _cmp_call(qb3, kcmp3, vcmp3, T):
    B, S, _ = qb3.shape
    NBP = kcmp3.shape[1]
    return pl.pallas_call(
        functools.partial(_cmp_kernel, T=T, NBP=NBP), grid=(B, S // T),
        in_specs=[pl.BlockSpec((None, T, 384), lambda b, i: (b, i, 0)),
                  pl.BlockSpec((None, NBP, LANES), lambda b, i: (b, 0, 0)),
                  pl.BlockSpec((None, NBP, LANES), lambda b, i: (b, 0, 0))],
        out_specs=[pl.BlockSpec((None, T, 384), lambda b, i: (b, i, 0)),
                   pl.BlockSpec((None, G_B, NBP, T), lambda b, i: (b, 0, 0, i))],
        out_shape=[jax.ShapeDtypeStruct((B, S, 384), F32), jax.ShapeDtypeStruct((B, G_B, NBP, S), F32)],
        compiler_params=_cparams(2), name="nsa_cmp_topk",
    )(qb3, kcmp3, vcmp3)


def _cumsum_kernel(x_ref, mw_ref, mr_ref, *o_refs, nh, sign, split):
    mw = mw_ref[...]
    mr = mr_ref[...]
    ones = jnp.ones((LANES, LANES), BF16)
    for h in range(nh):
        x = x_ref[h]
        tot = _dot3(x, ones)
        c = (_dot3(x, mw) + _dot3_l(mr, tot)) * (sign * LOG2E)
        if split:
            for o_ref, part in zip(o_refs, _split3(c)):
                o_ref[h] = part
        else:
            o_refs[0][h] = c


def _cumsum_call(x4, mw, mr, sign, split=False):
    B, nh, nr, _ = x4.shape
    spec = pl.BlockSpec((None, nh, nr, LANES), lambda b: (b, 0, 0, 0))
    return pl.pallas_call(
        functools.partial(_cumsum_kernel, nh=nh, sign=sign, split=split), grid=(B,),
        in_specs=[spec, pl.BlockSpec(mw.shape, lambda b: (0, 0)), pl.BlockSpec(mr.shape, lambda b: (0, 0))],
        out_specs=[spec] * 3 if split else spec,
        out_shape=[jax.ShapeDtypeStruct(x4.shape, BF16)] * 3 if split else jax.ShapeDtypeStruct(x4.shape, F32),
        compiler_params=_cparams(1), name="logf_cumsum",
    )(x4, mw, mr)


def _merge_kernel(x_ref, oa_ref, obc_ref, obs_ref, obw_ref, oc_ref, misc_ref, zs_ref, w_ref, eg_ref, y_ref):
    g = misc_ref[...]
    g1 = g.astype(BF16)
    g2 = (g - g1.astype(F32)).astype(BF16)

    def gate(c):
        return _dot(g1, eg_ref[c]) + _dot(g2, eg_ref[c])

    ob = gate(0) * obc_ref[...] + gate(1) * obs_ref[...] + gate(2) * obw_ref[...]
    ma = (oa_ref[...] * zs_ref[:, 0:256]).astype(BF16)
    mb = (ob * zs_ref[:, 256:640]).astype(BF16)
    mc = (oc_ref[...] * zs_ref[:, 640:1024]).astype(BF16)
    y_ref[...] = x_ref[...] + _dot(ma, w_ref[0:256, :]) + _dot(mb, w_ref[256:640, :]) + _dot(mc, w_ref[640:1024, :])


def _merge_call(x2, oa, obc, obs, obw, oc, misc, zs, w_out, eg):
    T = x2.shape[0]
    tm = min(512, T)
    row = lambda w: pl.BlockSpec((tm, w), lambda i: (i, 0))
    return pl.pallas_call(
        _merge_kernel, grid=(T // tm,),
        in_specs=[row(1024), row(256), row(384), row(384), row(384), row(384), row(128), row(1024),
                  pl.BlockSpec(w_out.shape, lambda i: (0, 0)), pl.BlockSpec(eg.shape, lambda i: (0, 0, 0))],
        out_specs=row(1024), out_shape=jax.ShapeDtypeStruct((T, 1024), F32),
        compiler_params=_cparams(1), name="merge_outproj",
    )(x2, oa, obc, obs, obw, oc, misc, zs, w_out, eg)


def _gather_rows_kernel(pt_ref, *refs, n):
    out = refs[n]
    for i in range(n):
        out[i] = refs[i][...]


def _gather_logf(cache4, layer, page_table):
    B, NP = page_table.shape
    n = min(16, NP)
    assert NP % n == 0
    w = cache4.shape[-1]
    in_specs = [pl.BlockSpec((None, None, 1, w), functools.partial(
        lambda b, j, pt, i: (layer, pt[b, j * n + i], 0, 0), i=i)) for i in range(n)]
    return pl.pallas_call(
        functools.partial(_gather_rows_kernel, n=n),
        grid_spec=pltpu.PrefetchScalarGridSpec(
            num_scalar_prefetch=1, grid=(B, NP // n), in_specs=in_specs,
            out_specs=pl.BlockSpec((None, n, 1, w), lambda b, j, pt: (b, j, 0, 0))),
        out_shape=jax.ShapeDtypeStruct((B, NP, 1, w), F32),
        compiler_params=_cparams(2), name="gather_logf",
    )(page_table, *([cache4] * n))


def _scmp_kernel(pt_ref, *refs, PG, NP, NBP, P):
    pages = refs[:PG]
    cmpw_ref, new_ref, qb_ref, oc_ref, sel_ref, kcs, vcs = refs[PG:]
    jc = pl.program_id(1)
    ns = pl.num_programs(1)
    nrow = 2 * PG

    @pl.when(jc == 0)
    def _():
        kcs[...] = jnp.zeros(kcs.shape, F32)
        vcs[...] = jnp.zeros(vcs.shape, F32)

    rowi = lax.broadcasted_iota(jnp.int32, (nrow, PAGE), 0)
    coli = lax.broadcasted_iota(jnp.int32, (nrow, PAGE), 1)

    def half_softmax(wl):
        e = jnp.exp(wl - jnp.max(wl, axis=-1, keepdims=True))
        return e / (0.5 * jnp.sum(e, axis=-1, keepdims=True))

    wk = half_softmax(cmpw_ref[0])
    wv = half_softmax(cmpw_ref[1])
    kacc = jnp.zeros((nrow, LANES), F32)
    vacc = jnp.zeros((nrow, LANES), F32)
    for i in range(PG):
        sel = (coli >> 6) + 2 * i == rowi
        pg = pages[i][...]
        kacc = kacc + _dot_nt(jnp.where(sel, wk, 0.0).astype(BF16), pg[0:128, :].astype(BF16))
        vacc = vacc + _dot_nt(jnp.where(sel, wv, 0.0).astype(BF16), pg[128:256, :].astype(BF16))
    r0 = pl.multiple_of(jc * nrow, nrow)
    kcs[pl.ds(r0, nrow), :] = kacc
    vcs[pl.ds(r0, nrow), :] = vacc

    @pl.when(jc == ns - 1)
    def _():
        first = (rowi == 0) & (coli < CMP_BLOCK)
        new = new_ref[...]
        kcs[2 * NP:2 * NP + nrow, :] = _dot(jnp.where(first, wk, 0.0).astype(BF16), new[:, 0:128].astype(BF16))
        vcs[2 * NP:2 * NP + nrow, :] = _dot(jnp.where(first, wv, 0.0).astype(BF16), new[:, 128:256].astype(BF16))
        kc = kcs[...].astype(BF16)
        vc = vcs[...].astype(BF16)
        R = qb_ref.shape[0]
        s = _dot_nt(qb_ref[...], kc)
        t_r = lax.broadcasted_iota(jnp.int32, (R, NBP), 0) & 7
        blk = lax.broadcasted_iota(jnp.int32, (R, NBP), 1)
        cm = ((blk + 1) * CMP_BLOCK - 1) <= (P + t_r)
        p = _masked_softmax2(s, cm, -1)
        oc_ref[...] = _dot(p.astype(BF16), vc)
        qpos = P + lax.broadcasted_iota(jnp.int32, (8, NBP), 0)
        blk8 = lax.broadcasted_iota(jnp.int32, (8, NBP), 1)
        qblk = qpos >> 6
        forced = (blk8 == qblk) | (blk8 == qblk - 1) | (blk8 == 0)
        valid = blk8 * CMP_BLOCK <= qpos
        for g in range(G_B):
            imp = sum(p[(r * G_B + g) * 8:(r * G_B + g) * 8 + 8] for r in range(R_B))
            score = jnp.where(valid, imp + jnp.where(forced, FORCE_BONUS, 0.0), -1.0)
            sel_ref[g] = _topk_select(score, blk8, NBP, -1) - 1.0


def _scmp_call(cache_b, layer, page_table, cmpw_s, new_pad, qb_rows, P):
    B, NP = page_table.shape
    PG = min(DECODE_PAGES, NP)
    assert NP % PG == 0
    NB = 2 * NP + 1
    NBP = -(-(2 * NP + 2 * PG) // LANES) * LANES
    in_specs = [pl.BlockSpec((None, None, 256, PAGE), functools.partial(
        lambda b, j, pt, i: (layer, pt[b, j * PG + i], 0, 0), i=i)) for i in range(PG)]
    in_specs += [pl.BlockSpec(cmpw_s.shape, lambda b, j, pt: (0, 0, 0)),
                 pl.BlockSpec((None, PAGE, 256), lambda b, j, pt: (b, 0, 0)),
                 pl.BlockSpec((None,) + qb_rows.shape[1:], lambda b, j, pt: (b, 0, 0))]
    R = qb_rows.shape[1]
    del NB
    return pl.pallas_call(
        functools.partial(_scmp_kernel, PG=PG, NP=NP, NBP=NBP, P=P),
        grid_spec=pltpu.PrefetchScalarGridSpec(
            num_scalar_prefetch=1, grid=(B, NP // PG), in_specs=in_specs,
            out_specs=[pl.BlockSpec((None, R, LANES), lambda b, j, pt: (b, 0, 0)),
                       pl.BlockSpec((None, G_B, 8, NBP), lambda b, j, pt: (b, 0, 0, 0))],
            scratch_shapes=[pltpu.VMEM((NBP, LANES), F32), pltpu.VMEM((NBP, LANES), F32)]),
        out_shape=[jax.ShapeDtypeStruct((B, R, LANES), F32), jax.ShapeDtypeStruct((B, G_B, 8, NBP), F32)],
        compiler_params=_cparams(2), name="sample_cmp_topk",
    )(page_table, *([cache_b] * PG), cmpw_s, new_pad, qb_rows)


def _decode_kernel(pt_ref, *refs, PG, kc0, KD, vc0, VD, mode, n_near, lam_init):
    pages = refs[:PG]
    rest = list(refs[PG:])
    qb_ref, knew_ref, vnew_ref, addnew_ref = rest[:4]
    rest = rest[4:]
    bias_ref = ck_ref = cnew_ref = cq_ref = amask_ref = lam_ref = onorm_ref = None
    if mode == "a":
        bias_ref, lam_ref, onorm_ref = rest[:3]
        rest = rest[3:]
    elif mode == "c":
        ck_ref, cnew_ref, cq_ref = rest[:3]
        rest = rest[3:]
    elif mode == "bs":
        bias_ref, amask_ref = rest[:2]
        rest = rest[2:]
    else:
        bias_ref = rest[0]
        rest = rest[1:]
    o_ref, m_s, l_s, acc_s = rest
    jc = pl.program_id(1)
    ns = pl.num_programs(1)
    R = qb_ref.shape[0]

    @pl.when(jc == 0)
    def _():
        m_s[...] = jnp.full(m_s.shape, M_INIT, F32)
        l_s[...] = jnp.zeros(l_s.shape, F32)
        acc_s[...] = jnp.zeros(acc_s.shape, F32)

    def rows_of_heads(c):
        return jnp.concatenate([jnp.broadcast_to(c[h:h + 1, :], (8, c.shape[1])) for h in range(R // 8)], axis=0)

    def update(s, v, v_feature_major):
        m = m_s[...]
        m_new = jnp.maximum(m, jnp.max(s, axis=-1, keepdims=True))
        p = jnp.exp2(s - m_new)
        alpha = jnp.exp2(m - m_new)
        l_s[...] = alpha * l_s[...] + jnp.sum(p, axis=-1, keepdims=True)
        pb = p.astype(BF16)
        acc_s[...] = alpha * acc_s[...] + (_dot_nt(pb, v) if v_feature_major else _dot(pb, v))
        m_s[...] = m_new

    q = qb_ref[...]
    kcat = jnp.concatenate([pg[kc0:kc0 + KD, :].astype(BF16) for pg in pages], axis=1)
    vcat = jnp.concatenate([pg[vc0:vc0 + VD, :].astype(BF16) for pg in pages], axis=1)
    s = _dot(q, kcat)
    if mode == "c":
        s = s + cq_ref[...] - rows_of_heads(ck_ref[...])
    if mode == "bs":
        s = s + amask_ref[...].astype(F32)
    if bias_ref is not None:
        e = jc - (ns - n_near)
        s = s + jnp.where(e >= 0, 1.0, 0.0) * bias_ref[jnp.maximum(e, 0)]
    update(s, vcat, True)

    @pl.when(jc == ns - 1)
    def _():
        s2 = _dot_nt(q, knew_ref[...]) + addnew_ref[...]
        if mode == "c":
            s2 = s2 + cq_ref[...] - rows_of_heads(cnew_ref[...])
        update(s2, vnew_ref[...], False)
        o = acc_s[...] / l_s[...]
        if mode == "a":
            lam = _diff_lambda(lam_ref, lam_init)
            half = R // 2
            pd = o[0:half] - lam * o[half:R]
            rowi = lax.broadcasted_iota(jnp.int32, pd.shape, 0)
            lanei = lax.broadcasted_iota(jnp.int32, pd.shape, 1)
            x = jnp.where((rowi >> 3) == (lanei >> 6), pd, 0.0)
            ss = jnp.sum(x * x, axis=-1, keepdims=True)
            y = x * lax.rsqrt(ss * (1.0 / HD) + EPS) * onorm_ref[...] * (1.0 - lam_init)
            o_ref[...] = sum(y[h * 8:(h + 1) * 8] for h in range(H_A))
        elif mode == "c":
            rowi = lax.broadcasted_iota(jnp.int32, o.shape, 0)
            lanei = lax.broadcasted_iota(jnp.int32, o.shape, 1)
            x = jnp.where((rowi >> 3) == (lanei >> 6), o, 0.0)
            o_ref[...] = sum(x[h * 8:(h + 1) * 8] for h in range(H_C))
        else:
            rowi = lax.broadcasted_iota(jnp.int32, o.shape, 0)
            lanei = lax.broadcasted_iota(jnp.int32, o.shape, 1)
            x = jnp.where(((rowi >> 3) & 1) == (lanei >> 6), o, 0.0)
            o_ref[...] = jnp.concatenate(
                [x[(2 * r) * 8:(2 * r) * 8 + 8] + x[(2 * r + 1) * 8:(2 * r + 1) * 8 + 8] for r in range(R_B)], axis=1)


def _decode_call(name, cache, page_maps, page_table, rowblk, kc0, KD, vc0, VD, mode, qb_rows, knew, vnew, addnew,
                 extras, extra_specs, n_near, out_w, lam_init=0.0):
    B, NP = page_table.shape
    PG = len(page_maps)
    assert NP % PG == 0
    R = qb_rows.shape[1]
    bh, bi = rowblk

    def page_spec(f):
        def index(b, j, pt):
            d0, d1, lane_blk = f(b, j, pt)
            return (d0, d1, bi, lane_blk)
        return pl.BlockSpec((None, None, bh, PAGE), index)

    in_specs = [page_spec(f) for f in page_maps]
    bspec = lambda a: pl.BlockSpec((None,) + a.shape[1:], lambda b, j, pt: (b,) + (0,) * (a.ndim - 1))
    in_specs += [bspec(qb_rows), bspec(knew), bspec(vnew), bspec(addnew)] + extra_specs
    return pl.pallas_call(
        functools.partial(_decode_kernel, PG=PG, kc0=kc0, KD=KD, vc0=vc0, VD=VD, mode=mode, n_near=n_near,
                          lam_init=lam_init),
        grid_spec=pltpu.PrefetchScalarGridSpec(
            num_scalar_prefetch=1, grid=(B, NP // PG), in_specs=in_specs,
            out_specs=pl.BlockSpec((None, 8, out_w), lambda b, j, pt: (b, 0, 0)),
            scratch_shapes=[pltpu.VMEM((R, 1), F32), pltpu.VMEM((R, 1), F32), pltpu.VMEM((R, VD), F32)]),
        out_shape=jax.ShapeDtypeStruct((B, 8, out_w), F32),
        compiler_params=_cparams(2), name=name,
    )(page_table, *([cache] * PG), qb_rows, knew, vnew, addnew, *extras)


def _prep_layer(l, norm_g, w_in, w_out, qk_a, qk_b, qk_c, onorm_a, lam_a, cmp_w, b_f):
    wl = w_in[l]
    slab_cols = lambda base: [wl[:, base + h * HD:base + (h + 1) * HD] for h in SLAB_HEADS]
    w = jnp.concatenate(
        [wl[:, 0:O_QB]] + slab_cols(O_QB) + [wl[:, O_KVB:O_GB]] + slab_cols(O_ZB)
        + [wl[:, O_QC:O_FC], wl[:, O_ZC:PROJ_W], wl[:, O_GB:O_GB + 18], wl[:, O_FC:O_FC + H_C],
           jnp.zeros((wl.shape[0], P_END - PROJ_W), wl.dtype)], axis=1).astype(BF16)
    wol = w_out[l]
    wo = jnp.concatenate([wol[0:256]] + [wol[256 + h * HD:256 + (h + 1) * HD] for h in SLAB_HEADS]
                         + [wol[640:1024]], axis=0).astype(BF16)
    tile = lambda v, n: jnp.tile(v.astype(F32), n)[None, :]
    bf_row = jnp.zeros((1, LANES), F32).at[0, 18:18 + H_C].set(b_f[l].astype(F32))
    bd = lambda n, seg: jnp.asarray(np.kron(np.eye(n // seg), np.ones((seg, seg))), BF16)
    eg = np.zeros((3, LANES, 384), np.float32)
    for g in range(G_B):
        for r in range(R_B):
            for c in range(3):
                eg[c, g * 9 + r * 3 + c, (r * G_B + g) * HD:(r * G_B + g + 1) * HD] = 1.0
    return dict(
        w_in=w, w_out=wo, g_in=norm_g[l].astype(F32)[None, :],
        ga_q=tile(qk_a[l, 0], 8) * (DA ** -0.5 * LOG2E), ga_k=tile(qk_a[l, 1], 8),
        gb_q=tile(qk_b[l, 0], 6) * (HD ** -0.5 * LOG2E), gb_k=tile(qk_b[l, 1], 2),
        gc_q=tile(qk_c[l, 0], 6) * (HD ** -0.5 * LOG2E), gc_k=tile(qk_c[l, 1], 6),
        bf_row=bf_row, bd32=bd(256, DA), bd64=bd(384, HD),
        cmpw_in=jnp.tile(cmp_w[l].astype(F32)[:, None, :], (1, 8, 8)),
        cmpw_s=jnp.tile(cmp_w[l].astype(F32)[:, None, :], (1, 1, 2)),
        onorm2=tile(onorm_a[l], 2), onorm4=tile(onorm_a[l], 4), lam=lam_a[l].astype(F32),
        eg=jnp.asarray(eg, BF16),
    )


def _toeplitz(tab_h, noff, T, mask):
    H, nt = tab_h.shape
    seg = 2 * T - 1
    padded = jnp.pad(tab_h, ((0, 0), (T - 1, max(0, noff * T - nt))))
    tiles = []
    for d in range(noff):
        u = padded[:, d * T:d * T + seg][:, ::-1]
        flat = jnp.broadcast_to(u[:, None, :], (H, T, seg)).reshape(H, T * seg)
        tiles.append(flat[:, T - 1:T - 1 + T * (seg - 1)].reshape(H, T, seg - 1)[:, :, :T])
    return jnp.where(jnp.asarray(mask)[None], jnp.stack(tiles, axis=1), NEG)


def _shifted_rows(tab_h, width, n_t):
    return jnp.stack([tab_h[:, t + 1:t + 1 + width][:, ::-1] for t in range(n_t)], axis=1)


def _prompt_layer(x3, wp, tabs, lam_init, T):
    B, S, D = x3.shape
    pj = _inproj(x3.reshape(B * S, D), wp, do_cmp=True, seq=(B, S))
    r3 = lambda a: a.reshape(B, S, a.shape[-1])
    rows_out = lambda a, *dims: jnp.moveaxis(a.reshape((B,) + dims + (S,)), -1, 1)
    nq = S // T
    o_a = _flash_call(
        functools.partial(_flash_a_kernel, T=T, noff=tabs["noff"], lam_init=lam_init), "flash_a", B, S, T, 2,
        [r3(pj["qa"]), r3(pj["ka"]), r3(pj["va"]), tabs["bias_a"], wp["lam"], wp["onorm2"]],
        [_qspec(T), _kvspec(S, LANES, True), _kvspec(S, 2 * LANES, True), _biasspec(tabs["noff"], T),
         _small(wp["lam"]), _small(wp["onorm2"])], 256)
    logf = pj["misc"][:, 18:18 + H_C].reshape(B, S, H_C)
    nr = S // LANES
    lf4 = jnp.swapaxes(logf, 1, 2).reshape(B, H_C, nr, LANES)
    mw = jnp.asarray(np.triu(np.ones((LANES, LANES))), BF16)
    mr = jnp.asarray(np.tril(np.ones((nr, nr)), -1), BF16)
    cparts = _cumsum_call(lf4, mw, mr, 1.0, split=True)
    c_split = jnp.stack([p.reshape(B, 3, 2, S) for p in cparts], axis=-1)
    c_split = jnp.transpose(c_split, (0, 3, 1, 2, 4)).reshape(B, S, 3, 6)
    ones6 = jnp.ones((B, S, 3, 6), BF16)
    zpad = jnp.zeros((B, S, 3, LANES - 12), BF16)
    q_extra = jnp.concatenate([-ones6, c_split, zpad], axis=-1).reshape(B, S, 3 * LANES)
    k_extra = jnp.concatenate([c_split, ones6], axis=-1)
    kc_aug = r3(pj["kc"])
    for g in range(3):
        kc_aug = kc_aug.at[:, :, (2 * g + 1) * LANES:(2 * g + 1) * LANES + 12].set(k_extra[:, :, g])
    o_c = _flash_call(
        functools.partial(_flash_c_kernel, T=T), "flash_c", B, S, T, 3,
        [r3(pj["qc"]), q_extra, kc_aug, r3(pj["vc"]), tabs["causal"]],
        [_qspec(T), _qspec(T), _kvspec(S, 2 * LANES, True), _kvspec(S, 2 * LANES, True), _small(tabs["causal"])], 384)
    NB = S // CMP_BLOCK
    NBP = -(-NB // LANES) * LANES
    padb = lambda a: jnp.pad(a.reshape(B, NB, LANES), ((0, 0), (0, NBP - NB), (0, 0))).astype(BF16)
    o_bc, selt = _cmp_call(r3(pj["qb"]), padb(pj["kcmp"]), padb(pj["vcmp"]), T)
    sel = jnp.swapaxes(selt, 2, 3)
    sel = jnp.swapaxes(sel, 1, 2).reshape(B, S, G_B * NBP).astype(BF16)
    kaug = jnp.concatenate([r3(pj["ksel"]), jnp.broadcast_to(tabs["blockhot"][None], (B, S, NBP))], axis=-1)
    o_bs = _flash_call(
        functools.partial(_flash_b_kernel, T=T, noff=tabs["noff"], selected=True), "flash_bsel", B, S, T, 3,
        [r3(pj["qb"]), sel, kaug, r3(pj["vsel"]), tabs["bias_b"]],
        [_qspec(T), pl.BlockSpec((None, T, G_B * NBP), lambda b, g, i: (b, i, 0)),
         _kvspec(S, LANES + NBP, False), _kvspec(S, 2 * LANES, False), _biasspec(tabs["noff"], T)], 384)
    o_bw = _flash_call(
        functools.partial(_flash_b_kernel, T=T, noff=2, selected=False), "flash_bwin", B, S, T, 3,
        [r3(pj["qb"]), r3(pj["kwin"]), r3(pj["vwin"]), tabs["bias_w"]],
        [_qspec(T), _kvspec(S, LANES, False), _kvspec(S, 2 * LANES, False), _biasspec(2, T)], 384)
    f2 = lambda a: a.reshape(B * S, a.shape[-1])
    y = _merge_call(x3.reshape(B * S, D), f2(o_a), f2(o_bc), f2(o_bs), f2(o_bw), f2(o_c), pj["misc"], pj["zs"],
                    wp["w_out"], wp["eg"])
    del nq
    w = min(WINDOW, S)
    return (y.reshape(B, S, D), rows_out(pj["arow"], 2, H_A, HD), rows_out(pj["kvb"], 4, G_B, HD),
            pj["win"].reshape(B, S, 2 * G_B * HD)[:, S - w:].reshape(B, w, 2, G_B, HD),
            rows_out(pj["crow"], 2, H_C, HD), logf)


def _prompt_tables(rel_bias, T, S):
    noff = -(-(REL_MAX_DIST - 1) // T) + 1
    noff = min(noff, S // T)
    nt = noff * T + 1
    bucket = _rel_bucket(jnp.arange(nt, dtype=jnp.int32))
    tab = jnp.take(rel_bias.astype(F32), bucket, axis=0).T
    far = rel_bias.astype(F32)[N_BUCKETS - 1][:, None]
    i = np.arange(T)[:, None]
    j = np.arange(T)[None, :]
    rel = np.stack([d * T + i - j for d in range(noff)])
    tab_s = (tab - far) * LOG2E
    heads_b = np.asarray([H_A + h for h in SLAB_HEADS])
    tab_b = jnp.stack([tab_s[h] for h in heads_b])
    tab_w = jnp.stack([tab[h] for h in heads_b]) * LOG2E
    bias_a = _toeplitz(tab_s[:H_A], noff, T, rel >= 0)
    bias_b = _toeplitz(tab_b, noff, T, rel >= 0)
    relw = np.stack([d * T + i - j for d in range(2)])
    bias_w = _toeplitz(tab_w, 2, T, (relw >= 0) & (relw < WINDOW))
    causal = jnp.asarray(np.where(i >= j, 0.0, NEG), F32)
    NB = S // CMP_BLOCK
    NBP = -(-NB // LANES) * LANES
    hot = (np.arange(S)[:, None] // CMP_BLOCK == np.arange(NBP)[None, :]) * 1e30
    return dict(noff=noff, bias_a=bias_a, bias_b=bias_b, bias_w=bias_w, causal=causal,
                blockhot=jnp.asarray(hot, BF16))


def _sample_tables(rel_bias, P, PG):
    NP = P // PAGE
    n_near_pages = min(NP, -(-REL_MAX_DIST // (PAGE * PG)) * PG)
    assert n_near_pages % PG == 0 or NP == n_near_pages
    nt = n_near_pages * PAGE + 16
    bucket = _rel_bucket(jnp.arange(nt, dtype=jnp.int32))
    rb = rel_bias.astype(F32)
    tab = jnp.take(rb, bucket, axis=0).T
    far = rb[N_BUCKETS - 1][:, None]
    tab_s = (tab - far) * LOG2E
    tab_a = tab_s[:H_A]
    tab_b = jnp.stack([tab_s[H_A + h] for h in SLAB_HEADS])
    wn = n_near_pages * PAGE
    flat = lambda a: a.reshape(a.shape[0] * a.shape[1], a.shape[2])
    near_a = flat(_shifted_rows(tab_a, wn, 8))
    new_a = flat(_new_rows(tab_a))
    return dict(n_near_pages=n_near_pages,
                near_a=jnp.concatenate([near_a, near_a], axis=0), new_a=jnp.concatenate([new_a, new_a], axis=0),
                near_b=flat(_shifted_rows(tab_b, wn, 8)), new_b=flat(_new_rows(tab_b)),
                tab_w=jnp.stack([tab[H_A + h] for h in SLAB_HEADS]) * LOG2E)


def _new_rows(tab_h):
    rows = [jnp.pad(tab_h[:, :t + 1][:, ::-1], ((0, 0), (0, PAGE - t - 1)), constant_values=NEG) for t in range(8)]
    return jnp.stack(rows, axis=1)


def _rows_from_lanes(q, bands, width):
    lane = np.arange(q.shape[-1])
    m = np.stack([(lane >= lo) & (lane < lo + width) for lo in bands])
    return jnp.where(jnp.asarray(m)[None, :, None, :], q[:, None], jnp.zeros((), q.dtype)).reshape(
        q.shape[0], len(bands) * 8, q.shape[-1])


def _pad_rows(a, n):
    return jnp.pad(a, ((0, 0), (0, n - a.shape[1]), (0, 0)))


def _sample_layer(l, x3, caches, page_table, wp, stabs, wtabs, lam_init):
    cache_a, cache_b, cache_c, cache_lf4, state_win = caches
    B, S8, D = x3.shape
    NP = page_table.shape[1]
    P = NP * PAGE
    pj = _inproj(x3.reshape(B * S8, D), wp, do_cmp=False)
    r3 = lambda a: a.reshape(B, S8, a.shape[-1])
    PG = min(DECODE_PAGES, NP)
    n_near = stabs["n_near_pages"] // PG if stabs["n_near_pages"] >= PG else 1
    split_steps = lambda a: jnp.swapaxes(a.reshape(a.shape[0], -1, PG * PAGE), 0, 1)
    bcast = lambda a: jnp.broadcast_to(a[None], (B,) + a.shape)
    pool_maps = [functools.partial(lambda b, j, pt, i: (l, pt[b, j * PG + i], 0), i=i) for i in range(PG)]

    qa_rows = _rows_from_lanes(r3(pj["qa"]), [h * HD + mp * DA for mp in range(2) for h in range(H_A)], DA)
    arow = r3(pj["arow"])
    near_a = split_steps(stabs["near_a"])
    o_a = _decode_call(
        "decode_a", cache_a, pool_maps, page_table, (512, 0), 0, 256, 256, 256, "a", qa_rows,
        _pad_rows(arow[..., 0:256], PAGE).astype(BF16), _pad_rows(arow[..., 256:512], PAGE).astype(BF16),
        bcast(stabs["new_a"]), [near_a, wp["lam"], wp["onorm4"]],
        [pl.BlockSpec(near_a.shape, lambda b, j, pt: (0, 0, 0)), pl.BlockSpec(wp["lam"].shape, lambda b, j, pt: (0, 0)),
         pl.BlockSpec(wp["onorm4"].shape, lambda b, j, pt: (0, 0))], n_near, 256, lam_init)

    lf_pages = _gather_logf(cache_lf4, l, page_table)
    lf4 = jnp.swapaxes(lf_pages.reshape(B, P, H_C), 1, 2).reshape(B, H_C, NP, PAGE)
    msu = jnp.asarray(np.tril(np.ones((LANES, LANES)), -1), BF16)
    mpu = jnp.asarray(np.triu(np.ones((NP, NP)), 1), BF16)
    c_past = _cumsum_call(lf4, msu, mpu, -1.0).reshape(B, H_C, P)
    c_past = jnp.pad(c_past, ((0, 0), (0, 8 - H_C), (0, 0)))
    logf = pj["misc"][:, 18:18 + H_C].reshape(B, S8, H_C)
    lfn = jnp.pad(jnp.swapaxes(logf, 1, 2), ((0, 0), (0, 8 - H_C), (0, LANES - S8)))[:, None]
    mw = jnp.asarray(np.triu(np.ones((LANES, LANES))), BF16)
    c_new = _cumsum_call(lfn, mw, jnp.zeros((8, 8), BF16), 1.0)[:, 0]
    cq_rows = c_new[:, :H_C, :S8].reshape(B, H_C * S8, 1)
    qc_rows = _rows_from_lanes(r3(pj["qc"]), [h * HD for h in range(H_C)], HD)
    crow = r3(pj["crow"])
    jj = np.arange(PAGE)
    tq = np.tile(np.arange(8), H_C)
    causal_new = jnp.asarray(np.where((jj[None, :] <= tq[:, None]) & (jj[None, :] < 8), 0.0, NEG), F32)
    o_c = _decode_call(
        "decode_c", cache_c, pool_maps, page_table, (768, 0), 0, 384, 384, 384, "c", qc_rows,
        _pad_rows(crow[..., 0:384], PAGE).astype(BF16), _pad_rows(crow[..., 384:768], PAGE).astype(BF16),
        bcast(causal_new), [c_past, c_new, cq_rows],
        [pl.BlockSpec((None, 8, PG * PAGE), lambda b, j, pt: (b, 0, j)),
         pl.BlockSpec((None, 8, LANES), lambda b, j, pt: (b, 0, 0)),
         pl.BlockSpec((None, H_C * S8, 1), lambda b, j, pt: (b, 0, 0))], 0, 384)

    gmask = jnp.asarray(np.arange(LANES)[None, :] // HD == np.arange(G_B)[:, None])
    qb5 = jnp.where(gmask[None, None, None], r3(pj["qb"]).reshape(B, S8, R_B, 1, LANES), jnp.zeros((), BF16))
    qb_rows = jnp.transpose(qb5, (0, 2, 3, 1, 4)).reshape(B, H_B * 8, LANES)
    kvb = r3(pj["kvb"])
    kvb_pad = _pad_rows(kvb, PAGE)
    o_bc_rows, selm1 = _scmp_call(cache_b, l, page_table, wp["cmpw_s"], kvb_pad[..., 0:256], qb_rows, P)
    rowi = np.arange(H_B * 8)
    lanei = np.arange(LANES)
    keep = jnp.asarray(((rowi[:, None] >> 3) & 1) == (lanei[None, :] >> 6))
    x = jnp.where(keep[None], o_bc_rows, 0.0).reshape(B, R_B, G_B, 8, LANES)
    o_bc = jnp.transpose(x[:, :, 0] + x[:, :, 1], (0, 2, 1, 3)).reshape(B, 8, R_B * LANES)
    selrows = jnp.broadcast_to(selm1[:, None], (B, R_B, G_B, 8, selm1.shape[-1])).reshape(B, H_B * 8, -1)
    amask = jnp.repeat(selrows[..., :2 * NP], CMP_BLOCK, axis=-1) * 1e30
    amask_new = jnp.broadcast_to(selrows[..., 2 * NP:2 * NP + 1], (B, H_B * 8, PAGE)) * 1e30
    near_b = split_steps(stabs["near_b"])
    o_bs = _decode_call(
        "decode_bsel", cache_b, pool_maps, page_table, (256, 1), 0, 128, 128, 128, "bs", qb_rows,
        kvb_pad[..., 256:384].astype(BF16), kvb_pad[..., 384:512].astype(BF16),
        stabs["new_b"][None] + amask_new, [near_b, amask.astype(BF16)],
        [pl.BlockSpec(near_b.shape, lambda b, j, pt: (0, 0, 0)),
         pl.BlockSpec((None, H_B * 8, PG * PAGE), lambda b, j, pt: (b, 0, j))], n_near, 384)
    wb = state_win.shape[2]
    npw = wb // PAGE
    win_t = jnp.transpose(state_win, (0, 1, 3, 4, 5, 2)).reshape(state_win.shape[0], B, 2 * G_B * HD, wb)
    pt_w = jnp.zeros((B, npw), jnp.int32)
    win_maps = [functools.partial(lambda b, j, pt, i: (l, b, i), i=i) for i in range(npw)]
    win = r3(pj["win"])
    win_pad = _pad_rows(win, PAGE)
    near_w = wtabs["near_w"][None]
    o_bw = _decode_call(
        "decode_bwin", win_t, win_maps, pt_w, (256, 0), 0, 128, 128, 128, "bw", qb_rows,
        win_pad[..., 0:128].astype(BF16), win_pad[..., 128:256].astype(BF16),
        bcast(wtabs["new_w"]), [near_w], [pl.BlockSpec(near_w.shape, lambda b, j, pt: (0, 0, 0))], 1, 384)

    f2 = lambda a: a.reshape(B * S8, a.shape[-1])
    y = _merge_call(x3.reshape(B * S8, D), f2(o_a), f2(o_bc), f2(o_bs), f2(o_bw), f2(o_c), pj["misc"], pj["zs"],
                    wp["w_out"], wp["eg"])
    win_all = jnp.concatenate([state_win[l].reshape(B, wb, 2 * G_B * HD), win], axis=1)[:, -wb:]
    return (y.reshape(B, S8, D), arow.reshape(B, S8, 2, H_A, HD), kvb.reshape(B, S8, 4, G_B, HD),
            win_all.reshape(B, wb, 2, G_B, HD), crow.reshape(B, S8, 2, H_C, HD), logf)


def _window_tables(stabs, wb):
    tab_w = stabs["tab_w"]
    assert tab_w.shape[1] >= wb + 8
    rel = wb + np.arange(8)[:, None] - np.arange(wb)[None, :]
    near_w = jnp.where(jnp.asarray(rel < WINDOW)[None], _shifted_rows(tab_w, wb, 8), NEG)
    new_w = _new_rows(tab_w)
    flat = lambda a: a.reshape(a.shape[0] * a.shape[1], a.shape[2])
    return dict(near_w=flat(near_w), new_w=flat(new_w))


def kernel(x_prompt, x_sample, cache_a_kv, cache_b_kv, cache_c_kv, cache_c_logf, state_b_win, page_table,
           rel_bias, norm_g, w_in, w_out, qk_a, qk_b, qk_c, onorm_a, lam_a, cmp_w, b_f):
    depth = w_in.shape[0]
    B, S, _ = x_prompt.shape
    DB, S8, _ = x_sample.shape
    assert S8 == 8 and S % 512 == 0 and state_b_win.shape[2] == WINDOW
    T = 512
    NP = page_table.shape[1]
    P = NP * PAGE
    n_pool = cache_a_kv.shape[1]
    ptabs = _prompt_tables(rel_bias, T, S)
    stabs = _sample_tables(rel_bias, P, min(DECODE_PAGES, NP))
    wtabs = _window_tables(stabs, state_b_win.shape[2])
    fmajor = lambda c: jnp.transpose(c, (0, 1, 3, 4, 5, 2)).reshape(depth, n_pool, -1, PAGE)
    caches = (fmajor(cache_a_kv), fmajor(cache_b_kv), fmajor(cache_c_kv),
              cache_c_logf.reshape(depth, n_pool, 1, PAGE * H_C), state_b_win)
    page_table = page_table.astype(jnp.int32)
    yp, ys = x_prompt, x_sample
    outs = [[] for _ in range(10)]
    for l in range(depth):
        lam_init = 0.8 - 0.6 * math.exp(-0.3 * l)
        wp = _prep_layer(l, norm_g, w_in, w_out, qk_a, qk_b, qk_c, onorm_a, lam_a, cmp_w, b_f)
        yp, a_r, b_r, w_r, c_r, l_r = _prompt_layer(yp, wp, ptabs, lam_init, T)
        for k, v in zip((0, 2, 4, 6, 8), (a_r, b_r, w_r, c_r, l_r)):
            outs[k].append(v)
        ys, a_r, b_r, w_r, c_r, l_r = _sample_layer(l, ys, caches, page_table, wp, stabs, wtabs, lam_init)
        for k, v in zip((1, 3, 5, 7, 9), (a_r, b_r, w_r, c_r, l_r)):
            outs[k].append(v)
    return (yp, ys) + tuple(jnp.stack(o) for o in outs)
```

```python
import functools
import math

import numpy as np
import jax
import jax.numpy as jnp
from jax import lax
from jax.experimental import pallas as pl
from jax.experimental.pallas import tpu as pltpu

F32 = jnp.float32
BF16 = jnp.bfloat16

HD = 64
H_A = 4
DA = HD // 2
H_B = 6
G_B = 2
R_B = H_B // G_B
H_C = 6
CMP_BLOCK = 64
TOPK = 16
WINDOW = 512
FORCE_BONUS = float(R_B + 1)
N_BUCKETS = 32
REL_MAX_DIST = 1024
EPS = 1e-6
PAGE = 128

LOG2E = 1.4426950408889634
NEG = -1e30
M_INIT = -0.5e30
LANES = 128
VMEM_LIMIT = 56 * 1024 * 1024
DECODE_PAGES = 16

O_QB, O_KVB, O_WIN, O_GB, O_ZB, O_QC, O_FC, O_ZC = 1024, 1408, 1920, 2176, 2194, 2578, 3730, 3736
PROJ_W = 4120
P_QB, P_KVB, P_WIN, P_ZB, P_QC, P_ZC, P_MISC, P_END = 1024, 1408, 1920, 2176, 2560, 3712, 4096, 4224
SLAB_HEADS = [g * R_B + r for r in range(R_B) for g in range(G_B)]


def _cparams(n_grid):
    return pltpu.CompilerParams(dimension_semantics=("arbitrary",) * n_grid, vmem_limit_bytes=VMEM_LIMIT)


def _dot(a, b):
    return jnp.dot(a, b, preferred_element_type=F32)


def _dot_nt(a, b):
    return lax.dot_general(a, b, (((1,), (1,)), ((), ())), preferred_element_type=F32)


def _split3(x):
    x1 = x.astype(BF16)
    r = x - x1.astype(F32)
    x2 = r.astype(BF16)
    x3 = (r - x2.astype(F32)).astype(BF16)
    return x1, x2, x3


def _dot3(x, m):
    x1, x2, x3 = _split3(x)
    return _dot(x1, m) + _dot(x2, m) + _dot(x3, m)


def _dot3_l(m, x):
    x1, x2, x3 = _split3(x)
    return _dot(m, x1) + _dot(m, x2) + _dot(m, x3)


def _rel_bucket(rel):
    n = jnp.maximum(rel, 0)
    max_exact = N_BUCKETS // 2
    nf = jnp.maximum(n, 1).astype(F32)
    large = max_exact + (jnp.log(nf / max_exact) / math.log(REL_MAX_DIST / max_exact)
                         * (N_BUCKETS - max_exact)).astype(jnp.int32)
    return jnp.where(n < max_exact, n, jnp.minimum(large, N_BUCKETS - 1))


def _inproj_kernel(x_ref, w_ref, g_in, ga_q, ga_k, gb_q, gb_k, gc_q, gc_k, bf_row, cmpw_ref, bd32_ref, bd64_ref,
                   qa_o, arow_o, ka_o, va_o, qb_o, kvb_o, ksel_o, vsel_o, kcmp_o, vcmp_o, win_o, kwin_o, vwin_o,
                   qc_o, crow_o, kc_o, vc_o, zs_o, misc_o, *, tm, do_cmp, fmajor):
    x = x_ref[...]
    ms = jnp.mean(x * x, axis=-1, keepdims=True)
    h = (x * lax.rsqrt(ms + EPS) * g_in[...]).astype(BF16)

    def put_wide(ref, y, fill):
        for j in range(y.shape[1] // LANES):
            ref[:, 2 * j * LANES:(2 * j + 1) * LANES] = y[:, j * LANES:(j + 1) * LANES].astype(BF16)
            ref[:, (2 * j + 1) * LANES:(2 * j + 2) * LANES] = jnp.full((y.shape[0], LANES), fill, BF16)

    def put_values(ref, y):
        put_wide(ref, y, 1.0)

    def put_rows(ref, c0, y):
        if fmajor:
            ref[c0:c0 + y.shape[1], :] = y.T
        else:
            ref[:, c0:c0 + y.shape[1]] = y

    def proj(c0, c1):
        return _dot(h, w_ref[:, c0:c1])

    def segnorm(y, bd, seg):
        ss = _dot((y * y).astype(BF16), bd)
        return y * lax.rsqrt(ss * (1.0 / seg) + EPS)

    def silu(y):
        return y * (1.0 / (1.0 + jnp.exp(-y)))

    bd32 = bd32_ref[...]
    bd64 = bd64_ref[...]
    bd64_1 = bd64[0:128, 0:128]

    y = proj(0, 256)
    qa_o[...] = (segnorm(y, bd32, DA) * ga_q[...]).astype(BF16)
    y = segnorm(proj(256, 512), bd32, DA) * ga_k[...]
    put_rows(arow_o, 0, y)
    ka_o[...] = y.astype(BF16)
    y = proj(512, 768)
    put_rows(arow_o, 256, y)
    put_values(va_o, y)
    zs_o[:, 0:256] = silu(proj(768, 1024))

    qb_o[...] = (segnorm(proj(P_QB, P_KVB), bd64, HD) * gb_q[...]).astype(BF16)
    kcn = segnorm(proj(P_KVB, P_KVB + 128), bd64_1, HD) * gb_k[...]
    put_rows(kvb_o, 0, kcn)
    vcr = proj(P_KVB + 128, P_KVB + 256)
    put_rows(kvb_o, 128, vcr)
    y = segnorm(proj(P_KVB + 256, P_KVB + 384), bd64_1, HD) * gb_k[...]
    put_rows(kvb_o, 256, y)
    ksel_o[...] = y.astype(BF16)
    y = proj(P_KVB + 384, P_KVB + 512)
    put_rows(kvb_o, 384, y)
    put_values(vsel_o, y)
    if do_cmp:
        nbt = tm // CMP_BLOCK
        rowi = lax.broadcasted_iota(jnp.int32, (nbt, tm), 0)
        coli = lax.broadcasted_iota(jnp.int32, (nbt, tm), 1)
        blk = (coli >> 6) == rowi

        def cmp_weights(wl):
            e = jnp.where(blk, jnp.exp(wl - jnp.max(wl, axis=-1, keepdims=True)), 0.0)
            return (e / jnp.sum(e, axis=-1, keepdims=True)).astype(BF16)

        kcmp_o[...] = _dot(cmp_weights(cmpw_ref[0]), kcn.astype(BF16))
        vcmp_o[...] = _dot(cmp_weights(cmpw_ref[1]), vcr.astype(BF16))
    else:
        kcmp_o[...] = jnp.zeros(kcmp_o.shape, F32)
        vcmp_o[...] = jnp.zeros(vcmp_o.shape, F32)
    y = segnorm(proj(P_WIN, P_WIN + 128), bd64_1, HD) * gb_k[...]
    win_o[:, 0:128] = y
    kwin_o[...] = y.astype(BF16)
    y = proj(P_WIN + 128, P_WIN + 256)
    win_o[:, 128:256] = y
    put_values(vwin_o, y)
    zs_o[:, 256:640] = silu(proj(P_ZB, P_QC))

    qc_o[...] = (segnorm(proj(P_QC, P_QC + 384), bd64, HD) * gc_q[...]).astype(BF16)
    y = segnorm(proj(P_QC + 384, P_QC + 768), bd64, HD) * gc_k[...]
    put_rows(crow_o, 0, y)
    kc_o[...] = y.astype(BF16)
    y = proj(P_QC + 768, P_ZC)
    put_rows(crow_o, 384, y)
    put_values(vc_o, y)
    zs_o[:, 640:1024] = silu(proj(P_ZC, P_MISC))

    y = proj(P_MISC, P_END) + bf_row[...]
    lane = lax.broadcasted_iota(jnp.int32, y.shape, 1)
    sig = 1.0 / (1.0 + jnp.exp(-y))
    lsg = jnp.minimum(y, 0.0) - jnp.log(1.0 + jnp.exp(-jnp.abs(y)))
    misc_o[...] = jnp.where(lane < 18, sig, lsg)


def _inproj(x2, wp, *, do_cmp, seq=None):
    T = x2.shape[0]
    tm = min(512, T)
    assert T % tm == 0 and (tm == 512 or not do_cmp)
    nbt = 8
    n = T // tm
    fmajor = seq is not None
    if fmajor:
        assert seq[0] * seq[1] == T and seq[1] % tm == 0
        nsb = seq[1] // tm
    full = lambda a: pl.BlockSpec(a.shape, lambda i: (0,) * a.ndim)
    row = lambda w: pl.BlockSpec((tm, w), lambda i: (i, 0))
    consts = [wp["w_in"], wp["g_in"], wp["ga_q"], wp["ga_k"], wp["gb_q"], wp["gb_k"], wp["gc_q"], wp["gc_k"],
              wp["bf_row"], wp["cmpw_in"], wp["bd32"], wp["bd64"]]
    outs = [("qa", 256, BF16), ("arow", 512, F32), ("ka", 256, BF16), ("va", 512, BF16), ("qb", 384, BF16),
            ("kvb", 512, F32), ("ksel", 128, BF16), ("vsel", 256, BF16), ("kcmp", None, F32), ("vcmp", None, F32),
            ("win", 256, F32), ("kwin", 128, BF16), ("vwin", 256, BF16), ("qc", 384, BF16), ("crow", 768, F32),
            ("kc", 384, BF16), ("vc", 768, BF16), ("zs", 1024, F32), ("misc", 128, F32)]
    out_shape, out_specs = [], []
    for name, w, dt in outs:
        if w is None:
            out_shape.append(jax.ShapeDtypeStruct((n * nbt, 128), dt))
            out_specs.append(pl.BlockSpec((nbt, 128), lambda i: (i, 0)))
        elif fmajor and name in ("arow", "kvb", "crow"):
            out_shape.append(jax.ShapeDtypeStruct((seq[0], w, seq[1]), dt))
            out_specs.append(pl.BlockSpec((None, w, tm), lambda i: (i // nsb, 0, i % nsb)))
        else:
            out_shape.append(jax.ShapeDtypeStruct((T, w), dt))
            out_specs.append(row(w))
    res = pl.pallas_call(
        functools.partial(_inproj_kernel, tm=tm, do_cmp=do_cmp, fmajor=fmajor),
        grid=(n,),
        in_specs=[row(1024)] + [full(c) for c in consts],
        out_specs=out_specs, out_shape=out_shape,
        compiler_params=_cparams(1), name="inproj",
    )(x2, *consts)
    return {k: v for (k, _, _), v in zip(outs, res)}


def _attend(lhs_list, k_ref, v_ref, T, far_lo, near_lo, hi, near_add=None):
    n = len(lhs_list)

    def step(kt, state, near):
        ks = pl.multiple_of(kt * T, T)
        k_refs = k_ref if isinstance(k_ref, (tuple, list)) else (k_ref,)
        kk = jnp.concatenate([r[pl.ds(ks, T), :] for r in k_refs], axis=1) if len(k_refs) > 1 else k_ref[pl.ds(ks, T), :]
        vv = v_ref[pl.ds(ks, T), :]
        out = []
        for i in range(n):
            m, acc = state[i]
            s = _dot_nt(lhs_list[i], kk)
            if near:
                s = near_add[i](s, kt)
            m_new = jnp.maximum(m, jnp.max(s, axis=-1, keepdims=True))
            p = jnp.exp2((s - m_new).astype(BF16))
            alpha = jnp.exp2(m - m_new)
            acc = alpha * acc + _dot(p, vv)
            out.append((m_new, acc))
        return tuple(out)

    one = (jnp.full((T, 1), M_INIT, F32), jnp.zeros((T, v_ref.shape[-1]), F32))
    state = (one,) * n
    npair = (near_lo - far_lo) // 2
    state = lax.fori_loop(
        0, npair, lambda j, c: step(far_lo + 2 * j + 1, step(far_lo + 2 * j, c, False), False), state)
    state = lax.fori_loop(far_lo + 2 * npair, near_lo, lambda kt, c: step(kt, c, False), state)
    state = lax.fori_loop(near_lo, hi, lambda kt, c: step(kt, c, True), state)
    return [acc[:, :LANES] / acc[:, LANES:] for _, acc in state]


def _lane_band(x, lo, width):
    lane = lax.broadcasted_iota(jnp.int32, x.shape, 1)
    return jnp.where((lane >= lo) & (lane < lo + width), x, jnp.zeros_like(x))


def _halves(a, b):
    lane = lax.broadcasted_iota(jnp.int32, a.shape, 1)
    return jnp.where(lane < HD, a, b)


def _diff_lambda(lam_ref, lam_init):
    lf = lam_ref[...]
    a = jnp.sum(lf[0:1] * lf[1:2], axis=-1, keepdims=True)
    b = jnp.sum(lf[2:3] * lf[3:4], axis=-1, keepdims=True)
    return jnp.exp(a) - jnp.exp(b) + lam_init


def _flash_a_kernel(q_ref, k_ref, v_ref, bias_ref, lam_ref, onorm_ref, o_ref, *, T, noff, lam_init):
    qi = pl.program_id(2)
    near_lo = jnp.maximum(qi - (noff - 1), 0)
    q = q_ref[...]
    lam = _diff_lambda(lam_ref, lam_init)
    lhs = [_lane_band(q, hl * HD + mp * DA, DA) for hl in range(2) for mp in range(2)]
    add = [functools.partial(lambda s, kt, hl: s + bias_ref[hl, qi - kt], hl=i // 2) for i in range(4)]
    o = _attend(lhs, k_ref, v_ref, T, 0, near_lo, qi + 1, near_add=add)
    o = _halves(o[0] - lam * o[1], o[2] - lam * o[3])
    x2 = o * o
    lane = lax.broadcasted_iota(jnp.int32, o.shape, 1)
    s0 = jnp.sum(jnp.where(lane < HD, x2, 0.0), axis=-1, keepdims=True)
    s1 = jnp.sum(jnp.where(lane < HD, 0.0, x2), axis=-1, keepdims=True)
    ss = jnp.where(lane < HD, s0, s1)
    o_ref[...] = o * lax.rsqrt(ss * (1.0 / HD) + EPS) * onorm_ref[...] * (1.0 - lam_init)


def _flash_c_kernel(q_ref, qx_ref, k_ref, kx_ref, v_ref, mask_ref, o_ref, *, T):
    qi = pl.program_id(2)
    q = q_ref[...]
    qx = qx_ref[...]
    lane = lax.broadcasted_iota(jnp.int32, qx.shape, 1)
    lhs = []
    for hl in range(2):
        own = ((lane >= 3 * hl) & (lane < 3 * hl + 3)) | ((lane >= 6 + 3 * hl) & (lane < 9 + 3 * hl))
        lhs.append(jnp.concatenate([_lane_band(q, hl * HD, HD), jnp.where(own, qx, jnp.zeros_like(qx))], axis=1))
    causal = [lambda s, kt: s + mask_ref[...]] * 2
    o = _attend(lhs, (k_ref, kx_ref), v_ref, T, 0, qi, qi + 1, near_add=causal)
    o_ref[...] = _halves(o[0], o[1])


def _flash_b_kernel(*refs, T, noff, selected):
    if selected:
        q_ref, sel_ref, ksel_ref, hot_ref, v_ref, bias_ref, o_ref = refs
        k_ref = (ksel_ref, hot_ref)
    else:
        q_ref, k_ref, v_ref, bias_ref, o_ref = refs
    qi = pl.program_id(2)
    near_lo = jnp.maximum(qi - (noff - 1), 0)
    far_lo = 0 if selected else near_lo
    q = q_ref[...]
    lhs = [_lane_band(q, g * HD, HD) for g in range(G_B)]
    if selected:
        nbp = sel_ref.shape[-1] // G_B
        lhs = [jnp.concatenate([lhs[g], sel_ref[:, g * nbp:(g + 1) * nbp]], axis=1) for g in range(G_B)]
    add = [functools.partial(lambda s, kt, g: s + bias_ref[g, qi - kt], g=g) for g in range(G_B)]
    o = _attend(lhs, k_ref, v_ref, T, far_lo, near_lo, qi + 1, near_add=add)
    o_ref[...] = _halves(o[0], o[1])


def _flash_call(kernel, name, B, S, T, n_grp, ins, in_specs, out_w):
    return pl.pallas_call(
        kernel, grid=(B, n_grp, S // T), in_specs=in_specs,
        out_specs=pl.BlockSpec((None, T, LANES), lambda b, g, i: (b, i, g)),
        out_shape=jax.ShapeDtypeStruct((B, S, out_w), F32),
        compiler_params=_cparams(3), name=name,
    )(*ins)


def _qspec(T):
    return pl.BlockSpec((None, T, LANES), lambda b, g, i: (b, i, g))


def _kvspec(S, w, per_group):
    if per_group:
        return pl.BlockSpec((None, S, w), lambda b, g, i: (b, 0, g))
    return pl.BlockSpec((None, S, w), lambda b, g, i: (b, 0, 0))


def _biasspec(noff, T):
    return pl.BlockSpec((2, noff, T, T), lambda b, g, i: (g, 0, 0, 0))


def _small(a):
    return pl.BlockSpec(a.shape, lambda b, g, i: (0,) * a.ndim)


def _masked_softmax2(s, mask, axis):
    s = jnp.where(mask, s, NEG)
    m = jnp.max(s, axis=axis, keepdims=True)
    p = jnp.where(mask, jnp.exp2(s - m), 0.0)
    return p / jnp.maximum(jnp.sum(p, axis=axis, keepdims=True), 1e-30)


def _topk_select(score, idx, n_idx, axis):
    def it(_, c):
        sc, sel = c
        mx = jnp.max(sc, axis=axis, keepdims=True)
        first = jnp.min(jnp.where(sc == mx, idx, n_idx), axis=axis, keepdims=True)
        oh = idx == first
        sel = jnp.where(oh, jnp.where(mx >= 0.0, 1.0, sel), sel)
        sc = jnp.where(oh, -2.0, sc)
        return sc, sel

    _, sel = lax.fori_loop(0, TOPK, it, (score, jnp.zeros(score.shape, F32)))
    return sel


def _cmp_kernel(q_ref, kc_ref, vc_ref, oc_ref, selt_ref, *, T, NBP):
    qi = pl.program_id(1)
    q = q_ref[...]
    kc = kc_ref[...]
    vc = vc_ref[...]
    qpos_r = qi * T + lax.broadcasted_iota(jnp.int32, (T, NBP), 0)
    blk_c = lax.broadcasted_iota(jnp.int32, (T, NBP), 1)
    cm = ((blk_c + 1) * CMP_BLOCK - 1) <= qpos_r
    qpos_c = qi * T + lax.broadcasted_iota(jnp.int32, (NBP, T), 1)
    blk_r = lax.broadcasted_iota(jnp.int32, (NBP, T), 0)
    cmt = ((blk_r + 1) * CMP_BLOCK - 1) <= qpos_c
    imp = [jnp.zeros((NBP, T), F32) for _ in range(G_B)]
    for r in range(R_B):
        slab = q[:, r * LANES:(r + 1) * LANES]
        outs = []
        for g in range(G_B):
            qm = _lane_band(slab, g * HD, HD)
            p = _masked_softmax2(_dot_nt(qm, kc), cm, -1)
            outs.append(_dot(p.astype(BF16), vc))
            imp[g] = imp[g] + _masked_softmax2(_dot_nt(kc, qm), cmt, 0)
        oc_ref[:, r * LANES:(r + 1) * LANES] = _halves(outs[0], outs[1])
    qblk = qpos_c >> 6
    forced = (blk_r == qblk) | (blk_r == qblk - 1) | (blk_r == 0)
    valid = blk_r * CMP_BLOCK <= qpos_c
    blk_q = lax.broadcasted_iota(jnp.int32, (NBP, LANES), 0)
    for g in range(G_B):
        score = jnp.where(valid, imp[g] + jnp.where(forced, FORCE_BONUS, 0.0), -1.0)
        for c in range(T // LANES):
            cs = slice(c * LANES, (c + 1) * LANES)
            selt_ref[g, :, cs] = _topk_select(score[:, cs], blk_q, NBP, 0) - 1.0


def _cmp_call(qb3, kcmp3, vcmp3, T):
    B, S, _ = qb3.shape
    NBP = kcmp3.shape[1]
    return pl.pallas_call(
        functools.partial(_cmp_kernel, T=T, NBP=NBP), grid=(B, S // T),
        in_specs=[pl.BlockSpec((None, T, 384), lambda b, i: (b, i, 0)),
                  pl.BlockSpec((None, NBP, LANES), lambda b, i: (b, 0, 0)),
                  pl.BlockSpec((None, NBP, LANES), lambda b, i: (b, 0, 0))],
        out_specs=[pl.BlockSpec((None, T, 384), lambda b, i: (b, i, 0)),
                   pl.BlockSpec((None, G_B, NBP, T), lambda b, i: (b, 0, 0, i))],
        out_shape=[jax.ShapeDtypeStruct((B, S, 384), F32), jax.ShapeDtypeStruct((B, G_B, NBP, S), F32)],
        compiler_params=_cparams(2), name="nsa_cmp_topk",
    )(qb3, kcmp3, vcmp3)


def _cumsum_kernel(x_ref, mw_ref, mr_ref, *o_refs, nh, sign, split):
    mw = mw_ref[...]
    mr = mr_ref[...]
    ones = jnp.ones((LANES, LANES), BF16)
    for h in range(nh):
        x = x_ref[h]
        tot = _dot3(x, ones)
        c = (_dot3(x, mw) + _dot3_l(mr, tot)) * (sign * LOG2E)
        if split:
            for o_ref, part in zip(o_refs, _split3(c)):
                o_ref[h] = part
        else:
            o_refs[0][h] = c


def _cumsum_call(x4, mw, mr, sign, split=False):
    B, nh, nr, _ = x4.shape
    spec = pl.BlockSpec((None, nh, nr, LANES), lambda b: (b, 0, 0, 0))
    return pl.pallas_call(
        functools.partial(_cumsum_kernel, nh=nh, sign=sign, split=split), grid=(B,),
        in_specs=[spec, pl.BlockSpec(mw.shape, lambda b: (0, 0)), pl.BlockSpec(mr.shape, lambda b: (0, 0))],
        out_specs=[spec] * 3 if split else spec,
        out_shape=[jax.ShapeDtypeStruct(x4.shape, BF16)] * 3 if split else jax.ShapeDtypeStruct(x4.shape, F32),
        compiler_params=_cparams(1), name="logf_cumsum",
    )(x4, mw, mr)


def _merge_kernel(x_ref, oa_ref, obc_ref, obs_ref, obw_ref, oc_ref, misc_ref, zs_ref, w_ref, eg_ref, y_ref):
    g = misc_ref[...]
    g1 = g.astype(BF16)
    g2 = (g - g1.astype(F32)).astype(BF16)

    def gate(c):
        return _dot(g1, eg_ref[c]) + _dot(g2, eg_ref[c])

    ob = gate(0) * obc_ref[...] + gate(1) * obs_ref[...] + gate(2) * obw_ref[...]
    ma = (oa_ref[...] * zs_ref[:, 0:256]).astype(BF16)
    mb = (ob * zs_ref[:, 256:640]).astype(BF16)
    mc = (oc_ref[...] * zs_ref[:, 640:1024]).astype(BF16)
    y_ref[...] = x_ref[...] + _dot(ma, w_ref[0:256, :]) + _dot(mb, w_ref[256:640, :]) + _dot(mc, w_ref[640:1024, :])


def _merge_call(x2, oa, obc, obs, obw, oc, misc, zs, w_out, eg):
    T = x2.shape[0]
    tm = min(512, T)
    row = lambda w: pl.BlockSpec((tm, w), lambda i: (i, 0))
    return pl.pallas_call(
        _merge_kernel, grid=(T // tm,),
        in_specs=[row(1024), row(256), row(384), row(384), row(384), row(384), row(128), row(1024),
                  pl.BlockSpec(w_out.shape, lambda i: (0, 0)), pl.BlockSpec(eg.shape, lambda i: (0, 0, 0))],
        out_specs=row(1024), out_shape=jax.ShapeDtypeStruct((T, 1024), F32),
        compiler_params=_cparams(1), name="merge_outproj",
    )(x2, oa, obc, obs, obw, oc, misc, zs, w_out, eg)


def _gather_rows_kernel(pt_ref, *refs, n):
    out = refs[n]
    for i in range(n):
        out[i] = refs[i][...]


def _gather_logf(cache4, layer, page_table):
    B, NP = page_table.shape
    n = min(16, NP)
    assert NP % n == 0
    w = cache4.shape[-1]
    in_specs = [pl.BlockSpec((None, None, 1, w), functools.partial(
        lambda b, j, pt, i: (layer, pt[b, j * n + i], 0, 0), i=i)) for i in range(n)]
    return pl.pallas_call(
        functools.partial(_gather_rows_kernel, n=n),
        grid_spec=pltpu.PrefetchScalarGridSpec(
            num_scalar_prefetch=1, grid=(B, NP // n), in_specs=in_specs,
            out_specs=pl.BlockSpec((None, n, 1, w), lambda b, j, pt: (b, j, 0, 0))),
        out_shape=jax.ShapeDtypeStruct((B, NP, 1, w), F32),
        compiler_params=_cparams(2), name="gather_logf",
    )(page_table, *([cache4] * n))


def _scmp_kernel(pt_ref, *refs, PG, NP, NBP, P):
    pages = refs[:PG]
    cmpw_ref, new_ref, qb_ref, oc_ref, sel_ref, kcs, vcs = refs[PG:]
    jc = pl.program_id(1)
    ns = pl.num_programs(1)
    nrow = 2 * PG

    @pl.when(jc == 0)
    def _():
        kcs[...] = jnp.zeros(kcs.shape, F32)
        vcs[...] = jnp.zeros(vcs.shape, F32)

    rowi = lax.broadcasted_iota(jnp.int32, (nrow, PAGE), 0)
    coli = lax.broadcasted_iota(jnp.int32, (nrow, PAGE), 1)

    def half_softmax(wl):
        e = jnp.exp(wl - jnp.max(wl, axis=-1, keepdims=True))
        return e / (0.5 * jnp.sum(e, axis=-1, keepdims=True))

    wk = half_softmax(cmpw_ref[0])
    wv = half_softmax(cmpw_ref[1])
    kacc = jnp.zeros((nrow, LANES), F32)
    vacc = jnp.zeros((nrow, LANES), F32)
    for i in range(PG):
        sel = (coli >> 6) + 2 * i == rowi
        pg = pages[i][...]
        kacc = kacc + _dot_nt(jnp.where(sel, wk, 0.0).astype(BF16), pg[0:128, :].astype(BF16))
        vacc = vacc + _dot_nt(jnp.where(sel, wv, 0.0).astype(BF16), pg[128:256, :].astype(BF16))
    r0 = pl.multiple_of(jc * nrow, nrow)
    kcs[pl.ds(r0, nrow), :] = kacc
    vcs[pl.ds(r0, nrow), :] = vacc

    @pl.when(jc == ns - 1)
    def _():
        first = (rowi == 0) & (coli < CMP_BLOCK)
        new = new_ref[...]
        kcs[2 * NP:2 * NP + nrow, :] = _dot(jnp.where(first, wk, 0.0).astype(BF16), new[:, 0:128].astype(BF16))
        vcs[2 * NP:2 * NP + nrow, :] = _dot(jnp.where(first, wv, 0.0).astype(BF16), new[:, 128:256].astype(BF16))
        kc = kcs[...].astype(BF16)
        vc = vcs[...].astype(BF16)
        R = qb_ref.shape[0]
        s = _dot_nt(qb_ref[...], kc)
        t_r = lax.broadcasted_iota(jnp.int32, (R, NBP), 0) & 7
        blk = lax.broadcasted_iota(jnp.int32, (R, NBP), 1)
        cm = ((blk + 1) * CMP_BLOCK - 1) <= (P + t_r)
        p = _masked_softmax2(s, cm, -1)
        oc_ref[...] = _dot(p.astype(BF16), vc)
        qpos = P + lax.broadcasted_iota(jnp.int32, (8, NBP), 0)
        blk8 = lax.broadcasted_iota(jnp.int32, (8, NBP), 1)
        qblk = qpos >> 6
        forced = (blk8 == qblk) | (blk8 == qblk - 1) | (blk8 == 0)
        valid = blk8 * CMP_BLOCK <= qpos
        for g in range(G_B):
            imp = sum(p[(r * G_B + g) * 8:(r * G_B + g) * 8 + 8] for r in range(R_B))
            score = jnp.where(valid, imp + jnp.where(forced, FORCE_BONUS, 0.0), -1.0)
            sel_ref[g] = _topk_select(score, blk8, NBP, -1) - 1.0


def _scmp_call(cache_b, layer, page_table, cmpw_s, new_pad, qb_rows, P):
    B, NP = page_table.shape
    PG = min(DECODE_PAGES, NP)
    assert NP % PG == 0
    NB = 2 * NP + 1
    NBP = -(-(2 * NP + 2 * PG) // LANES) * LANES
    in_specs = [pl.BlockSpec((None, None, 256, PAGE), functools.partial(
        lambda b, j, pt, i: (layer, pt[b, j * PG + i], 0, 0), i=i)) for i in range(PG)]
    in_specs += [pl.BlockSpec(cmpw_s.shape, lambda b, j, pt: (0, 0, 0)),
                 pl.BlockSpec((None, PAGE, 256), lambda b, j, pt: (b, 0, 0)),
                 pl.BlockSpec((None,) + qb_rows.shape[1:], lambda b, j, pt: (b, 0, 0))]
    R = qb_rows.shape[1]
    del NB
    return pl.pallas_call(
        functools.partial(_scmp_kernel, PG=PG, NP=NP, NBP=NBP, P=P),
        grid_spec=pltpu.PrefetchScalarGridSpec(
            num_scalar_prefetch=1, grid=(B, NP // PG), in_specs=in_specs,
            out_specs=[pl.BlockSpec((None, R, LANES), lambda b, j, pt: (b, 0, 0)),
                       pl.BlockSpec((None, G_B, 8, NBP), lambda b, j, pt: (b, 0, 0, 0))],
            scratch_shapes=[pltpu.VMEM((NBP, LANES), F32), pltpu.VMEM((NBP, LANES), F32)]),
        out_shape=[jax.ShapeDtypeStruct((B, R, LANES), F32), jax.ShapeDtypeStruct((B, G_B, 8, NBP), F32)],
        compiler_params=_cparams(2), name="sample_cmp_topk",
    )(page_table, *([cache_b] * PG), cmpw_s, new_pad, qb_rows)


def _decode_kernel(pt_ref, *refs, PG, kc0, KD, vc0, VD, mode, n_near, lam_init):
    pages = refs[:PG]
    rest = list(refs[PG:])
    qb_ref, knew_ref, vnew_ref, addnew_ref = rest[:4]
    rest = rest[4:]
    bias_ref = ck_ref = cnew_ref = cq_ref = amask_ref = lam_ref = onorm_ref = None
    if mode == "a":
        bias_ref, lam_ref, onorm_ref = rest[:3]
        rest = rest[3:]
    elif mode == "c":
        ck_ref, cnew_ref, cq_ref = rest[:3]
        rest = rest[3:]
    elif mode == "bs":
        bias_ref, amask_ref, hot_ref = rest[:3]
        rest = rest[3:]
    else:
        bias_ref = rest[0]
        rest = rest[1:]
    o_ref, m_s, l_s, acc_s = rest
    jc = pl.program_id(1)
    ns = pl.num_programs(1)
    R = qb_ref.shape[0]

    @pl.when(jc == 0)
    def _():
        m_s[...] = jnp.full(m_s.shape, M_INIT, F32)
        l_s[...] = jnp.zeros(l_s.shape, F32)
        acc_s[...] = jnp.zeros(acc_s.shape, F32)

    def rows_of_heads(c):
        return jnp.concatenate([jnp.broadcast_to(c[h:h + 1, :], (8, c.shape[1])) for h in range(R // 8)], axis=0)

    def update(s, v, v_feature_major):
        m = m_s[...]
        m_new = jnp.maximum(m, jnp.max(s, axis=-1, keepdims=True))
        p = jnp.exp2(s - m_new)
        alpha = jnp.exp2(m - m_new)
        l_s[...] = alpha * l_s[...] + jnp.sum(p, axis=-1, keepdims=True)
        pb = p.astype(BF16)
        acc_s[...] = alpha * acc_s[...] + (_dot_nt(pb, v) if v_feature_major else _dot(pb, v))
        m_s[...] = m_new

    q = qb_ref[...]
    kcat = jnp.concatenate([pg[kc0:kc0 + KD, :].astype(BF16) for pg in pages], axis=1)
    vcat = jnp.concatenate([pg[vc0:vc0 + VD, :].astype(BF16) for pg in pages], axis=1)
    s = _dot(q, kcat)
    if mode == "c":
        s = s + cq_ref[...] - rows_of_heads(ck_ref[...])
    if mode == "bs":
        s = s + _dot(amask_ref[...], hot_ref[...])
    if bias_ref is not None:
        e = jc - (ns - n_near)
        s = s + jnp.where(e >= 0, 1.0, 0.0) * bias_ref[jnp.maximum(e, 0)]
    update(s, vcat, True)

    @pl.when(jc == ns - 1)
    def _():
        s2 = _dot_nt(q, knew_ref[...]) + addnew_ref[...]
        if mode == "c":
            s2 = s2 + cq_ref[...] - rows_of_heads(cnew_ref[...])
        update(s2, vnew_ref[...], False)
        o = acc_s[...] / l_s[...]
        if mode == "a":
            lam = _diff_lambda(lam_ref, lam_init)
            half = R // 2
            pd = o[0:half] - lam * o[half:R]
            rowi = lax.broadcasted_iota(jnp.int32, pd.shape, 0)
            lanei = lax.broadcasted_iota(jnp.int32, pd.shape, 1)
            x = jnp.where((rowi >> 3) == (lanei >> 6), pd, 0.0)
            ss = jnp.sum(x * x, axis=-1, keepdims=True)
            y = x * lax.rsqrt(ss * (1.0 / HD) + EPS) * onorm_ref[...] * (1.0 - lam_init)
            o_ref[...] = sum(y[h * 8:(h + 1) * 8] for h in range(H_A))
        elif mode == "c":
            rowi = lax.broadcasted_iota(jnp.int32, o.shape, 0)
            lanei = lax.broadcasted_iota(jnp.int32, o.shape, 1)
            x = jnp.where((rowi >> 3) == (lanei >> 6), o, 0.0)
            o_ref[...] = sum(x[h * 8:(h + 1) * 8] for h in range(H_C))
        else:
            rowi = lax.broadcasted_iota(jnp.int32, o.shape, 0)
            lanei = lax.broadcasted_iota(jnp.int32, o.shape, 1)
            x = jnp.where(((rowi >> 3) & 1) == (lanei >> 6), o, 0.0)
            o_ref[...] = jnp.concatenate(
                [x[(2 * r) * 8:(2 * r) * 8 + 8] + x[(2 * r + 1) * 8:(2 * r + 1) * 8 + 8] for r in range(R_B)], axis=1)


def _decode_call(name, cache, page_maps, page_table, rowblk, kc0, KD, vc0, VD, mode, qb_rows, knew, vnew, addnew,
                 extras, extra_specs, n_near, out_w, lam_init=0.0):
    B, NP = page_table.shape
    PG = len(page_maps)
    assert NP % PG == 0
    R = qb_rows.shape[1]
    bh, bi = rowblk

    def page_spec(f):
        def index(b, j, pt):
            d0, d1, lane_blk = f(b, j, pt)
            return (d0, d1, bi, lane_blk)
        return pl.BlockSpec((None, None, bh, PAGE), index)

    in_specs = [page_spec(f) for f in page_maps]
    bspec = lambda a: pl.BlockSpec((None,) + a.shape[1:], lambda b, j, pt: (b,) + (0,) * (a.ndim - 1))
    in_specs += [bspec(qb_rows), bspec(knew), bspec(vnew), bspec(addnew)] + extra_specs
    return pl.pallas_call(
        functools.partial(_decode_kernel, PG=PG, kc0=kc0, KD=KD, vc0=vc0, VD=VD, mode=mode, n_near=n_near,
                          lam_init=lam_init),
        grid_spec=pltpu.PrefetchScalarGridSpec(
            num_scalar_prefetch=1, grid=(B, NP // PG), in_specs=in_specs,
            out_specs=pl.BlockSpec((None, 8, out_w), lambda b, j, pt: (b, 0, 0)),
            scratch_shapes=[pltpu.VMEM((R, 1), F32), pltpu.VMEM((R, 1), F32), pltpu.VMEM((R, VD), F32)]),
        out_shape=jax.ShapeDtypeStruct((B, 8, out_w), F32),
        compiler_params=_cparams(2), name=name,
    )(page_table, *([cache] * PG), qb_rows, knew, vnew, addnew, *extras)


def _prep_layer(l, norm_g, w_in, w_out, qk_a, qk_b, qk_c, onorm_a, lam_a, cmp_w, b_f):
    wl = w_in[l]
    slab_cols = lambda base: [wl[:, base + h * HD:base + (h + 1) * HD] for h in SLAB_HEADS]
    w = jnp.concatenate(
        [wl[:, 0:O_QB]] + slab_cols(O_QB) + [wl[:, O_KVB:O_GB]] + slab_cols(O_ZB)
        + [wl[:, O_QC:O_FC], wl[:, O_ZC:PROJ_W], wl[:, O_GB:O_GB + 18], wl[:, O_FC:O_FC + H_C],
           jnp.zeros((wl.shape[0], P_END - PROJ_W), wl.dtype)], axis=1).astype(BF16)
    wol = w_out[l]
    wo = jnp.concatenate([wol[0:256]] + [wol[256 + h * HD:256 + (h + 1) * HD] for h in SLAB_HEADS]
                         + [wol[640:1024]], axis=0).astype(BF16)
    tile = lambda v, n: jnp.tile(v.astype(F32), n)[None, :]
    bf_row = jnp.zeros((1, LANES), F32).at[0, 18:18 + H_C].set(b_f[l].astype(F32))
    bd = lambda n, seg: jnp.asarray(np.kron(np.eye(n // seg), np.ones((seg, seg))), BF16)
    eg = np.zeros((3, LANES, 384), np.float32)
    for g in range(G_B):
        for r in range(R_B):
            for c in range(3):
                eg[c, g * 9 + r * 3 + c, (r * G_B + g) * HD:(r * G_B + g + 1) * HD] = 1.0
    return dict(
        w_in=w, w_out=wo, g_in=norm_g[l].astype(F32)[None, :],
        ga_q=tile(qk_a[l, 0], 8) * (DA ** -0.5 * LOG2E), ga_k=tile(qk_a[l, 1], 8),
        gb_q=tile(qk_b[l, 0], 6) * (HD ** -0.5 * LOG2E), gb_k=tile(qk_b[l, 1], 2),
        gc_q=tile(qk_c[l, 0], 6) * (HD ** -0.5 * LOG2E), gc_k=tile(qk_c[l, 1], 6),
        bf_row=bf_row, bd32=bd(256, DA), bd64=bd(384, HD),
        cmpw_in=jnp.tile(cmp_w[l].astype(F32)[:, None, :], (1, 8, 8)),
        cmpw_s=jnp.tile(cmp_w[l].astype(F32)[:, None, :], (1, 1, 2)),
        onorm2=tile(onorm_a[l], 2), onorm4=tile(onorm_a[l], 4), lam=lam_a[l].astype(F32),
        eg=jnp.asarray(eg, BF16),
    )


def _toeplitz(tab_h, noff, T, mask):
    H, nt = tab_h.shape
    seg = 2 * T - 1
    padded = jnp.pad(tab_h, ((0, 0), (T - 1, max(0, noff * T - nt))))
    tiles = []
    for d in range(noff):
        u = padded[:, d * T:d * T + seg][:, ::-1]
        flat = jnp.broadcast_to(u[:, None, :], (H, T, seg)).reshape(H, T * seg)
        tiles.append(flat[:, T - 1:T - 1 + T * (seg - 1)].reshape(H, T, seg - 1)[:, :, :T])
    return jnp.where(jnp.asarray(mask)[None], jnp.stack(tiles, axis=1), NEG)


def _shifted_rows(tab_h, width, n_t):
    return jnp.stack([tab_h[:, t + 1:t + 1 + width][:, ::-1] for t in range(n_t)], axis=1)


def _prompt_layer(x3, wp, tabs, lam_init, T):
    B, S, D = x3.shape
    pj = _inproj(x3.reshape(B * S, D), wp, do_cmp=True, seq=(B, S))
    r3 = lambda a: a.reshape(B, S, a.shape[-1])
    rows_out = lambda a, *dims: jnp.moveaxis(a.reshape((B,) + dims + (S,)), -1, 1)
    nq = S // T
    o_a = _flash_call(
        functools.partial(_flash_a_kernel, T=T, noff=tabs["noff"], lam_init=lam_init), "flash_a", B, S, T, 2,
        [r3(pj["qa"]), r3(pj["ka"]), r3(pj["va"]), tabs["bias_a"], wp["lam"], wp["onorm2"]],
        [_qspec(T), _kvspec(S, LANES, True), _kvspec(S, 2 * LANES, True), _biasspec(tabs["noff"], T),
         _small(wp["lam"]), _small(wp["onorm2"])], 256)
    logf = pj["misc"][:, 18:18 + H_C].reshape(B, S, H_C)
    nr = S // LANES
    lf4 = jnp.swapaxes(logf, 1, 2).reshape(B, H_C, nr, LANES)
    mw = jnp.asarray(np.triu(np.ones((LANES, LANES))), BF16)
    mr = jnp.asarray(np.tril(np.ones((nr, nr)), -1), BF16)
    cparts = _cumsum_call(lf4, mw, mr, 1.0, split=True)
    c_split = jnp.stack([p.reshape(B, 3, 2, S) for p in cparts], axis=-1)
    c_split = jnp.transpose(c_split, (0, 3, 1, 2, 4)).reshape(B, S, 3, 6)
    ones6 = jnp.ones((B, S, 3, 6), BF16)
    zpad = jnp.zeros((B, S, 3, LANES - 12), BF16)
    q_extra = jnp.concatenate([-ones6, c_split, zpad], axis=-1).reshape(B, S, 3 * LANES)
    k_extra = jnp.concatenate([c_split, ones6, zpad], axis=-1).reshape(B, S, 3 * LANES)
    o_c = _flash_call(
        functools.partial(_flash_c_kernel, T=T), "flash_c", B, S, T, 3,
        [r3(pj["qc"]), q_extra, r3(pj["kc"]), k_extra, r3(pj["vc"]), tabs["causal"]],
        [_qspec(T), _qspec(T), _kvspec(S, LANES, True), _kvspec(S, LANES, True), _kvspec(S, 2 * LANES, True),
         _small(tabs["causal"])], 384)
    NB = S // CMP_BLOCK
    NBP = -(-NB // LANES) * LANES
    padb = lambda a: jnp.pad(a.reshape(B, NB, LANES), ((0, 0), (0, NBP - NB), (0, 0))).astype(BF16)
    o_bc, selt = _cmp_call(r3(pj["qb"]), padb(pj["kcmp"]), padb(pj["vcmp"]), T)
    sel = jnp.swapaxes(selt, 2, 3)
    sel = jnp.swapaxes(sel, 1, 2).reshape(B, S, G_B * NBP).astype(BF16)
    o_bs = _flash_call(
        functools.partial(_flash_b_kernel, T=T, noff=tabs["noff"], selected=True), "flash_bsel", B, S, T, 3,
        [r3(pj["qb"]), sel, r3(pj["ksel"]), tabs["blockhot"], r3(pj["vsel"]), tabs["bias_b"]],
        [_qspec(T), pl.BlockSpec((None, T, G_B * NBP), lambda b, g, i: (b, i, 0)),
         _kvspec(S, LANES, False), _small(tabs["blockhot"]), _kvspec(S, 2 * LANES, False),
         _biasspec(tabs["noff"], T)], 384)
    o_bw = _flash_call(
        functools.partial(_flash_b_kernel, T=T, noff=2, selected=False), "flash_bwin", B, S, T, 3,
        [r3(pj["qb"]), r3(pj["kwin"]), r3(pj["vwin"]), tabs["bias_w"]],
        [_qspec(T), _kvspec(S, LANES, False), _kvspec(S, 2 * LANES, False), _biasspec(2, T)], 384)
    f2 = lambda a: a.reshape(B * S, a.shape[-1])
    y = _merge_call(x3.reshape(B * S, D), f2(o_a), f2(o_bc), f2(o_bs), f2(o_bw), f2(o_c), pj["misc"], pj["zs"],
                    wp["w_out"], wp["eg"])
    del nq
    w = min(WINDOW, S)
    return (y.reshape(B, S, D), rows_out(pj["arow"], 2, H_A, HD), rows_out(pj["kvb"], 4, G_B, HD),
            pj["win"].reshape(B, S, 2 * G_B * HD)[:, S - w:].reshape(B, w, 2, G_B, HD),
            rows_out(pj["crow"], 2, H_C, HD), logf)


def _prompt_tables(rel_bias, T, S):
    noff = -(-(REL_MAX_DIST - 1) // T) + 1
    noff = min(noff, S // T)
    nt = noff * T + 1
    bucket = _rel_bucket(jnp.arange(nt, dtype=jnp.int32))
    tab = jnp.take(rel_bias.astype(F32), bucket, axis=0).T
    far = rel_bias.astype(F32)[N_BUCKETS - 1][:, None]
    i = np.arange(T)[:, None]
    j = np.arange(T)[None, :]
    rel = np.stack([d * T + i - j for d in range(noff)])
    tab_s = (tab - far) * LOG2E
    heads_b = np.asarray([H_A + h for h in SLAB_HEADS])
    tab_b = jnp.stack([tab_s[h] for h in heads_b])
    tab_w = jnp.stack([tab[h] for h in heads_b]) * LOG2E
    bias_a = _toeplitz(tab_s[:H_A], noff, T, rel >= 0)
    bias_b = _toeplitz(tab_b, noff, T, rel >= 0)
    relw = np.stack([d * T + i - j for d in range(2)])
    bias_w = _toeplitz(tab_w, 2, T, (relw >= 0) & (relw < WINDOW))
    causal = jnp.asarray(np.where(i >= j, 0.0, NEG), F32)
    NB = S // CMP_BLOCK
    NBP = -(-NB // LANES) * LANES
    hot = (np.arange(S)[:, None] // CMP_BLOCK == np.arange(NBP)[None, :]) * 1e30
    return dict(noff=noff, bias_a=bias_a, bias_b=bias_b, bias_w=bias_w, causal=causal,
                blockhot=jnp.asarray(hot, BF16))


def _sample_tables(rel_bias, P, PG):
    NP = P // PAGE
    n_near_pages = min(NP, -(-REL_MAX_DIST // (PAGE * PG)) * PG)
    assert n_near_pages % PG == 0 or NP == n_near_pages
    nt = n_near_pages * PAGE + 16
    bucket = _rel_bucket(jnp.arange(nt, dtype=jnp.int32))
    rb = rel_bias.astype(F32)
    tab = jnp.take(rb, bucket, axis=0).T
    far = rb[N_BUCKETS - 1][:, None]
    tab_s = (tab - far) * LOG2E
    tab_a = tab_s[:H_A]
    tab_b = jnp.stack([tab_s[H_A + h] for h in SLAB_HEADS])
    wn = n_near_pages * PAGE
    flat = lambda a: a.reshape(a.shape[0] * a.shape[1], a.shape[2])
    near_a = flat(_shifted_rows(tab_a, wn, 8))
    new_a = flat(_new_rows(tab_a))
    return dict(n_near_pages=n_near_pages,
                near_a=jnp.concatenate([near_a, near_a], axis=0), new_a=jnp.concatenate([new_a, new_a], axis=0),
                near_b=flat(_shifted_rows(tab_b, wn, 8)), new_b=flat(_new_rows(tab_b)),
                tab_w=jnp.stack([tab[H_A + h] for h in SLAB_HEADS]) * LOG2E)


def _new_rows(tab_h):
    rows = [jnp.pad(tab_h[:, :t + 1][:, ::-1], ((0, 0), (0, PAGE - t - 1)), constant_values=NEG) for t in range(8)]
    return jnp.stack(rows, axis=1)


def _rows_from_lanes(q, bands, width):
    lane = np.arange(q.shape[-1])
    m = np.stack([(lane >= lo) & (lane < lo + width) for lo in bands])
    return jnp.where(jnp.asarray(m)[None, :, None, :], q[:, None], jnp.zeros((), q.dtype)).reshape(
        q.shape[0], len(bands) * 8, q.shape[-1])


def _pad_rows(a, n):
    return jnp.pad(a, ((0, 0), (0, n - a.shape[1]), (0, 0)))


def _sample_layer(l, x3, caches, page_table, wp, stabs, wtabs, lam_init):
    cache_a, cache_b, cache_c, cache_lf4, state_win = caches
    B, S8, D = x3.shape
    NP = page_table.shape[1]
    P = NP * PAGE
    pj = _inproj(x3.reshape(B * S8, D), wp, do_cmp=False)
    r3 = lambda a: a.reshape(B, S8, a.shape[-1])
    PG = min(DECODE_PAGES, NP)
    n_near = stabs["n_near_pages"] // PG if stabs["n_near_pages"] >= PG else 1
    split_steps = lambda a: jnp.swapaxes(a.reshape(a.shape[0], -1, PG * PAGE), 0, 1)
    bcast = lambda a: jnp.broadcast_to(a[None], (B,) + a.shape)
    pool_maps = [functools.partial(lambda b, j, pt, i: (l, pt[b, j * PG + i], 0), i=i) for i in range(PG)]

    qa_rows = _rows_from_lanes(r3(pj["qa"]), [h * HD + mp * DA for mp in range(2) for h in range(H_A)], DA)
    arow = r3(pj["arow"])
    near_a = split_steps(stabs["near_a"])
    o_a = _decode_call(
        "decode_a", cache_a, pool_maps, page_table, (512, 0), 0, 256, 256, 256, "a", qa_rows,
        _pad_rows(arow[..., 0:256], PAGE).astype(BF16), _pad_rows(arow[..., 256:512], PAGE).astype(BF16),
        bcast(stabs["new_a"]), [near_a, wp["lam"], wp["onorm4"]],
        [pl.BlockSpec(near_a.shape, lambda b, j, pt: (0, 0, 0)), pl.BlockSpec(wp["lam"].shape, lambda b, j, pt: (0, 0)),
         pl.BlockSpec(wp["onorm4"].shape, lambda b, j, pt: (0, 0))], n_near, 256, lam_init)

    lf_pages = _gather_logf(cache_lf4, l, page_table)
    lf4 = jnp.swapaxes(lf_pages.reshape(B, P, H_C), 1, 2).reshape(B, H_C, NP, PAGE)
    msu = jnp.asarray(np.tril(np.ones((LANES, LANES)), -1), BF16)
    mpu = jnp.asarray(np.triu(np.ones((NP, NP)), 1), BF16)
    c_past = _cumsum_call(lf4, msu, mpu, -1.0).reshape(B, H_C, P)
    c_past = jnp.pad(c_past, ((0, 0), (0, 8 - H_C), (0, 0)))
    logf = pj["misc"][:, 18:18 + H_C].reshape(B, S8, H_C)
    lfn = jnp.pad(jnp.swapaxes(logf, 1, 2), ((0, 0), (0, 8 - H_C), (0, LANES - S8)))[:, None]
    mw = jnp.asarray(np.triu(np.ones((LANES, LANES))), BF16)
    c_new = _cumsum_call(lfn, mw, jnp.zeros((8, 8), BF16), 1.0)[:, 0]
    cq_rows = c_new[:, :H_C, :S8].reshape(B, H_C * S8, 1)
    qc_rows = _rows_from_lanes(r3(pj["qc"]), [h * HD for h in range(H_C)], HD)
    crow = r3(pj["crow"])
    jj = np.arange(PAGE)
    tq = np.tile(np.arange(8), H_C)
    causal_new = jnp.asarray(np.where((jj[None, :] <= tq[:, None]) & (jj[None, :] < 8), 0.0, NEG), F32)
    o_c = _decode_call(
        "decode_c", cache_c, pool_maps, page_table, (768, 0), 0, 384, 384, 384, "c", qc_rows,
        _pad_rows(crow[..., 0:384], PAGE).astype(BF16), _pad_rows(crow[..., 384:768], PAGE).astype(BF16),
        bcast(causal_new), [c_past, c_new, cq_rows],
        [pl.BlockSpec((None, 8, PG * PAGE), lambda b, j, pt: (b, 0, j)),
         pl.BlockSpec((None, 8, LANES), lambda b, j, pt: (b, 0, 0)),
         pl.BlockSpec((None, H_C * S8, 1), lambda b, j, pt: (b, 0, 0))], 0, 384)

    gmask = jnp.asarray(np.arange(LANES)[None, :] // HD == np.arange(G_B)[:, None])
    qb5 = jnp.where(gmask[None, None, None], r3(pj["qb"]).reshape(B, S8, R_B, 1, LANES), jnp.zeros((), BF16))
    qb_rows = jnp.transpose(qb5, (0, 2, 3, 1, 4)).reshape(B, H_B * 8, LANES)
    kvb = r3(pj["kvb"])
    kvb_pad = _pad_rows(kvb, PAGE)
    o_bc_rows, selm1 = _scmp_call(cache_b, l, page_table, wp["cmpw_s"], kvb_pad[..., 0:256], qb_rows, P)
    rowi = np.arange(H_B * 8)
    lanei = np.arange(LANES)
    keep = jnp.asarray(((rowi[:, None] >> 3) & 1) == (lanei[None, :] >> 6))
    x = jnp.where(keep[None], o_bc_rows, 0.0).reshape(B, R_B, G_B, 8, LANES)
    o_bc = jnp.transpose(x[:, :, 0] + x[:, :, 1], (0, 2, 1, 3)).reshape(B, 8, R_B * LANES)
    selrows = jnp.broadcast_to(selm1[:, None], (B, R_B, G_B, 8, selm1.shape[-1])).reshape(B, H_B * 8, -1)
    nbs = 2 * PG
    sel_steps = jnp.swapaxes(selrows[..., :2 * NP].reshape(B, H_B * 8, NP // PG, nbs), 1, 2)
    sel_steps = jnp.pad(sel_steps, ((0, 0), (0, 0), (0, 0), (0, LANES - nbs))).astype(BF16)
    hot = (np.arange(LANES)[:, None] == np.arange(PG * PAGE)[None, :] // CMP_BLOCK) * 1e30
    hot = jnp.asarray(hot, BF16)
    amask_new = jnp.broadcast_to(selrows[..., 2 * NP:2 * NP + 1], (B, H_B * 8, PAGE)) * 1e30
    near_b = split_steps(stabs["near_b"])
    o_bs = _decode_call(
        "decode_bsel", cache_b, pool_maps, page_table, (256, 1), 0, 128, 128, 128, "bs", qb_rows,
        kvb_pad[..., 256:384].astype(BF16), kvb_pad[..., 384:512].astype(BF16),
        stabs["new_b"][None] + amask_new, [near_b, sel_steps, hot],
        [pl.BlockSpec(near_b.shape, lambda b, j, pt: (0, 0, 0)),
         pl.BlockSpec((None, None, H_B * 8, LANES), lambda b, j, pt: (b, j, 0, 0)),
         pl.BlockSpec(hot.shape, lambda b, j, pt: (0, 0))], n_near, 384)
    wb = state_win.shape[2]
    npw = wb // PAGE
    win_t = jnp.transpose(state_win, (0, 1, 3, 4, 5, 2)).reshape(state_win.shape[0], B, 2 * G_B * HD, wb)
    pt_w = jnp.zeros((B, npw), jnp.int32)
    win_maps = [functools.partial(lambda b, j, pt, i: (l, b, i), i=i) for i in range(npw)]
    win = r3(pj["win"])
    win_pad = _pad_rows(win, PAGE)
    near_w = wtabs["near_w"][None]
    o_bw = _decode_call(
        "decode_bwin", win_t, win_maps, pt_w, (256, 0), 0, 128, 128, 128, "bw", qb_rows,
        win_pad[..., 0:128].astype(BF16), win_pad[..., 128:256].astype(BF16),
        bcast(wtabs["new_w"]), [near_w], [pl.BlockSpec(near_w.shape, lambda b, j, pt: (0, 0, 0))], 1, 384)

    f2 = lambda a: a.reshape(B * S8, a.shape[-1])
    y = _merge_call(x3.reshape(B * S8, D), f2(o_a), f2(o_bc), f2(o_bs), f2(o_bw), f2(o_c), pj["misc"], pj["zs"],
                    wp["w_out"], wp["eg"])
    win_all = jnp.concatenate([state_win[l].reshape(B, wb, 2 * G_B * HD), win], axis=1)[:, -wb:]
    return (y.reshape(B, S8, D), arow.reshape(B, S8, 2, H_A, HD), kvb.reshape(B, S8, 4, G_B, HD),
            win_all.reshape(B, wb, 2, G_B, HD), crow.reshape(B, S8, 2, H_C, HD), logf)


def _window_tables(stabs, wb):
    tab_w = stabs["tab_w"]
    assert tab_w.shape[1] >= wb + 8
    rel = wb + np.arange(8)[:, None] - np.arange(wb)[None, :]
    near_w = jnp.where(jnp.asarray(rel < WINDOW)[None], _shifted_rows(tab_w, wb, 8), NEG)
    new_w = _new_rows(tab_w)
    flat = lambda a: a.reshape(a.shape[0] * a.shape[1], a.shape[2])
    return dict(near_w=flat(near_w), new_w=flat(new_w))


def kernel(x_prompt, x_sample, cache_a_kv, cache_b_kv, cache_c_kv, cache_c_logf, state_b_win, page_table,
           rel_bias, norm_g, w_in, w_out, qk_a, qk_b, qk_c, onorm_a, lam_a, cmp_w, b_f):
    depth = w_in.shape[0]
    B, S, _ = x_prompt.shape
    DB, S8, _ = x_sample.shape
    assert S8 == 8 and S % 512 == 0 and state_b_win.shape[2] == WINDOW
    T = 512
    NP = page_table.shape[1]
    P = NP * PAGE
    n_pool = cache_a_kv.shape[1]
    ptabs = _prompt_tables(rel_bias, T, S)
    stabs = _sample_tables(rel_bias, P, min(DECODE_PAGES, NP))
    wtabs = _window_tables(stabs, state_b_win.shape[2])
    fmajor = lambda c: jnp.transpose(c, (0, 1, 3, 4, 5, 2)).reshape(depth, n_pool, -1, PAGE)
    caches = (fmajor(cache_a_kv), fmajor(cache_b_kv), fmajor(cache_c_kv),
              cache_c_logf.reshape(depth, n_pool, 1, PAGE * H_C), state_b_win)
    page_table = page_table.astype(jnp.int32)
    yp, ys = x_prompt, x_sample
    outs = [[] for _ in range(10)]
    for l in range(depth):
        lam_init = 0.8 - 0.6 * math.exp(-0.3 * l)
        wp = _prep_layer(l, norm_g, w_in, w_out, qk_a, qk_b, qk_c, onorm_a, lam_a, cmp_w, b_f)
        yp, a_r, b_r, w_r, c_r, l_r = _prompt_layer(yp, wp, ptabs, lam_init, T)
        for k, v in zip((0, 2, 4, 6, 8), (a_r, b_r, w_r, c_r, l_r)):
            outs[k].append(v)
        ys, a_r, b_r, w_r, c_r, l_r = _sample_layer(l, ys, caches, page_table, wp, stabs, wtabs, lam_init)
        for k, v in zip((1, 3, 5, 7, 9), (a_r, b_r, w_r, c_r, l_r)):
            outs[k].append(v)
    return (yp, ys) + tuple(jnp.stack(o) for o in outs)
```

```python
import functools
import math

import numpy as np
import jax
import jax.numpy as jnp
from jax import lax
from jax.experimental import pallas as pl
from jax.experimental.pallas import tpu as pltpu

F32 = jnp.float32
BF16 = jnp.bfloat16

HD = 64
H_A = 4
DA = HD // 2
H_B = 6
G_B = 2
R_B = H_B // G_B
H_C = 6
CMP_BLOCK = 64
TOPK = 16
WINDOW = 512
FORCE_BONUS = float(R_B + 1)
N_BUCKETS = 32
REL_MAX_DIST = 1024
EPS = 1e-6
PAGE = 128

LOG2E = 1.4426950408889634
NEG = -1e30
M_INIT = -0.5e30
LANES = 128
VMEM_LIMIT = 56 * 1024 * 1024
DECODE_PAGES = 32

O_QB, O_KVB, O_WIN, O_GB, O_ZB, O_QC, O_FC, O_ZC = 1024, 1408, 1920, 2176, 2194, 2578, 3730, 3736
PROJ_W = 4120
P_QB, P_KVB, P_WIN, P_ZB, P_QC, P_ZC, P_MISC, P_END = 1024, 1408, 1920, 2176, 2560, 3712, 4096, 4224
SLAB_HEADS = [g * R_B + r for r in range(R_B) for g in range(G_B)]


def _cparams(n_grid):
    return pltpu.CompilerParams(dimension_semantics=("arbitrary",) * n_grid, vmem_limit_bytes=VMEM_LIMIT)


def _dot(a, b):
    return jnp.dot(a, b, preferred_element_type=F32)


def _dot_nt(a, b):
    return lax.dot_general(a, b, (((1,), (1,)), ((), ())), preferred_element_type=F32)


def _split3(x):
    x1 = x.astype(BF16)
    r = x - x1.astype(F32)
    x2 = r.astype(BF16)
    x3 = (r - x2.astype(F32)).astype(BF16)
    return x1, x2, x3


def _dot3(x, m):
    x1, x2, x3 = _split3(x)
    return _dot(x1, m) + _dot(x2, m) + _dot(x3, m)


def _dot3_l(m, x):
    x1, x2, x3 = _split3(x)
    return _dot(m, x1) + _dot(m, x2) + _dot(m, x3)


def _rel_bucket(rel):
    n = jnp.maximum(rel, 0)
    max_exact = N_BUCKETS // 2
    nf = jnp.maximum(n, 1).astype(F32)
    large = max_exact + (jnp.log(nf / max_exact) / math.log(REL_MAX_DIST / max_exact)
                         * (N_BUCKETS - max_exact)).astype(jnp.int32)
    return jnp.where(n < max_exact, n, jnp.minimum(large, N_BUCKETS - 1))


def _inproj_kernel(x_ref, w_ref, g_in, ga_q, ga_k, gb_q, gb_k, gc_q, gc_k, bf_row, cmpw_ref, bd32_ref, bd64_ref,
                   qa_o, arow_o, ka_o, va_o, qb_o, kvb_o, ksel_o, vsel_o, kcmp_o, vcmp_o, win_o, kwin_o, vwin_o,
                   qc_o, crow_o, kc_o, vc_o, zs_o, misc_o, *, tm, do_cmp, fmajor):
    x = x_ref[...]
    ms = jnp.mean(x * x, axis=-1, keepdims=True)
    h = (x * lax.rsqrt(ms + EPS) * g_in[...]).astype(BF16)

    def put_wide(ref, y, fill):
        for j in range(y.shape[1] // LANES):
            ref[:, 2 * j * LANES:(2 * j + 1) * LANES] = y[:, j * LANES:(j + 1) * LANES].astype(BF16)
            ref[:, (2 * j + 1) * LANES:(2 * j + 2) * LANES] = jnp.full((y.shape[0], LANES), fill, BF16)

    def put_values(ref, y):
        put_wide(ref, y, 1.0)

    def put_rows(ref, c0, y):
        if fmajor:
            ref[c0:c0 + y.shape[1], :] = y.T
        else:
            ref[:, c0:c0 + y.shape[1]] = y

    def proj(c0, c1):
        return _dot(h, w_ref[:, c0:c1])

    def segnorm(y, bd, seg):
        ss = _dot((y * y).astype(BF16), bd)
        return y * lax.rsqrt(ss * (1.0 / seg) + EPS)

    def silu(y):
        return y * (1.0 / (1.0 + jnp.exp(-y)))

    bd32 = bd32_ref[...]
    bd64 = bd64_ref[...]
    bd64_1 = bd64[0:128, 0:128]

    y = proj(0, 256)
    qa_o[...] = (segnorm(y, bd32, DA) * ga_q[...]).astype(BF16)
    y = segnorm(proj(256, 512), bd32, DA) * ga_k[...]
    put_rows(arow_o, 0, y)
    ka_o[...] = y.astype(BF16)
    y = proj(512, 768)
    put_rows(arow_o, 256, y)
    put_values(va_o, y)
    zs_o[:, 0:256] = silu(proj(768, 1024))

    qb_o[...] = (segnorm(proj(P_QB, P_KVB), bd64, HD) * gb_q[...]).astype(BF16)
    kcn = segnorm(proj(P_KVB, P_KVB + 128), bd64_1, HD) * gb_k[...]
    put_rows(kvb_o, 0, kcn)
    vcr = proj(P_KVB + 128, P_KVB + 256)
    put_rows(kvb_o, 128, vcr)
    y = segnorm(proj(P_KVB + 256, P_KVB + 384), bd64_1, HD) * gb_k[...]
    put_rows(kvb_o, 256, y)
    ksel_o[...] = y.astype(BF16)
    y = proj(P_KVB + 384, P_KVB + 512)
    put_rows(kvb_o, 384, y)
    put_values(vsel_o, y)
    if do_cmp:
        nbt = tm // CMP_BLOCK
        rowi = lax.broadcasted_iota(jnp.int32, (nbt, tm), 0)
        coli = lax.broadcasted_iota(jnp.int32, (nbt, tm), 1)
        blk = (coli >> 6) == rowi

        def cmp_weights(wl):
            e = jnp.where(blk, jnp.exp(wl - jnp.max(wl, axis=-1, keepdims=True)), 0.0)
            return (e / jnp.sum(e, axis=-1, keepdims=True)).astype(BF16)

        kcmp_o[...] = _dot(cmp_weights(cmpw_ref[0]), kcn.astype(BF16))
        vcmp_o[...] = _dot(cmp_weights(cmpw_ref[1]), vcr.astype(BF16))
    else:
        kcmp_o[...] = jnp.zeros(kcmp_o.shape, F32)
        vcmp_o[...] = jnp.zeros(vcmp_o.shape, F32)
    y = segnorm(proj(P_WIN, P_WIN + 128), bd64_1, HD) * gb_k[...]
    win_o[:, 0:128] = y
    kwin_o[...] = y.astype(BF16)
    y = proj(P_WIN + 128, P_WIN + 256)
    win_o[:, 128:256] = y
    put_values(vwin_o, y)
    zs_o[:, 256:640] = silu(proj(P_ZB, P_QC))

    qc_o[...] = (segnorm(proj(P_QC, P_QC + 384), bd64, HD) * gc_q[...]).astype(BF16)
    y = segnorm(proj(P_QC + 384, P_QC + 768), bd64, HD) * gc_k[...]
    put_rows(crow_o, 0, y)
    kc_o[...] = y.astype(BF16)
    y = proj(P_QC + 768, P_ZC)
    put_rows(crow_o, 384, y)
    put_values(vc_o, y)
    zs_o[:, 640:1024] = silu(proj(P_ZC, P_MISC))

    y = proj(P_MISC, P_END) + bf_row[...]
    lane = lax.broadcasted_iota(jnp.int32, y.shape, 1)
    sig = 1.0 / (1.0 + jnp.exp(-y))
    lsg = jnp.minimum(y, 0.0) - jnp.log(1.0 + jnp.exp(-jnp.abs(y)))
    misc_o[...] = jnp.where(lane < 18, sig, lsg)


def _inproj(x2, wp, *, do_cmp, seq=None):
    T = x2.shape[0]
    tm = min(512, T)
    assert T % tm == 0 and (tm == 512 or not do_cmp)
    nbt = 8
    n = T // tm
    fmajor = seq is not None
    if fmajor:
        assert seq[0] * seq[1] == T and seq[1] % tm == 0
        nsb = seq[1] // tm
    full = lambda a: pl.BlockSpec(a.shape, lambda i: (0,) * a.ndim)
    row = lambda w: pl.BlockSpec((tm, w), lambda i: (i, 0))
    consts = [wp["w_in"], wp["g_in"], wp["ga_q"], wp["ga_k"], wp["gb_q"], wp["gb_k"], wp["gc_q"], wp["gc_k"],
              wp["bf_row"], wp["cmpw_in"], wp["bd32"], wp["bd64"]]
    outs = [("qa", 256, BF16), ("arow", 512, F32), ("ka", 256, BF16), ("va", 512, BF16), ("qb", 384, BF16),
            ("kvb", 512, F32), ("ksel", 128, BF16), ("vsel", 256, BF16), ("kcmp", None, F32), ("vcmp", None, F32),
            ("win", 256, F32), ("kwin", 128, BF16), ("vwin", 256, BF16), ("qc", 384, BF16), ("crow", 768, F32),
            ("kc", 384, BF16), ("vc", 768, BF16), ("zs", 1024, F32), ("misc", 128, F32)]
    out_shape, out_specs = [], []
    for name, w, dt in outs:
        if w is None:
            out_shape.append(jax.ShapeDtypeStruct((n * nbt, 128), dt))
            out_specs.append(pl.BlockSpec((nbt, 128), lambda i: (i, 0)))
        elif fmajor and name in ("arow", "kvb", "crow"):
            out_shape.append(jax.ShapeDtypeStruct((seq[0], w, seq[1]), dt))
            out_specs.append(pl.BlockSpec((None, w, tm), lambda i: (i // nsb, 0, i % nsb)))
        else:
            out_shape.append(jax.ShapeDtypeStruct((T, w), dt))
            out_specs.append(row(w))
    res = pl.pallas_call(
        functools.partial(_inproj_kernel, tm=tm, do_cmp=do_cmp, fmajor=fmajor),
        grid=(n,),
        in_specs=[row(1024)] + [full(c) for c in consts],
        out_specs=out_specs, out_shape=out_shape,
        compiler_params=_cparams(1), name="inproj",
    )(x2, *consts)
    return {k: v for (k, _, _), v in zip(outs, res)}


def _attend(lhs_list, k_ref, v_ref, T, far_lo, near_lo, hi, near_add=None):
    n = len(lhs_list)

    def step(kt, state, near):
        ks = pl.multiple_of(kt * T, T)
        k_refs = k_ref if isinstance(k_ref, (tuple, list)) else (k_ref,)
        kk = jnp.concatenate([r[pl.ds(ks, T), :] for r in k_refs], axis=1) if len(k_refs) > 1 else k_ref[pl.ds(ks, T), :]
        vv = v_ref[pl.ds(ks, T), :]
        out = []
        for i in range(n):
            m, acc = state[i]
            s = _dot_nt(lhs_list[i], kk)
            if near:
                s = near_add[i](s, kt)
            m_new = jnp.maximum(m, jnp.max(s, axis=-1, keepdims=True))
            p = jnp.exp2((s - m_new).astype(BF16))
            alpha = jnp.exp2(m - m_new)
            acc = alpha * acc + _dot(p, vv)
            out.append((m_new, acc))
        return tuple(out)

    one = (jnp.full((T, 1), M_INIT, F32), jnp.zeros((T, v_ref.shape[-1]), F32))
    state = (one,) * n
    npair = (near_lo - far_lo) // 2
    state = lax.fori_loop(
        0, npair, lambda j, c: step(far_lo + 2 * j + 1, step(far_lo + 2 * j, c, False), False), state)
    state = lax.fori_loop(far_lo + 2 * npair, near_lo, lambda kt, c: step(kt, c, False), state)
    state = lax.fori_loop(near_lo, hi, lambda kt, c: step(kt, c, True), state)
    return [acc[:, :LANES] / acc[:, LANES:] for _, acc in state]


def _lane_band(x, lo, width):
    lane = lax.broadcasted_iota(jnp.int32, x.shape, 1)
    return jnp.where((lane >= lo) & (lane < lo + width), x, jnp.zeros_like(x))


def _halves(a, b):
    lane = lax.broadcasted_iota(jnp.int32, a.shape, 1)
    return jnp.where(lane < HD, a, b)


def _diff_lambda(lam_ref, lam_init):
    lf = lam_ref[...]
    a = jnp.sum(lf[0:1] * lf[1:2], axis=-1, keepdims=True)
    b = jnp.sum(lf[2:3] * lf[3:4], axis=-1, keepdims=True)
    return jnp.exp(a) - jnp.exp(b) + lam_init


def _flash_a_kernel(q_ref, k_ref, v_ref, bias_ref, lam_ref, onorm_ref, o_ref, *, T, noff, lam_init):
    qi = pl.program_id(2)
    near_lo = jnp.maximum(qi - (noff - 1), 0)
    q = q_ref[...]
    lam = _diff_lambda(lam_ref, lam_init)
    lhs = [_lane_band(q, hl * HD + mp * DA, DA) for hl in range(2) for mp in range(2)]
    add = [functools.partial(lambda s, kt, hl: s + bias_ref[hl, qi - kt], hl=i // 2) for i in range(4)]
    o = _attend(lhs, k_ref, v_ref, T, 0, near_lo, qi + 1, near_add=add)
    o = _halves(o[0] - lam * o[1], o[2] - lam * o[3])
    x2 = o * o
    lane = lax.broadcasted_iota(jnp.int32, o.shape, 1)
    s0 = jnp.sum(jnp.where(lane < HD, x2, 0.0), axis=-1, keepdims=True)
    s1 = jnp.sum(jnp.where(lane < HD, 0.0, x2), axis=-1, keepdims=True)
    ss = jnp.where(lane < HD, s0, s1)
    o_ref[...] = o * lax.rsqrt(ss * (1.0 / HD) + EPS) * onorm_ref[...] * (1.0 - lam_init)


def _flash_c_kernel(q_ref, qx_ref, k_ref, kx_ref, v_ref, mask_ref, o_ref, *, T):
    qi = pl.program_id(2)
    q = q_ref[...]
    qx = qx_ref[...]
    lane = lax.broadcasted_iota(jnp.int32, qx.shape, 1)
    lhs = []
    for hl in range(2):
        own = ((lane >= 3 * hl) & (lane < 3 * hl + 3)) | ((lane >= 6 + 3 * hl) & (lane < 9 + 3 * hl))
        lhs.append(jnp.concatenate([_lane_band(q, hl * HD, HD), jnp.where(own, qx, jnp.zeros_like(qx))], axis=1))
    causal = [lambda s, kt: s + mask_ref[...]] * 2
    o = _attend(lhs, (k_ref, kx_ref), v_ref, T, 0, qi, qi + 1, near_add=causal)
    o_ref[...] = _halves(o[0], o[1])


def _flash_b_kernel(*refs, T, noff, selected):
    if selected:
        q_ref, sel_ref, ksel_ref, hot_ref, v_ref, bias_ref, o_ref = refs
        k_ref = (ksel_ref, hot_ref)
    else:
        q_ref, k_ref, v_ref, bias_ref, o_ref = refs
    qi = pl.program_id(2)
    near_lo = jnp.maximum(qi - (noff - 1), 0)
    far_lo = 0 if selected else near_lo
    q = q_ref[...]
    lhs = [_lane_band(q, g * HD, HD) for g in range(G_B)]
    if selected:
        nbp = sel_ref.shape[-1] // G_B
        lhs = [jnp.concatenate([lhs[g], sel_ref[:, g * nbp:(g + 1) * nbp]], axis=1) for g in range(G_B)]
    add = [functools.partial(lambda s, kt, g: s + bias_ref[g, qi - kt], g=g) for g in range(G_B)]
    o = _attend(lhs, k_ref, v_ref, T, far_lo, near_lo, qi + 1, near_add=add)
    o_ref[...] = _halves(o[0], o[1])


def _flash_call(kernel, name, B, S, T, n_grp, ins, in_specs, out_w):
    return pl.pallas_call(
        kernel, grid=(B, n_grp, S // T), in_specs=in_specs,
        out_specs=pl.BlockSpec((None, T, LANES), lambda b, g, i: (b, i, g)),
        out_shape=jax.ShapeDtypeStruct((B, S, out_w), F32),
        compiler_params=_cparams(3), name=name,
    )(*ins)


def _qspec(T):
    return pl.BlockSpec((None, T, LANES), lambda b, g, i: (b, i, g))


def _kvspec(S, w, per_group):
    if per_group:
        return pl.BlockSpec((None, S, w), lambda b, g, i: (b, 0, g))
    return pl.BlockSpec((None, S, w), lambda b, g, i: (b, 0, 0))


def _biasspec(noff, T):
    return pl.BlockSpec((2, noff, T, T), lambda b, g, i: (g, 0, 0, 0))


def _small(a):
    return pl.BlockSpec(a.shape, lambda b, g, i: (0,) * a.ndim)


def _masked_softmax2(s, mask, axis):
    s = jnp.where(mask, s, NEG)
    m = jnp.max(s, axis=axis, keepdims=True)
    p = jnp.where(mask, jnp.exp2(s - m), 0.0)
    return p / jnp.maximum(jnp.sum(p, axis=axis, keepdims=True), 1e-30)


def _topk_select(score, idx, n_idx, axis):
    def it(_, c):
        sc, sel = c
        mx = jnp.max(sc, axis=axis, keepdims=True)
        first = jnp.min(jnp.where(sc == mx, idx, n_idx), axis=axis, keepdims=True)
        oh = idx == first
        sel = jnp.where(oh, jnp.where(mx >= 0.0, 1.0, sel), sel)
        sc = jnp.where(oh, -2.0, sc)
        return sc, sel

    _, sel = lax.fori_loop(0, TOPK, it, (score, jnp.zeros(score.shape, F32)))
    return sel


def _cmp_kernel(q_ref, kc_ref, vc_ref, oc_ref, selt_ref, *, T, NBP):
    qi = pl.program_id(1)
    q = q_ref[...]
    kc = kc_ref[...]
    vc = vc_ref[...]
    qpos_r = qi * T + lax.broadcasted_iota(jnp.int32, (T, NBP), 0)
    blk_c = lax.broadcasted_iota(jnp.int32, (T, NBP), 1)
    cm = ((blk_c + 1) * CMP_BLOCK - 1) <= qpos_r
    qpos_c = qi * T + lax.broadcasted_iota(jnp.int32, (NBP, T), 1)
    blk_r = lax.broadcasted_iota(jnp.int32, (NBP, T), 0)
    cmt = ((blk_r + 1) * CMP_BLOCK - 1) <= qpos_c
    imp = [jnp.zeros((NBP, T), F32) for _ in range(G_B)]
    for r in range(R_B):
        slab = q[:, r * LANES:(r + 1) * LANES]
        outs = []
        for g in range(G_B):
            qm = _lane_band(slab, g * HD, HD)
            p = _masked_softmax2(_dot_nt(qm, kc), cm, -1)
            outs.append(_dot(p.astype(BF16), vc))
            imp[g] = imp[g] + _masked_softmax2(_dot_nt(kc, qm), cmt, 0)
        oc_ref[:, r * LANES:(r + 1) * LANES] = _halves(outs[0], outs[1])
    qblk = qpos_c >> 6
    forced = (blk_r == qblk) | (blk_r == qblk - 1) | (blk_r == 0)
    valid = blk_r * CMP_BLOCK <= qpos_c
    blk_q = lax.broadcasted_iota(jnp.int32, (NBP, LANES), 0)
    for g in range(G_B):
        score = jnp.where(valid, imp[g] + jnp.where(forced, FORCE_BONUS, 0.0), -1.0)
        for c in range(T // LANES):
            cs = slice(c * LANES, (c + 1) * LANES)
            selt_ref[g, :, cs] = _topk_select(score[:, cs], blk_q, NBP, 0) - 1.0


def _cmp_call(qb3, kcmp3, vcmp3, T):
    B, S, _ = qb3.shape
    NBP = kcmp3.shape[1]
    return pl.pallas_call(
        functools.partial(_cmp_kernel, T=T, NBP=NBP), grid=(B, S // T),
        in_specs=[pl.BlockSpec((None, T, 384), lambda b, i: (b, i, 0)),
                  pl.BlockSpec((None, NBP, LANES), lambda b, i: (b, 0, 0)),
                  pl.BlockSpec((None, NBP, LANES), lambda b, i: (b, 0, 0))],
        out_specs=[pl.BlockSpec((None, T, 384), lambda b, i: (b, i, 0)),
                   pl.BlockSpec((None, G_B, NBP, T), lambda b, i: (b, 0, 0, i))],
        out_shape=[jax.ShapeDtypeStruct((B, S, 384), F32), jax.ShapeDtypeStruct((B, G_B, NBP, S), F32)],
        compiler_params=_cparams(2), name="nsa_cmp_topk",
    )(qb3, kcmp3, vcmp3)


def _cumsum_kernel(x_ref, mw_ref, mr_ref, *o_refs, nh, sign, split):
    mw = mw_ref[...]
    mr = mr_ref[...]
    ones = jnp.ones((LANES, LANES), BF16)
    for h in range(nh):
        x = x_ref[h]
        tot = _dot3(x, ones)
        c = (_dot3(x, mw) + _dot3_l(mr, tot)) * (sign * LOG2E)
        if split:
            for o_ref, part in zip(o_refs, _split3(c)):
                o_ref[h] = part
        else:
            o_refs[0][h] = c


def _cumsum_call(x4, mw, mr, sign, split=False):
    B, nh, nr, _ = x4.shape
    spec = pl.BlockSpec((None, nh, nr, LANES), lambda b: (b, 0, 0, 0))
    return pl.pallas_call(
        functools.partial(_cumsum_kernel, nh=nh, sign=sign, split=split), grid=(B,),
        in_specs=[spec, pl.BlockSpec(mw.shape, lambda b: (0, 0)), pl.BlockSpec(mr.shape, lambda b: (0, 0))],
        out_specs=[spec] * 3 if split else spec,
        out_shape=[jax.ShapeDtypeStruct(x4.shape, BF16)] * 3 if split else jax.ShapeDtypeStruct(x4.shape, F32),
        compiler_params=_cparams(1), name="logf_cumsum",
    )(x4, mw, mr)


def _merge_kernel(x_ref, oa_ref, obc_ref, obs_ref, obw_ref, oc_ref, misc_ref, zs_ref, w_ref, eg_ref, y_ref):
    g = misc_ref[...]
    g1 = g.astype(BF16)
    g2 = (g - g1.astype(F32)).astype(BF16)

    def gate(c):
        return _dot(g1, eg_ref[c]) + _dot(g2, eg_ref[c])

    ob = gate(0) * obc_ref[...] + gate(1) * obs_ref[...] + gate(2) * obw_ref[...]
    ma = (oa_ref[...] * zs_ref[:, 0:256]).astype(BF16)
    mb = (ob * zs_ref[:, 256:640]).astype(BF16)
    mc = (oc_ref[...] * zs_ref[:, 640:1024]).astype(BF16)
    y_ref[...] = x_ref[...] + _dot(ma, w_ref[0:256, :]) + _dot(mb, w_ref[256:640, :]) + _dot(mc, w_ref[640:1024, :])


def _merge_call(x2, oa, obc, obs, obw, oc, misc, zs, w_out, eg):
    T = x2.shape[0]
    tm = min(512, T)
    row = lambda w: pl.BlockSpec((tm, w), lambda i: (i, 0))
    return pl.pallas_call(
        _merge_kernel, grid=(T // tm,),
        in_specs=[row(1024), row(256), row(384), row(384), row(384), row(384), row(128), row(1024),
                  pl.BlockSpec(w_out.shape, lambda i: (0, 0)), pl.BlockSpec(eg.shape, lambda i: (0, 0, 0))],
        out_specs=row(1024), out_shape=jax.ShapeDtypeStruct((T, 1024), F32),
        compiler_params=_cparams(1), name="merge_outproj",
    )(x2, oa, obc, obs, obw, oc, misc, zs, w_out, eg)


def _gather_rows_kernel(pt_ref, *refs, n):
    out = refs[n]
    for i in range(n):
        out[i] = refs[i][...]


def _gather_logf(cache4, layer, page_table):
    B, NP = page_table.shape
    n = min(16, NP)
    assert NP % n == 0
    w = cache4.shape[-1]
    in_specs = [pl.BlockSpec((None, None, 1, w), functools.partial(
        lambda b, j, pt, i: (layer, pt[b, j * n + i], 0, 0), i=i)) for i in range(n)]
    return pl.pallas_call(
        functools.partial(_gather_rows_kernel, n=n),
        grid_spec=pltpu.PrefetchScalarGridSpec(
            num_scalar_prefetch=1, grid=(B, NP // n), in_specs=in_specs,
            out_specs=pl.BlockSpec((None, n, 1, w), lambda b, j, pt: (b, j, 0, 0))),
        out_shape=jax.ShapeDtypeStruct((B, NP, 1, w), F32),
        compiler_params=_cparams(2), name="gather_logf",
    )(page_table, *([cache4] * n))


def _scmp_kernel(pt_ref, *refs, PG, NP, NBP, P):
    pages = refs[:PG]
    cmpw_ref, new_ref, qb_ref, oc_ref, sel_ref, kcs, vcs = refs[PG:]
    jc = pl.program_id(1)
    ns = pl.num_programs(1)
    nrow = 2 * PG

    @pl.when(jc == 0)
    def _():
        kcs[...] = jnp.zeros(kcs.shape, F32)
        vcs[...] = jnp.zeros(vcs.shape, F32)

    rowi = lax.broadcasted_iota(jnp.int32, (nrow, PAGE), 0)
    coli = lax.broadcasted_iota(jnp.int32, (nrow, PAGE), 1)

    def half_softmax(wl):
        e = jnp.exp(wl - jnp.max(wl, axis=-1, keepdims=True))
        return e / (0.5 * jnp.sum(e, axis=-1, keepdims=True))

    wk = half_softmax(cmpw_ref[0])
    wv = half_softmax(cmpw_ref[1])
    kacc = jnp.zeros((nrow, LANES), F32)
    vacc = jnp.zeros((nrow, LANES), F32)
    for i in range(PG):
        sel = (coli >> 6) + 2 * i == rowi
        pg = pages[i][...]
        kacc = kacc + _dot_nt(jnp.where(sel, wk, 0.0).astype(BF16), pg[0:128, :].astype(BF16))
        vacc = vacc + _dot_nt(jnp.where(sel, wv, 0.0).astype(BF16), pg[128:256, :].astype(BF16))
    r0 = pl.multiple_of(jc * nrow, nrow)
    kcs[pl.ds(r0, nrow), :] = kacc
    vcs[pl.ds(r0, nrow), :] = vacc

    @pl.when(jc == ns - 1)
    def _():
        first = (rowi == 0) & (coli < CMP_BLOCK)
        new = new_ref[...]
        kcs[2 * NP:2 * NP + nrow, :] = _dot(jnp.where(first, wk, 0.0).astype(BF16), new[:, 0:128].astype(BF16))
        vcs[2 * NP:2 * NP + nrow, :] = _dot(jnp.where(first, wv, 0.0).astype(BF16), new[:, 128:256].astype(BF16))
        kc = kcs[...].astype(BF16)
        vc = vcs[...].astype(BF16)
        R = qb_ref.shape[0]
        s = _dot_nt(qb_ref[...], kc)
        t_r = lax.broadcasted_iota(jnp.int32, (R, NBP), 0) & 7
        blk = lax.broadcasted_iota(jnp.int32, (R, NBP), 1)
        cm = ((blk + 1) * CMP_BLOCK - 1) <= (P + t_r)
        p = _masked_softmax2(s, cm, -1)
        oc_ref[...] = _dot(p.astype(BF16), vc)
        qpos = P + lax.broadcasted_iota(jnp.int32, (8, NBP), 0)
        blk8 = lax.broadcasted_iota(jnp.int32, (8, NBP), 1)
        qblk = qpos >> 6
        forced = (blk8 == qblk) | (blk8 == qblk - 1) | (blk8 == 0)
        valid = blk8 * CMP_BLOCK <= qpos
        for g in range(G_B):
            imp = sum(p[(r * G_B + g) * 8:(r * G_B + g) * 8 + 8] for r in range(R_B))
            score = jnp.where(valid, imp + jnp.where(forced, FORCE_BONUS, 0.0), -1.0)
            sel_ref[g] = _topk_select(score, blk8, NBP, -1) - 1.0


def _scmp_call(cache_b, layer, page_table, cmpw_s, new_pad, qb_rows, P):
    B, NP = page_table.shape
    PG = min(DECODE_PAGES, NP)
    assert NP % PG == 0
    NBP = -(-(2 * NP + 2 * PG) // LANES) * LANES
    in_specs = [pl.BlockSpec((None, None, 256, PAGE), functools.partial(
        lambda b, j, pt, i: (layer, pt[b, j * PG + i], 0, 0), i=i)) for i in range(PG)]
    in_specs += [pl.BlockSpec(cmpw_s.shape, lambda b, j, pt: (0, 0, 0)),
                 pl.BlockSpec((None, PAGE, 256), lambda b, j, pt: (b, 0, 0)),
                 pl.BlockSpec((None,) + qb_rows.shape[1:], lambda b, j, pt: (b, 0, 0))]
    R = qb_rows.shape[1]
    return pl.pallas_call(
        functools.partial(_scmp_kernel, PG=PG, NP=NP, NBP=NBP, P=P),
        grid_spec=pltpu.PrefetchScalarGridSpec(
            num_scalar_prefetch=1, grid=(B, NP // PG), in_specs=in_specs,
            out_specs=[pl.BlockSpec((None, R, LANES), lambda b, j, pt: (b, 0, 0)),
                       pl.BlockSpec((None, G_B, 8, NBP), lambda b, j, pt: (b, 0, 0, 0))],
            scratch_shapes=[pltpu.VMEM((NBP, LANES), F32), pltpu.VMEM((NBP, LANES), F32)]),
        out_shape=[jax.ShapeDtypeStruct((B, R, LANES), F32), jax.ShapeDtypeStruct((B, G_B, 8, NBP), F32)],
        compiler_params=_cparams(2), name="sample_cmp_topk",
    )(page_table, *([cache_b] * PG), cmpw_s, new_pad, qb_rows)


def _decode_kernel(pt_ref, *refs, PG, kc0, KD, vc0, VD, mode, n_near, lam_init):
    pages = refs[:PG]
    rest = list(refs[PG:])
    qb_ref, knew_ref, vnew_ref, addnew_ref = rest[:4]
    rest = rest[4:]
    bias_ref = ck_ref = cnew_ref = cq_ref = amask_ref = lam_ref = onorm_ref = None
    if mode == "a":
        bias_ref, lam_ref, onorm_ref = rest[:3]
        rest = rest[3:]
    elif mode == "c":
        ck_ref, cnew_ref, cq_ref = rest[:3]
        rest = rest[3:]
    elif mode == "bs":
        bias_ref, amask_ref, hot_ref = rest[:3]
        rest = rest[3:]
    else:
        bias_ref = rest[0]
        rest = rest[1:]
    o_ref, m_s, l_s, acc_s = rest
    jc = pl.program_id(1)
    ns = pl.num_programs(1)
    R = qb_ref.shape[0]

    @pl.when(jc == 0)
    def _():
        m_s[...] = jnp.full(m_s.shape, M_INIT, F32)
        l_s[...] = jnp.zeros(l_s.shape, F32)
        acc_s[...] = jnp.zeros(acc_s.shape, F32)

    def rows_of_heads(c):
        return jnp.concatenate([jnp.broadcast_to(c[h:h + 1, :], (8, c.shape[1])) for h in range(R // 8)], axis=0)

    def update(s, v, v_feature_major):
        m = m_s[...]
        m_new = jnp.maximum(m, jnp.max(s, axis=-1, keepdims=True))
        p = jnp.exp2(s - m_new)
        alpha = jnp.exp2(m - m_new)
        l_s[...] = alpha * l_s[...] + jnp.sum(p, axis=-1, keepdims=True)
        pb = p.astype(BF16)
        acc_s[...] = alpha * acc_s[...] + (_dot_nt(pb, v) if v_feature_major else _dot(pb, v))
        m_s[...] = m_new

    q = qb_ref[...]
    kcat = jnp.concatenate([pg[kc0:kc0 + KD, :].astype(BF16) for pg in pages], axis=1)
    vcat = jnp.concatenate([pg[vc0:vc0 + VD, :].astype(BF16) for pg in pages], axis=1)
    s = _dot(q, kcat)
    if mode == "c":
        s = s + cq_ref[...] - rows_of_heads(ck_ref[...])
    if mode == "bs":
        s = s + _dot(amask_ref[...], hot_ref[...])
    if bias_ref is not None:
        e = jc - (ns - n_near)
        s = s + jnp.where(e >= 0, 1.0, 0.0) * bias_ref[jnp.maximum(e, 0)]
    update(s, vcat, True)

    @pl.when(jc == ns - 1)
    def _():
        s2 = _dot_nt(q, knew_ref[...]) + addnew_ref[...]
        if mode == "c":
            s2 = s2 + cq_ref[...] - rows_of_heads(cnew_ref[...])
        update(s2, vnew_ref[...], False)
        o = acc_s[...] / l_s[...]
        if mode == "a":
            lam = _diff_lambda(lam_ref, lam_init)
            half = R // 2
            pd = o[0:half] - lam * o[half:R]
            rowi = lax.broadcasted_iota(jnp.int32, pd.shape, 0)
            lanei = lax.broadcasted_iota(jnp.int32, pd.shape, 1)
            x = jnp.where((rowi >> 3) == (lanei >> 6), pd, 0.0)
            ss = jnp.sum(x * x, axis=-1, keepdims=True)
            y = x * lax.rsqrt(ss * (1.0 / HD) + EPS) * onorm_ref[...] * (1.0 - lam_init)
            o_ref[...] = sum(y[h * 8:(h + 1) * 8] for h in range(H_A))
        elif mode == "c":
            rowi = lax.broadcasted_iota(jnp.int32, o.shape, 0)
            lanei = lax.broadcasted_iota(jnp.int32, o.shape, 1)
            x = jnp.where((rowi >> 3) == (lanei >> 6), o, 0.0)
            o_ref[...] = sum(x[h * 8:(h + 1) * 8] for h in range(H_C))
        else:
            rowi = lax.broadcasted_iota(jnp.int32, o.shape, 0)
            lanei = lax.broadcasted_iota(jnp.int32, o.shape, 1)
            x = jnp.where(((rowi >> 3) & 1) == (lanei >> 6), o, 0.0)
            o_ref[...] = jnp.concatenate(
                [x[(2 * r) * 8:(2 * r) * 8 + 8] + x[(2 * r + 1) * 8:(2 * r + 1) * 8 + 8] for r in range(R_B)], axis=1)


def _decode_call(name, cache, page_maps, page_table, rowblk, kc0, KD, vc0, VD, mode, qb_rows, knew, vnew, addnew,
                 extras, extra_specs, n_near, out_w, lam_init=0.0):
    B, NP = page_table.shape
    PG = len(page_maps)
    assert NP % PG == 0
    R = qb_rows.shape[1]
    bh, bi = rowblk

    def page_spec(f):
        def index(b, j, pt):
            d0, d1, lane_blk = f(b, j, pt)
            return (d0, d1, bi, lane_blk)
        return pl.BlockSpec((None, None, bh, PAGE), index)

    in_specs = [page_spec(f) for f in page_maps]
    bspec = lambda a: pl.BlockSpec((None,) + a.shape[1:], lambda b, j, pt: (b,) + (0,) * (a.ndim - 1))
    in_specs += [bspec(qb_rows), bspec(knew), bspec(vnew), bspec(addnew)] + extra_specs
    return pl.pallas_call(
        functools.partial(_decode_kernel, PG=PG, kc0=kc0, KD=KD, vc0=vc0, VD=VD, mode=mode, n_near=n_near,
                          lam_init=lam_init),
        grid_spec=pltpu.PrefetchScalarGridSpec(
            num_scalar_prefetch=1, grid=(B, NP // PG), in_specs=in_specs,
            out_specs=pl.BlockSpec((None, 8, out_w), lambda b, j, pt: (b, 0, 0)),
            scratch_shapes=[pltpu.VMEM((R, 1), F32), pltpu.VMEM((R, 1), F32), pltpu.VMEM((R, VD), F32)]),
        out_shape=jax.ShapeDtypeStruct((B, 8, out_w), F32),
        compiler_params=_cparams(2), name=name,
    )(page_table, *([cache] * PG), qb_rows, knew, vnew, addnew, *extras)


def _prep_layer(l, norm_g, w_in, w_out, qk_a, qk_b, qk_c, onorm_a, lam_a, cmp_w, b_f):
    wl = w_in[l]
    slab_cols = lambda base: [wl[:, base + h * HD:base + (h + 1) * HD] for h in SLAB_HEADS]
    w = jnp.concatenate(
        [wl[:, 0:O_QB]] + slab_cols(O_QB) + [wl[:, O_KVB:O_GB]] + slab_cols(O_ZB)
        + [wl[:, O_QC:O_FC], wl[:, O_ZC:PROJ_W], wl[:, O_GB:O_GB + 18], wl[:, O_FC:O_FC + H_C],
           jnp.zeros((wl.shape[0], P_END - PROJ_W), wl.dtype)], axis=1).astype(BF16)
    wol = w_out[l]
    wo = jnp.concatenate([wol[0:256]] + [wol[256 + h * HD:256 + (h + 1) * HD] for h in SLAB_HEADS]
                         + [wol[640:1024]], axis=0).astype(BF16)
    tile = lambda v, n: jnp.tile(v.astype(F32), n)[None, :]
    bf_row = jnp.zeros((1, LANES), F32).at[0, 18:18 + H_C].set(b_f[l].astype(F32))
    bd = lambda n, seg: jnp.asarray(np.kron(np.eye(n // seg), np.ones((seg, seg))), BF16)
    eg = np.zeros((3, LANES, 384), np.float32)
    for g in range(G_B):
        for r in range(R_B):
            for c in range(3):
                eg[c, g * 9 + r * 3 + c, (r * G_B + g) * HD:(r * G_B + g + 1) * HD] = 1.0
    return dict(
        w_in=w, w_out=wo, g_in=norm_g[l].astype(F32)[None, :],
        ga_q=tile(qk_a[l, 0], 8) * (DA ** -0.5 * LOG2E), ga_k=tile(qk_a[l, 1], 8),
        gb_q=tile(qk_b[l, 0], 6) * (HD ** -0.5 * LOG2E), gb_k=tile(qk_b[l, 1], 2),
        gc_q=tile(qk_c[l, 0], 6) * (HD ** -0.5 * LOG2E), gc_k=tile(qk_c[l, 1], 6),
        bf_row=bf_row, bd32=bd(256, DA), bd64=bd(384, HD),
        cmpw_in=jnp.tile(cmp_w[l].astype(F32)[:, None, :], (1, 8, 8)),
        cmpw_s=jnp.tile(cmp_w[l].astype(F32)[:, None, :], (1, 1, 2)),
        onorm2=tile(onorm_a[l], 2), onorm4=tile(onorm_a[l], 4), lam=lam_a[l].astype(F32),
        eg=jnp.asarray(eg, BF16),
    )


def _toeplitz(tab_h, noff, T, mask):
    H, nt = tab_h.shape
    seg = 2 * T - 1
    padded = jnp.pad(tab_h, ((0, 0), (T - 1, max(0, noff * T - nt))))
    tiles = []
    for d in range(noff):
        u = padded[:, d * T:d * T + seg][:, ::-1]
        flat = jnp.broadcast_to(u[:, None, :], (H, T, seg)).reshape(H, T * seg)
        tiles.append(flat[:, T - 1:T - 1 + T * (seg - 1)].reshape(H, T, seg - 1)[:, :, :T])
    return jnp.where(jnp.asarray(mask)[None], jnp.stack(tiles, axis=1), NEG)


def _shifted_rows(tab_h, width, n_t):
    return jnp.stack([tab_h[:, t + 1:t + 1 + width][:, ::-1] for t in range(n_t)], axis=1)


def _prompt_layer(x3, wp, tabs, lam_init, T):
    B, S, D = x3.shape
    pj = _inproj(x3.reshape(B * S, D), wp, do_cmp=True, seq=(B, S))
    r3 = lambda a: a.reshape(B, S, a.shape[-1])
    rows_out = lambda a, *dims: jnp.moveaxis(a.reshape((B,) + dims + (S,)), -1, 1)
    o_a = _flash_call(
        functools.partial(_flash_a_kernel, T=T, noff=tabs["noff"], lam_init=lam_init), "flash_a", B, S, T, 2,
        [r3(pj["qa"]), r3(pj["ka"]), r3(pj["va"]), tabs["bias_a"], wp["lam"], wp["onorm2"]],
        [_qspec(T), _kvspec(S, LANES, True), _kvspec(S, 2 * LANES, True), _biasspec(tabs["noff"], T),
         _small(wp["lam"]), _small(wp["onorm2"])], 256)
    logf = pj["misc"][:, 18:18 + H_C].reshape(B, S, H_C)
    nr = S // LANES
    lf4 = jnp.swapaxes(logf, 1, 2).reshape(B, H_C, nr, LANES)
    mw = jnp.asarray(np.triu(np.ones((LANES, LANES))), BF16)
    mr = jnp.asarray(np.tril(np.ones((nr, nr)), -1), BF16)
    cparts = _cumsum_call(lf4, mw, mr, 1.0, split=True)
    c_split = jnp.stack([p.reshape(B, 3, 2, S) for p in cparts], axis=-1)
    c_split = jnp.transpose(c_split, (0, 3, 1, 2, 4)).reshape(B, S, 3, 6)
    ones6 = jnp.ones((B, S, 3, 6), BF16)
    zpad = jnp.zeros((B, S, 3, LANES - 12), BF16)
    q_extra = jnp.concatenate([-ones6, c_split, zpad], axis=-1).reshape(B, S, 3 * LANES)
    k_extra = jnp.concatenate([c_split, ones6, zpad], axis=-1).reshape(B, S, 3 * LANES)
    o_c = _flash_call(
        functools.partial(_flash_c_kernel, T=T), "flash_c", B, S, T, 3,
        [r3(pj["qc"]), q_extra, r3(pj["kc"]), k_extra, r3(pj["vc"]), tabs["causal"]],
        [_qspec(T), _qspec(T), _kvspec(S, LANES, True), _kvspec(S, LANES, True), _kvspec(S, 2 * LANES, True),
         _small(tabs["causal"])], 384)
    NB = S // CMP_BLOCK
    NBP = -(-NB // LANES) * LANES
    padb = lambda a: jnp.pad(a.reshape(B, NB, LANES), ((0, 0), (0, NBP - NB), (0, 0))).astype(BF16)
    o_bc, selt = _cmp_call(r3(pj["qb"]), padb(pj["kcmp"]), padb(pj["vcmp"]), T)
    sel = jnp.swapaxes(selt, 2, 3)
    sel = jnp.swapaxes(sel, 1, 2).reshape(B, S, G_B * NBP).astype(BF16)
    o_bs = _flash_call(
        functools.partial(_flash_b_kernel, T=T, noff=tabs["noff"], selected=True), "flash_bsel", B, S, T, 3,
        [r3(pj["qb"]), sel, r3(pj["ksel"]), tabs["blockhot"], r3(pj["vsel"]), tabs["bias_b"]],
        [_qspec(T), pl.BlockSpec((None, T, G_B * NBP), lambda b, g, i: (b, i, 0)),
         _kvspec(S, LANES, False), _small(tabs["blockhot"]), _kvspec(S, 2 * LANES, False),
         _biasspec(tabs["noff"], T)], 384)
    o_bw = _flash_call(
        functools.partial(_flash_b_kernel, T=T, noff=2, selected=False), "flash_bwin", B, S, T, 3,
        [r3(pj["qb"]), r3(pj["kwin"]), r3(pj["vwin"]), tabs["bias_w"]],
        [_qspec(T), _kvspec(S, LANES, False), _kvspec(S, 2 * LANES, False), _biasspec(2, T)], 384)
    f2 = lambda a: a.reshape(B * S, a.shape[-1])
    y = _merge_call(x3.reshape(B * S, D), f2(o_a), f2(o_bc), f2(o_bs), f2(o_bw), f2(o_c), pj["misc"], pj["zs"],
                    wp["w_out"], wp["eg"])
    w = min(WINDOW, S)
    return (y.reshape(B, S, D), rows_out(pj["arow"], 2, H_A, HD), rows_out(pj["kvb"], 4, G_B, HD),
            pj["win"].reshape(B, S, 2 * G_B * HD)[:, S - w:].reshape(B, w, 2, G_B, HD),
            rows_out(pj["crow"], 2, H_C, HD), logf)


def _prompt_tables(rel_bias, T, S):
    noff = -(-(REL_MAX_DIST - 1) // T) + 1
    noff = min(noff, S // T)
    nt = noff * T + 1
    bucket = _rel_bucket(jnp.arange(nt, dtype=jnp.int32))
    tab = jnp.take(rel_bias.astype(F32), bucket, axis=0).T
    far = rel_bias.astype(F32)[N_BUCKETS - 1][:, None]
    i = np.arange(T)[:, None]
    j = np.arange(T)[None, :]
    rel = np.stack([d * T + i - j for d in range(noff)])
    tab_s = (tab - far) * LOG2E
    heads_b = np.asarray([H_A + h for h in SLAB_HEADS])
    tab_b = jnp.stack([tab_s[h] for h in heads_b])
    tab_w = jnp.stack([tab[h] for h in heads_b]) * LOG2E
    bias_a = _toeplitz(tab_s[:H_A], noff, T, rel >= 0)
    bias_b = _toeplitz(tab_b, noff, T, rel >= 0)
    relw = np.stack([d * T + i - j for d in range(2)])
    bias_w = _toeplitz(tab_w, 2, T, (relw >= 0) & (relw < WINDOW))
    causal = jnp.asarray(np.where(i >= j, 0.0, NEG), F32)
    NB = S // CMP_BLOCK
    NBP = -(-NB // LANES) * LANES
    hot = (np.arange(S)[:, None] // CMP_BLOCK == np.arange(NBP)[None, :]) * 1e30
    return dict(noff=noff, bias_a=bias_a, bias_b=bias_b, bias_w=bias_w, causal=causal,
                blockhot=jnp.asarray(hot, BF16))


def _sample_tables(rel_bias, P, PG):
    NP = P // PAGE
    n_near_pages = min(NP, -(-REL_MAX_DIST // (PAGE * PG)) * PG)
    assert n_near_pages % PG == 0 or NP == n_near_pages
    nt = n_near_pages * PAGE + 16
    bucket = _rel_bucket(jnp.arange(nt, dtype=jnp.int32))
    rb = rel_bias.astype(F32)
    tab = jnp.take(rb, bucket, axis=0).T
    far = rb[N_BUCKETS - 1][:, None]
    tab_s = (tab - far) * LOG2E
    tab_a = tab_s[:H_A]
    tab_b = jnp.stack([tab_s[H_A + h] for h in SLAB_HEADS])
    wn = n_near_pages * PAGE
    flat = lambda a: a.reshape(a.shape[0] * a.shape[1], a.shape[2])
    near_a = flat(_shifted_rows(tab_a, wn, 8))
    new_a = flat(_new_rows(tab_a))
    return dict(n_near_pages=n_near_pages,
                near_a=jnp.concatenate([near_a, near_a], axis=0), new_a=jnp.concatenate([new_a, new_a], axis=0),
                near_b=flat(_shifted_rows(tab_b, wn, 8)), new_b=flat(_new_rows(tab_b)),
                tab_w=jnp.stack([tab[H_A + h] for h in SLAB_HEADS]) * LOG2E)


def _new_rows(tab_h):
    rows = [jnp.pad(tab_h[:, :t + 1][:, ::-1], ((0, 0), (0, PAGE - t - 1)), constant_values=NEG) for t in range(8)]
    return jnp.stack(rows, axis=1)


def _rows_from_lanes(q, bands, width):
    lane = np.arange(q.shape[-1])
    m = np.stack([(lane >= lo) & (lane < lo + width) for lo in bands])
    return jnp.where(jnp.asarray(m)[None, :, None, :], q[:, None], jnp.zeros((), q.dtype)).reshape(
        q.shape[0], len(bands) * 8, q.shape[-1])


def _pad_rows(a, n):
    return jnp.pad(a, ((0, 0), (0, n - a.shape[1]), (0, 0)))


def _sample_layer(l, x3, caches, page_table, wp, stabs, wtabs, lam_init):
    cache_a, cache_b, cache_c, cache_lf4, state_win = caches
    B, S8, D = x3.shape
    NP = page_table.shape[1]
    P = NP * PAGE
    pj = _inproj(x3.reshape(B * S8, D), wp, do_cmp=False)
    r3 = lambda a: a.reshape(B, S8, a.shape[-1])
    PG = min(DECODE_PAGES, NP)
    n_near = stabs["n_near_pages"] // PG if stabs["n_near_pages"] >= PG else 1
    split_steps = lambda a: jnp.swapaxes(a.reshape(a.shape[0], -1, PG * PAGE), 0, 1)
    bcast = lambda a: jnp.broadcast_to(a[None], (B,) + a.shape)
    pool_maps = [functools.partial(lambda b, j, pt, i: (l, pt[b, j * PG + i], 0), i=i) for i in range(PG)]

    qa_rows = _rows_from_lanes(r3(pj["qa"]), [h * HD + mp * DA for mp in range(2) for h in range(H_A)], DA)
    arow = r3(pj["arow"])
    near_a = split_steps(stabs["near_a"])
    o_a = _decode_call(
        "decode_a", cache_a, pool_maps, page_table, (512, 0), 0, 256, 256, 256, "a", qa_rows,
        _pad_rows(arow[..., 0:256], PAGE).astype(BF16), _pad_rows(arow[..., 256:512], PAGE).astype(BF16),
        bcast(stabs["new_a"]), [near_a, wp["lam"], wp["onorm4"]],
        [pl.BlockSpec(near_a.shape, lambda b, j, pt: (0, 0, 0)), pl.BlockSpec(wp["lam"].shape, lambda b, j, pt: (0, 0)),
         pl.BlockSpec(wp["onorm4"].shape, lambda b, j, pt: (0, 0))], n_near, 256, lam_init)

    lf_pages = _gather_logf(cache_lf4, l, page_table)
    lf4 = jnp.swapaxes(lf_pages.reshape(B, P, H_C), 1, 2).reshape(B, H_C, NP, PAGE)
    msu = jnp.asarray(np.tril(np.ones((LANES, LANES)), -1), BF16)
    mpu = jnp.asarray(np.triu(np.ones((NP, NP)), 1), BF16)
    c_past = _cumsum_call(lf4, msu, mpu, -1.0).reshape(B, H_C, P)
    c_past = jnp.pad(c_past, ((0, 0), (0, 8 - H_C), (0, 0)))
    logf = pj["misc"][:, 18:18 + H_C].reshape(B, S8, H_C)
    lfn = jnp.pad(jnp.swapaxes(logf, 1, 2), ((0, 0), (0, 8 - H_C), (0, LANES - S8)))[:, None]
    mw = jnp.asarray(np.triu(np.ones((LANES, LANES))), BF16)
    c_new = _cumsum_call(lfn, mw, jnp.zeros((8, 8), BF16), 1.0)[:, 0]
    cq_rows = c_new[:, :H_C, :S8].reshape(B, H_C * S8, 1)
    qc_rows = _rows_from_lanes(r3(pj["qc"]), [h * HD for h in range(H_C)], HD)
    crow = r3(pj["crow"])
    jj = np.arange(PAGE)
    tq = np.tile(np.arange(8), H_C)
    causal_new = jnp.asarray(np.where((jj[None, :] <= tq[:, None]) & (jj[None, :] < 8), 0.0, NEG), F32)
    o_c = _decode_call(
        "decode_c", cache_c, pool_maps, page_table, (768, 0), 0, 384, 384, 384, "c", qc_rows,
        _pad_rows(crow[..., 0:384], PAGE).astype(BF16), _pad_rows(crow[..., 384:768], PAGE).astype(BF16),
        bcast(causal_new), [c_past, c_new, cq_rows],
        [pl.BlockSpec((None, 8, PG * PAGE), lambda b, j, pt: (b, 0, j)),
         pl.BlockSpec((None, 8, LANES), lambda b, j, pt: (b, 0, 0)),
         pl.BlockSpec((None, H_C * S8, 1), lambda b, j, pt: (b, 0, 0))], 0, 384)

    gmask = jnp.asarray(np.arange(LANES)[None, :] // HD == np.arange(G_B)[:, None])
    qb5 = jnp.where(gmask[None, None, None], r3(pj["qb"]).reshape(B, S8, R_B, 1, LANES), jnp.zeros((), BF16))
    qb_rows = jnp.transpose(qb5, (0, 2, 3, 1, 4)).reshape(B, H_B * 8, LANES)
    kvb = r3(pj["kvb"])
    kvb_pad = _pad_rows(kvb, PAGE)
    o_bc_rows, selm1 = _scmp_call(cache_b, l, page_table, wp["cmpw_s"], kvb_pad[..., 0:256], qb_rows, P)
    rowi = np.arange(H_B * 8)
    lanei = np.arange(LANES)
    keep = jnp.asarray(((rowi[:, None] >> 3) & 1) == (lanei[None, :] >> 6))
    x = jnp.where(keep[None], o_bc_rows, 0.0).reshape(B, R_B, G_B, 8, LANES)
    o_bc = jnp.transpose(x[:, :, 0] + x[:, :, 1], (0, 2, 1, 3)).reshape(B, 8, R_B * LANES)
    selrows = jnp.broadcast_to(selm1[:, None], (B, R_B, G_B, 8, selm1.shape[-1])).reshape(B, H_B * 8, -1)
    nbs = 2 * PG
    sel_steps = jnp.swapaxes(selrows[..., :2 * NP].reshape(B, H_B * 8, NP // PG, nbs), 1, 2)
    sel_steps = jnp.pad(sel_steps, ((0, 0), (0, 0), (0, 0), (0, LANES - nbs))).astype(BF16)
    hot = (np.arange(LANES)[:, None] == np.arange(PG * PAGE)[None, :] // CMP_BLOCK) * 1e30
    hot = jnp.asarray(hot, BF16)
    amask_new = jnp.broadcast_to(selrows[..., 2 * NP:2 * NP + 1], (B, H_B * 8, PAGE)) * 1e30
    near_b = split_steps(stabs["near_b"])
    o_bs = _decode_call(
        "decode_bsel", cache_b, pool_maps, page_table, (256, 1), 0, 128, 128, 128, "bs", qb_rows,
        kvb_pad[..., 256:384].astype(BF16), kvb_pad[..., 384:512].astype(BF16),
        stabs["new_b"][None] + amask_new, [near_b, sel_steps, hot],
        [pl.BlockSpec(near_b.shape, lambda b, j, pt: (0, 0, 0)),
         pl.BlockSpec((None, None, H_B * 8, LANES), lambda b, j, pt: (b, j, 0, 0)),
         pl.BlockSpec(hot.shape, lambda b, j, pt: (0, 0))], n_near, 384)
    wb = state_win.shape[2]
    npw = wb // PAGE
    win_t = jnp.transpose(state_win, (0, 1, 3, 4, 5, 2)).reshape(state_win.shape[0], B, 2 * G_B * HD, wb)
    pt_w = jnp.zeros((B, npw), jnp.int32)
    win_maps = [functools.partial(lambda b, j, pt, i: (l, b, i), i=i) for i in range(npw)]
    win = r3(pj["win"])
    win_pad = _pad_rows(win, PAGE)
    near_w = wtabs["near_w"][None]
    o_bw = _decode_call(
        "decode_bwin", win_t, win_maps, pt_w, (256, 0), 0, 128, 128, 128, "bw", qb_rows,
        win_pad[..., 0:128].astype(BF16), win_pad[..., 128:256].astype(BF16),
        bcast(wtabs["new_w"]), [near_w], [pl.BlockSpec(near_w.shape, lambda b, j, pt: (0, 0, 0))], 1, 384)

    f2 = lambda a: a.reshape(B * S8, a.shape[-1])
    y = _merge_call(x3.reshape(B * S8, D), f2(o_a), f2(o_bc), f2(o_bs), f2(o_bw), f2(o_c), pj["misc"], pj["zs"],
                    wp["w_out"], wp["eg"])
    win_all = jnp.concatenate([state_win[l].reshape(B, wb, 2 * G_B * HD), win], axis=1)[:, -wb:]
    return (y.reshape(B, S8, D), arow.reshape(B, S8, 2, H_A, HD), kvb.reshape(B, S8, 4, G_B, HD),
            win_all.reshape(B, wb, 2, G_B, HD), crow.reshape(B, S8, 2, H_C, HD), logf)


def _window_tables(stabs, wb):
    tab_w = stabs["tab_w"]
    assert tab_w.shape[1] >= wb + 8
    rel = wb + np.arange(8)[:, None] - np.arange(wb)[None, :]
    near_w = jnp.where(jnp.asarray(rel < WINDOW)[None], _shifted_rows(tab_w, wb, 8), NEG)
    new_w = _new_rows(tab_w)
    flat = lambda a: a.reshape(a.shape[0] * a.shape[1], a.shape[2])
    return dict(near_w=flat(near_w), new_w=flat(new_w))


def kernel(x_prompt, x_sample, cache_a_kv, cache_b_kv, cache_c_kv, cache_c_logf, state_b_win, page_table,
           rel_bias, norm_g, w_in, w_out, qk_a, qk_b, qk_c, onorm_a, lam_a, cmp_w, b_f):
    depth = w_in.shape[0]
    B, S, _ = x_prompt.shape
    DB, S8, _ = x_sample.shape
    assert S8 == 8 and S % 512 == 0 and state_b_win.shape[2] == WINDOW
    T = 512
    NP = page_table.shape[1]
    P = NP * PAGE
    n_pool = cache_a_kv.shape[1]
    ptabs = _prompt_tables(rel_bias, T, S)
    stabs = _sample_tables(rel_bias, P, min(DECODE_PAGES, NP))
    wtabs = _window_tables(stabs, state_b_win.shape[2])
    fmajor = lambda c: jnp.transpose(c, (0, 1, 3, 4, 5, 2)).reshape(depth, n_pool, -1, PAGE)
    caches = (fmajor(cache_a_kv), fmajor(cache_b_kv), fmajor(cache_c_kv),
              cache_c_logf.reshape(depth, n_pool, 1, PAGE * H_C), state_b_win)
    page_table = page_table.astype(jnp.int32)
    yp, ys = x_prompt, x_sample
    outs = [[] for _ in range(10)]
    for l in range(depth):
        lam_init = 0.8 - 0.6 * math.exp(-0.3 * l)
        wp = _prep_layer(l, norm_g, w_in, w_out, qk_a, qk_b, qk_c, onorm_a, lam_a, cmp_w, b_f)
        yp, a_r, b_r, w_r, c_r, l_r = _prompt_layer(yp, wp, ptabs, lam_init, T)
        for k, v in zip((0, 2, 4, 6, 8), (a_r, b_r, w_r, c_r, l_r)):
            outs[k].append(v)
        ys, a_r, b_r, w_r, c_r, l_r = _sample_layer(l, ys, caches, page_table, wp, stabs, wtabs, lam_init)
        for k, v in zip((1, 3, 5, 7, 9), (a_r, b_r, w_r, c_r, l_r)):
            outs[k].append(v)
    return (yp, ys) + tuple(jnp.stack(o) for o in outs)
```

```python
import functools
import math

import numpy as np
import jax
import jax.numpy as jnp
from jax import lax
from jax.experimental import pallas as pl
from jax.experimental.pallas import tpu as pltpu

F32 = jnp.float32
BF16 = jnp.bfloat16

HD = 64
H_A = 4
DA = HD // 2
H_B = 6
G_B = 2
R_B = H_B // G_B
H_C = 6
CMP_BLOCK = 64
TOPK = 16
WINDOW = 512
FORCE_BONUS = float(R_B + 1)
N_BUCKETS = 32
REL_MAX_DIST = 1024
EPS = 1e-6
PAGE = 128

LOG2E = 1.4426950408889634
NEG = -1e30
M_INIT = -0.5e30
LANES = 128
VMEM_LIMIT = 56 * 1024 * 1024
DECODE_PAGES = 32

O_QB, O_KVB, O_WIN, O_GB, O_ZB, O_QC, O_FC, O_ZC = 1024, 1408, 1920, 2176, 2194, 2578, 3730, 3736
PROJ_W = 4120
P_QB, P_KVB, P_WIN, P_ZB, P_QC, P_ZC, P_MISC, P_END = 1024, 1408, 1920, 2176, 2560, 3712, 4096, 4224
SLAB_HEADS = [g * R_B + r for r in range(R_B) for g in range(G_B)]


def _cparams(n_grid):
    return pltpu.CompilerParams(dimension_semantics=("arbitrary",) * n_grid, vmem_limit_bytes=VMEM_LIMIT)


def _dot(a, b):
    return jnp.dot(a, b, preferred_element_type=F32)


def _dot_nt(a, b):
    return lax.dot_general(a, b, (((1,), (1,)), ((), ())), preferred_element_type=F32)


def _split3(x):
    x1 = x.astype(BF16)
    r = x - x1.astype(F32)
    x2 = r.astype(BF16)
    x3 = (r - x2.astype(F32)).astype(BF16)
    return x1, x2, x3


def _dot3(x, m):
    x1, x2, x3 = _split3(x)
    return _dot(x1, m) + _dot(x2, m) + _dot(x3, m)


def _dot3_l(m, x):
    x1, x2, x3 = _split3(x)
    return _dot(m, x1) + _dot(m, x2) + _dot(m, x3)


def _rel_bucket(rel):
    n = jnp.maximum(rel, 0)
    max_exact = N_BUCKETS // 2
    nf = jnp.maximum(n, 1).astype(F32)
    large = max_exact + (jnp.log(nf / max_exact) / math.log(REL_MAX_DIST / max_exact)
                         * (N_BUCKETS - max_exact)).astype(jnp.int32)
    return jnp.where(n < max_exact, n, jnp.minimum(large, N_BUCKETS - 1))


def _inproj_kernel(x_ref, w_ref, g_in, ga_q, ga_k, gb_q, gb_k, gc_q, gc_k, bf_row, cmpw_ref, bd32_ref, bd64_ref,
                   qa_o, arow_o, ka_o, va_o, qb_o, kvb_o, ksel_o, vsel_o, kcmp_o, vcmp_o, win_o, kwin_o, vwin_o,
                   qc_o, crow_o, kc_o, vc_o, zs_o, misc_o, *, tm, do_cmp, fmajor):
    x = x_ref[...]
    ms = jnp.mean(x * x, axis=-1, keepdims=True)
    h = (x * lax.rsqrt(ms + EPS) * g_in[...]).astype(BF16)

    def put_wide(ref, y, fill):
        for j in range(y.shape[1] // LANES):
            ref[:, 2 * j * LANES:(2 * j + 1) * LANES] = y[:, j * LANES:(j + 1) * LANES].astype(BF16)
            ref[:, (2 * j + 1) * LANES:(2 * j + 2) * LANES] = jnp.full((y.shape[0], LANES), fill, BF16)

    def put_values(ref, y):
        put_wide(ref, y, 1.0)

    def put_rows(ref, c0, y):
        if fmajor:
            ref[c0:c0 + y.shape[1], :] = y.T
        else:
            ref[:, c0:c0 + y.shape[1]] = y

    def proj(c0, c1):
        return _dot(h, w_ref[:, c0:c1])

    def segnorm(y, bd, seg):
        ss = _dot((y * y).astype(BF16), bd)
        return y * lax.rsqrt(ss * (1.0 / seg) + EPS)

    def silu(y):
        return y * (1.0 / (1.0 + jnp.exp(-y)))

    bd32 = bd32_ref[...]
    bd64 = bd64_ref[...]
    bd64_1 = bd64[0:128, 0:128]

    y = proj(0, 256)
    qa_o[...] = (segnorm(y, bd32, DA) * ga_q[...]).astype(BF16)
    y = segnorm(proj(256, 512), bd32, DA) * ga_k[...]
    put_rows(arow_o, 0, y)
    ka_o[...] = y.astype(BF16)
    y = proj(512, 768)
    put_rows(arow_o, 256, y)
    put_values(va_o, y)
    zs_o[:, 0:256] = silu(proj(768, 1024))

    qb_o[...] = (segnorm(proj(P_QB, P_KVB), bd64, HD) * gb_q[...]).astype(BF16)
    kcn = segnorm(proj(P_KVB, P_KVB + 128), bd64_1, HD) * gb_k[...]
    put_rows(kvb_o, 0, kcn)
    vcr = proj(P_KVB + 128, P_KVB + 256)
    put_rows(kvb_o, 128, vcr)
    y = segnorm(proj(P_KVB + 256, P_KVB + 384), bd64_1, HD) * gb_k[...]
    put_rows(kvb_o, 256, y)
    ksel_o[...] = y.astype(BF16)
    y = proj(P_KVB + 384, P_KVB + 512)
    put_rows(kvb_o, 384, y)
    put_values(vsel_o, y)
    if do_cmp:
        nbt = tm // CMP_BLOCK
        rowi = lax.broadcasted_iota(jnp.int32, (nbt, tm), 0)
        coli = lax.broadcasted_iota(jnp.int32, (nbt, tm), 1)
        blk = (coli >> 6) == rowi

        def cmp_weights(wl):
            e = jnp.where(blk, jnp.exp(wl - jnp.max(wl, axis=-1, keepdims=True)), 0.0)
            return (e / jnp.sum(e, axis=-1, keepdims=True)).astype(BF16)

        kcmp_o[...] = _dot(cmp_weights(cmpw_ref[0]), kcn.astype(BF16))
        vcmp_o[...] = _dot(cmp_weights(cmpw_ref[1]), vcr.astype(BF16))
    else:
        kcmp_o[...] = jnp.zeros(kcmp_o.shape, F32)
        vcmp_o[...] = jnp.zeros(vcmp_o.shape, F32)
    y = segnorm(proj(P_WIN, P_WIN + 128), bd64_1, HD) * gb_k[...]
    win_o[:, 0:128] = y
    kwin_o[...] = y.astype(BF16)
    y = proj(P_WIN + 128, P_WIN + 256)
    win_o[:, 128:256] = y
    put_values(vwin_o, y)
    zs_o[:, 256:640] = silu(proj(P_ZB, P_QC))

    qc_o[...] = (segnorm(proj(P_QC, P_QC + 384), bd64, HD) * gc_q[...]).astype(BF16)
    y = segnorm(proj(P_QC + 384, P_QC + 768), bd64, HD) * gc_k[...]
    put_rows(crow_o, 0, y)
    kc_o[...] = y.astype(BF16)
    y = proj(P_QC + 768, P_ZC)
    put_rows(crow_o, 384, y)
    put_values(vc_o, y)
    zs_o[:, 640:1024] = silu(proj(P_ZC, P_MISC))

    y = proj(P_MISC, P_END) + bf_row[...]
    lane = lax.broadcasted_iota(jnp.int32, y.shape, 1)
    sig = 1.0 / (1.0 + jnp.exp(-y))
    lsg = jnp.minimum(y, 0.0) - jnp.log(1.0 + jnp.exp(-jnp.abs(y)))
    misc_o[...] = jnp.where(lane < 18, sig, lsg)


def _inproj(x2, wp, *, do_cmp, seq=None):
    T = x2.shape[0]
    tm = min(512, T)
    assert T % tm == 0 and (tm == 512 or not do_cmp)
    nbt = 8
    n = T // tm
    fmajor = seq is not None
    if fmajor:
        assert seq[0] * seq[1] == T and seq[1] % tm == 0
        nsb = seq[1] // tm
    full = lambda a: pl.BlockSpec(a.shape, lambda i: (0,) * a.ndim)
    row = lambda w: pl.BlockSpec((tm, w), lambda i: (i, 0))
    consts = [wp["w_in"], wp["g_in"], wp["ga_q"], wp["ga_k"], wp["gb_q"], wp["gb_k"], wp["gc_q"], wp["gc_k"],
              wp["bf_row"], wp["cmpw_in"], wp["bd32"], wp["bd64"]]
    outs = [("qa", 256, BF16), ("arow", 512, F32), ("ka", 256, BF16), ("va", 512, BF16), ("qb", 384, BF16),
            ("kvb", 512, F32), ("ksel", 128, BF16), ("vsel", 256, BF16), ("kcmp", None, F32), ("vcmp", None, F32),
            ("win", 256, F32), ("kwin", 128, BF16), ("vwin", 256, BF16), ("qc", 384, BF16), ("crow", 768, F32),
            ("kc", 384, BF16), ("vc", 768, BF16), ("zs", 1024, F32), ("misc", 128, F32)]
    out_shape, out_specs = [], []
    for name, w, dt in outs:
        if w is None:
            out_shape.append(jax.ShapeDtypeStruct((n * nbt, 128), dt))
            out_specs.append(pl.BlockSpec((nbt, 128), lambda i: (i, 0)))
        elif fmajor and name in ("arow", "kvb", "crow"):
            out_shape.append(jax.ShapeDtypeStruct((seq[0], w, seq[1]), dt))
            out_specs.append(pl.BlockSpec((None, w, tm), lambda i: (i // nsb, 0, i % nsb)))
        else:
            out_shape.append(jax.ShapeDtypeStruct((T, w), dt))
            out_specs.append(row(w))
    res = pl.pallas_call(
        functools.partial(_inproj_kernel, tm=tm, do_cmp=do_cmp, fmajor=fmajor),
        grid=(n,),
        in_specs=[row(1024)] + [full(c) for c in consts],
        out_specs=out_specs, out_shape=out_shape,
        compiler_params=_cparams(1), name="inproj",
    )(x2, *consts)
    return {k: v for (k, _, _), v in zip(outs, res)}


def _attend(lhs_list, k_ref, v_ref, T, lo, hi, add):
    n = len(lhs_list)
    last = v_ref.shape[0] // T - 1

    def step(kt, state):
        ks = pl.multiple_of(jnp.minimum(kt, last) * T, T)
        k_refs = k_ref if isinstance(k_ref, (tuple, list)) else (k_ref,)
        kk = jnp.concatenate([r[pl.ds(ks, T), :] for r in k_refs], axis=1) if len(k_refs) > 1 else k_ref[pl.ds(ks, T), :]
        vv = v_ref[pl.ds(ks, T), :]
        out = []
        for i in range(n):
            m, acc = state[i]
            s = add[i](_dot_nt(lhs_list[i], kk), kt)
            m_new = jnp.maximum(m, jnp.max(s, axis=-1, keepdims=True))
            p = jnp.exp2((s - m_new).astype(BF16))
            alpha = jnp.exp2(m - m_new)
            acc = alpha * acc + _dot(p, vv)
            out.append((m_new, acc))
        return tuple(out)

    one = (jnp.full((T, 1), M_INIT, F32), jnp.zeros((T, v_ref.shape[-1]), F32))
    state = lax.fori_loop(0, (hi - lo + 1) // 2,
                          lambda j, c: step(lo + 2 * j + 1, step(lo + 2 * j, c)), (one,) * n)
    return [acc[:, :LANES] / acc[:, LANES:] for _, acc in state]


def _tile_index(qi, kt, ntiles):
    return jnp.clip(qi - kt + 1, 0, ntiles - 1)


def _lane_band(x, lo, width):
    lane = lax.broadcasted_iota(jnp.int32, x.shape, 1)
    return jnp.where((lane >= lo) & (lane < lo + width), x, jnp.zeros_like(x))


def _halves(a, b):
    lane = lax.broadcasted_iota(jnp.int32, a.shape, 1)
    return jnp.where(lane < HD, a, b)


def _diff_lambda(lam_ref, lam_init):
    lf = lam_ref[...]
    a = jnp.sum(lf[0:1] * lf[1:2], axis=-1, keepdims=True)
    b = jnp.sum(lf[2:3] * lf[3:4], axis=-1, keepdims=True)
    return jnp.exp(a) - jnp.exp(b) + lam_init


def _flash_a_kernel(q_ref, k_ref, v_ref, bias_ref, lam_ref, onorm_ref, o_ref, *, T, noff, lam_init):
    qi = pl.program_id(2)
    q = q_ref[...]
    lam = _diff_lambda(lam_ref, lam_init)
    lhs = [_lane_band(q, hl * HD + mp * DA, DA) for hl in range(2) for mp in range(2)]
    add = [functools.partial(lambda s, kt, hl: s + bias_ref[hl, _tile_index(qi, kt, noff + 2)], hl=i // 2) for i in range(4)]
    o = _attend(lhs, k_ref, v_ref, T, 0, qi + 1, add)
    o = _halves(o[0] - lam * o[1], o[2] - lam * o[3])
    x2 = o * o
    lane = lax.broadcasted_iota(jnp.int32, o.shape, 1)
    s0 = jnp.sum(jnp.where(lane < HD, x2, 0.0), axis=-1, keepdims=True)
    s1 = jnp.sum(jnp.where(lane < HD, 0.0, x2), axis=-1, keepdims=True)
    ss = jnp.where(lane < HD, s0, s1)
    o_ref[...] = o * lax.rsqrt(ss * (1.0 / HD) + EPS) * onorm_ref[...] * (1.0 - lam_init)


def _flash_c_kernel(q_ref, qx_ref, k_ref, kx_ref, v_ref, mask_ref, o_ref, *, T):
    qi = pl.program_id(2)
    q = q_ref[...]
    qx = qx_ref[...]
    lane = lax.broadcasted_iota(jnp.int32, qx.shape, 1)
    lhs = []
    for hl in range(2):
        own = ((lane >= 3 * hl) & (lane < 3 * hl + 3)) | ((lane >= 6 + 3 * hl) & (lane < 9 + 3 * hl))
        lhs.append(jnp.concatenate([_lane_band(q, hl * HD, HD), jnp.where(own, qx, jnp.zeros_like(qx))], axis=1))
    causal = [lambda s, kt: s + mask_ref[_tile_index(qi, kt, 3)]] * 2
    o = _attend(lhs, (k_ref, kx_ref), v_ref, T, 0, qi + 1, causal)
    o_ref[...] = _halves(o[0], o[1])


def _flash_b_kernel(*refs, T, noff, selected):
    if selected:
        q_ref, sel_ref, ksel_ref, hot_ref, v_ref, bias_ref, o_ref = refs
        k_ref = (ksel_ref, hot_ref)
    else:
        q_ref, k_ref, v_ref, bias_ref, o_ref = refs
    qi = pl.program_id(2)
    lo = 0 if selected else jnp.maximum(qi - (noff - 1), 0)
    q = q_ref[...]
    lhs = [_lane_band(q, g * HD, HD) for g in range(G_B)]
    if selected:
        nbp = sel_ref.shape[-1] // G_B
        lhs = [jnp.concatenate([lhs[g], sel_ref[:, g * nbp:(g + 1) * nbp]], axis=1) for g in range(G_B)]
    add = [functools.partial(lambda s, kt, g: s + bias_ref[g, _tile_index(qi, kt, noff + 2)], g=g) for g in range(G_B)]
    o = _attend(lhs, k_ref, v_ref, T, lo, qi + 1, add)
    o_ref[...] = _halves(o[0], o[1])


def _flash_call(kernel, name, B, S, T, n_grp, ins, in_specs, out_w):
    return pl.pallas_call(
        kernel, grid=(B, n_grp, S // T), in_specs=in_specs,
        out_specs=pl.BlockSpec((None, T, LANES), lambda b, g, i: (b, i, g)),
        out_shape=jax.ShapeDtypeStruct((B, S, out_w), F32),
        compiler_params=_cparams(3), name=name,
    )(*ins)


def _qspec(T):
    return pl.BlockSpec((None, T, LANES), lambda b, g, i: (b, i, g))


def _kvspec(S, w, per_group):
    if per_group:
        return pl.BlockSpec((None, S, w), lambda b, g, i: (b, 0, g))
    return pl.BlockSpec((None, S, w), lambda b, g, i: (b, 0, 0))


def _biasspec(noff, T):
    return pl.BlockSpec((2, noff + 2, T, T), lambda b, g, i: (g, 0, 0, 0))


def _small(a):
    return pl.BlockSpec(a.shape, lambda b, g, i: (0,) * a.ndim)


def _masked_softmax2(s, mask, axis):
    s = jnp.where(mask, s, NEG)
    m = jnp.max(s, axis=axis, keepdims=True)
    p = jnp.where(mask, jnp.exp2(s - m), 0.0)
    return p / jnp.maximum(jnp.sum(p, axis=axis, keepdims=True), 1e-30)


def _topk_select(score, idx, n_idx, axis):
    def it(_, c):
        sc, sel = c
        mx = jnp.max(sc, axis=axis, keepdims=True)
        first = jnp.min(jnp.where(sc == mx, idx, n_idx), axis=axis, keepdims=True)
        oh = idx == first
        sel = jnp.where(oh, jnp.where(mx >= 0.0, 1.0, sel), sel)
        sc = jnp.where(oh, -2.0, sc)
        return sc, sel

    _, sel = lax.fori_loop(0, TOPK, it, (score, jnp.zeros(score.shape, F32)))
    return sel


def _cmp_kernel(q_ref, kc_ref, vc_ref, oc_ref, selt_ref, *, T, NBP):
    qi = pl.program_id(1)
    q = q_ref[...]
    kc = kc_ref[...]
    vc = vc_ref[...]
    qpos_r = qi * T + lax.broadcasted_iota(jnp.int32, (T, NBP), 0)
    blk_c = lax.broadcasted_iota(jnp.int32, (T, NBP), 1)
    cm = ((blk_c + 1) * CMP_BLOCK - 1) <= qpos_r
    qpos_c = qi * T + lax.broadcasted_iota(jnp.int32, (NBP, T), 1)
    blk_r = lax.broadcasted_iota(jnp.int32, (NBP, T), 0)
    cmt = ((blk_r + 1) * CMP_BLOCK - 1) <= qpos_c
    imp = [jnp.zeros((NBP, T), F32) for _ in range(G_B)]
    for r in range(R_B):
        slab = q[:, r * LANES:(r + 1) * LANES]
        outs = []
        for g in range(G_B):
            qm = _lane_band(slab, g * HD, HD)
            p = _masked_softmax2(_dot_nt(qm, kc), cm, -1)
            outs.append(_dot(p.astype(BF16), vc))
            imp[g] = imp[g] + _masked_softmax2(_dot_nt(kc, qm), cmt, 0)
        oc_ref[:, r * LANES:(r + 1) * LANES] = _halves(outs[0], outs[1])
    qblk = qpos_c >> 6
    forced = (blk_r == qblk) | (blk_r == qblk - 1) | (blk_r == 0)
    valid = blk_r * CMP_BLOCK <= qpos_c
    blk_q = lax.broadcasted_iota(jnp.int32, (NBP, LANES), 0)
    for g in range(G_B):
        score = jnp.where(valid, imp[g] + jnp.where(forced, FORCE_BONUS, 0.0), -1.0)
        for c in range(T // LANES):
            cs = slice(c * LANES, (c + 1) * LANES)
            selt_ref[g, :, cs] = _topk_select(score[:, cs], blk_q, NBP, 0) - 1.0


def _cmp_call(qb3, kcmp3, vcmp3, T):
    B, S, _ = qb3.shape
    NBP = kcmp3.shape[1]
    return pl.pallas_call(
        functools.partial(_cmp_kernel, T=T, NBP=NBP), grid=(B, S // T),
        in_specs=[pl.BlockSpec((None, T, 384), lambda b, i: (b, i, 0)),
                  pl.BlockSpec((None, NBP, LANES), lambda b, i: (b, 0, 0)),
                  pl.BlockSpec((None, NBP, LANES), lambda b, i: (b, 0, 0))],
        out_specs=[pl.BlockSpec((None, T, 384), lambda b, i: (b, i, 0)),
                   pl.BlockSpec((None, G_B, NBP, T), lambda b, i: (b, 0, 0, i))],
        out_shape=[jax.ShapeDtypeStruct((B, S, 384), F32), jax.ShapeDtypeStruct((B, G_B, NBP, S), F32)],
        compiler_params=_cparams(2), name="nsa_cmp_topk",
    )(qb3, kcmp3, vcmp3)


def _cumsum_kernel(x_ref, mw_ref, mr_ref, *o_refs, nh, sign, split):
    mw = mw_ref[...]
    mr = mr_ref[...]
    ones = jnp.ones((LANES, LANES), BF16)
    for h in range(nh):
        x = x_ref[h]
        tot = _dot3(x, ones)
        c = (_dot3(x, mw) + _dot3_l(mr, tot)) * (sign * LOG2E)
        if split:
            for o_ref, part in zip(o_refs, _split3(c)):
                o_ref[h] = part
        else:
            o_refs[0][h] = c


def _cumsum_call(x4, mw, mr, sign, split=False):
    B, nh, nr, _ = x4.shape
    spec = pl.BlockSpec((None, nh, nr, LANES), lambda b: (b, 0, 0, 0))
    return pl.pallas_call(
        functools.partial(_cumsum_kernel, nh=nh, sign=sign, split=split), grid=(B,),
        in_specs=[spec, pl.BlockSpec(mw.shape, lambda b: (0, 0)), pl.BlockSpec(mr.shape, lambda b: (0, 0))],
        out_specs=[spec] * 3 if split else spec,
        out_shape=[jax.ShapeDtypeStruct(x4.shape, BF16)] * 3 if split else jax.ShapeDtypeStruct(x4.shape, F32),
        compiler_params=_cparams(1), name="logf_cumsum",
    )(x4, mw, mr)


def _merge_kernel(x_ref, oa_ref, obc_ref, obs_ref, obw_ref, oc_ref, misc_ref, zs_ref, w_ref, eg_ref, y_ref):
    g = misc_ref[...]
    g1 = g.astype(BF16)
    g2 = (g - g1.astype(F32)).astype(BF16)

    def gate(c):
        return _dot(g1, eg_ref[c]) + _dot(g2, eg_ref[c])

    ob = gate(0) * obc_ref[...] + gate(1) * obs_ref[...] + gate(2) * obw_ref[...]
    ma = (oa_ref[...] * zs_ref[:, 0:256]).astype(BF16)
    mb = (ob * zs_ref[:, 256:640]).astype(BF16)
    mc = (oc_ref[...] * zs_ref[:, 640:1024]).astype(BF16)
    y_ref[...] = x_ref[...] + _dot(ma, w_ref[0:256, :]) + _dot(mb, w_ref[256:640, :]) + _dot(mc, w_ref[640:1024, :])


def _merge_call(x2, oa, obc, obs, obw, oc, misc, zs, w_out, eg):
    T = x2.shape[0]
    tm = min(512, T)
    row = lambda w: pl.BlockSpec((tm, w), lambda i: (i, 0))
    return pl.pallas_call(
        _merge_kernel, grid=(T // tm,),
        in_specs=[row(1024), row(256), row(384), row(384), row(384), row(384), row(128), row(1024),
                  pl.BlockSpec(w_out.shape, lambda i: (0, 0)), pl.BlockSpec(eg.shape, lambda i: (0, 0, 0))],
        out_specs=row(1024), out_shape=jax.ShapeDtypeStruct((T, 1024), F32),
        compiler_params=_cparams(1), name="merge_outproj",
    )(x2, oa, obc, obs, obw, oc, misc, zs, w_out, eg)


def _gather_rows_kernel(pt_ref, *refs, n):
    out = refs[n]
    for i in range(n):
        out[i] = refs[i][...]


def _gather_logf(cache4, layer, page_table):
    B, NP = page_table.shape
    n = min(16, NP)
    assert NP % n == 0
    w = cache4.shape[-1]
    in_specs = [pl.BlockSpec((None, None, 1, w), functools.partial(
        lambda b, j, pt, i: (layer, pt[b, j * n + i], 0, 0), i=i)) for i in range(n)]
    return pl.pallas_call(
        functools.partial(_gather_rows_kernel, n=n),
        grid_spec=pltpu.PrefetchScalarGridSpec(
            num_scalar_prefetch=1, grid=(B, NP // n), in_specs=in_specs,
            out_specs=pl.BlockSpec((None, n, 1, w), lambda b, j, pt: (b, j, 0, 0))),
        out_shape=jax.ShapeDtypeStruct((B, NP, 1, w), F32),
        compiler_params=_cparams(2), name="gather_logf",
    )(page_table, *([cache4] * n))


def _scmp_kernel(pt_ref, *refs, PG, NP, NBP, P):
    pages = refs[:PG]
    cmpw_ref, new_ref, qb_ref, oc_ref, sel_ref, kcs, vcs = refs[PG:]
    jc = pl.program_id(1)
    ns = pl.num_programs(1)
    nrow = 2 * PG

    @pl.when(jc == 0)
    def _():
        kcs[...] = jnp.zeros(kcs.shape, F32)
        vcs[...] = jnp.zeros(vcs.shape, F32)

    rowi = lax.broadcasted_iota(jnp.int32, (nrow, PAGE), 0)
    coli = lax.broadcasted_iota(jnp.int32, (nrow, PAGE), 1)

    def half_softmax(wl):
        e = jnp.exp(wl - jnp.max(wl, axis=-1, keepdims=True))
        return e / (0.5 * jnp.sum(e, axis=-1, keepdims=True))

    wk = half_softmax(cmpw_ref[0])
    wv = half_softmax(cmpw_ref[1])
    kacc = jnp.zeros((nrow, LANES), F32)
    vacc = jnp.zeros((nrow, LANES), F32)
    for i in range(PG):
        sel = (coli >> 6) + 2 * i == rowi
        pg = pages[i][...]
        kacc = kacc + _dot_nt(jnp.where(sel, wk, 0.0).astype(BF16), pg[0:128, :].astype(BF16))
        vacc = vacc + _dot_nt(jnp.where(sel, wv, 0.0).astype(BF16), pg[128:256, :].astype(BF16))
    r0 = pl.multiple_of(jc * nrow, nrow)
    kcs[pl.ds(r0, nrow), :] = kacc
    vcs[pl.ds(r0, nrow), :] = vacc

    @pl.when(jc == ns - 1)
    def _():
        first = (rowi == 0) & (coli < CMP_BLOCK)
        new = new_ref[...]
        kcs[2 * NP:2 * NP + nrow, :] = _dot(jnp.where(first, wk, 0.0).astype(BF16), new[:, 0:128].astype(BF16))
        vcs[2 * NP:2 * NP + nrow, :] = _dot(jnp.where(first, wv, 0.0).astype(BF16), new[:, 128:256].astype(BF16))
        kc = kcs[...].astype(BF16)
        vc = vcs[...].astype(BF16)
        R = qb_ref.shape[0]
        s = _dot_nt(qb_ref[...], kc)
        t_r = lax.broadcasted_iota(jnp.int32, (R, NBP), 0) & 7
        blk = lax.broadcasted_iota(jnp.int32, (R, NBP), 1)
        cm = ((blk + 1) * CMP_BLOCK - 1) <= (P + t_r)
        p = _masked_softmax2(s, cm, -1)
        oc_ref[...] = _dot(p.astype(BF16), vc)
        qpos = P + lax.broadcasted_iota(jnp.int32, (8, NBP), 0)
        blk8 = lax.broadcasted_iota(jnp.int32, (8, NBP), 1)
        qblk = qpos >> 6
        forced = (blk8 == qblk) | (blk8 == qblk - 1) | (blk8 == 0)
        valid = blk8 * CMP_BLOCK <= qpos
        for g in range(G_B):
            imp = sum(p[(r * G_B + g) * 8:(r * G_B + g) * 8 + 8] for r in range(R_B))
            score = jnp.where(valid, imp + jnp.where(forced, FORCE_BONUS, 0.0), -1.0)
            sel_ref[g] = _topk_select(score, blk8, NBP, -1) - 1.0


def _scmp_call(cache_b, layer, page_table, cmpw_s, new_pad, qb_rows, P):
    B, NP = page_table.shape
    PG = min(DECODE_PAGES, NP)
    assert NP % PG == 0
    NBP = -(-(2 * NP + 2 * PG) // LANES) * LANES
    in_specs = [pl.BlockSpec((None, None, 256, PAGE), functools.partial(
        lambda b, j, pt, i: (layer, pt[b, j * PG + i], 0, 0), i=i)) for i in range(PG)]
    in_specs += [pl.BlockSpec(cmpw_s.shape, lambda b, j, pt: (0, 0, 0)),
                 pl.BlockSpec((None, PAGE, 256), lambda b, j, pt: (b, 0, 0)),
                 pl.BlockSpec((None,) + qb_rows.shape[1:], lambda b, j, pt: (b, 0, 0))]
    R = qb_rows.shape[1]
    return pl.pallas_call(
        functools.partial(_scmp_kernel, PG=PG, NP=NP, NBP=NBP, P=P),
        grid_spec=pltpu.PrefetchScalarGridSpec(
            num_scalar_prefetch=1, grid=(B, NP // PG), in_specs=in_specs,
            out_specs=[pl.BlockSpec((None, R, LANES), lambda b, j, pt: (b, 0, 0)),
                       pl.BlockSpec((None, G_B, 8, NBP), lambda b, j, pt: (b, 0, 0, 0))],
            scratch_shapes=[pltpu.VMEM((NBP, LANES), F32), pltpu.VMEM((NBP, LANES), F32)]),
        out_shape=[jax.ShapeDtypeStruct((B, R, LANES), F32), jax.ShapeDtypeStruct((B, G_B, 8, NBP), F32)],
        compiler_params=_cparams(2), name="sample_cmp_topk",
    )(page_table, *([cache_b] * PG), cmpw_s, new_pad, qb_rows)


def _decode_kernel(pt_ref, *refs, PG, kc0, KD, vc0, VD, mode, n_near, lam_init):
    pages = refs[:PG]
    rest = list(refs[PG:])
    qb_ref, knew_ref, vnew_ref, addnew_ref = rest[:4]
    rest = rest[4:]
    bias_ref = ck_ref = cnew_ref = cq_ref = amask_ref = lam_ref = onorm_ref = None
    if mode == "a":
        bias_ref, lam_ref, onorm_ref = rest[:3]
        rest = rest[3:]
    elif mode == "c":
        ck_ref, cnew_ref, cq_ref = rest[:3]
        rest = rest[3:]
    elif mode == "bs":
        bias_ref, amask_ref, hot_ref = rest[:3]
        rest = rest[3:]
    else:
        bias_ref = rest[0]
        rest = rest[1:]
    o_ref, m_s, l_s, acc_s = rest
    jc = pl.program_id(1)
    ns = pl.num_programs(1)
    R = qb_ref.shape[0]

    @pl.when(jc == 0)
    def _():
        m_s[...] = jnp.full(m_s.shape, M_INIT, F32)
        l_s[...] = jnp.zeros(l_s.shape, F32)
        acc_s[...] = jnp.zeros(acc_s.shape, F32)

    def rows_of_heads(c):
        return jnp.concatenate([jnp.broadcast_to(c[h:h + 1, :], (8, c.shape[1])) for h in range(R // 8)], axis=0)

    def update(s, v, v_feature_major):
        m = m_s[...]
        m_new = jnp.maximum(m, jnp.max(s, axis=-1, keepdims=True))
        p = jnp.exp2(s - m_new)
        alpha = jnp.exp2(m - m_new)
        l_s[...] = alpha * l_s[...] + jnp.sum(p, axis=-1, keepdims=True)
        pb = p.astype(BF16)
        acc_s[...] = alpha * acc_s[...] + (_dot_nt(pb, v) if v_feature_major else _dot(pb, v))
        m_s[...] = m_new

    q = qb_ref[...]
    kcat = jnp.concatenate([pg[kc0:kc0 + KD, :].astype(BF16) for pg in pages], axis=1)
    vcat = jnp.concatenate([pg[vc0:vc0 + VD, :].astype(BF16) for pg in pages], axis=1)
    s = _dot(q, kcat)
    if mode == "c":
        s = s + cq_ref[...] - rows_of_heads(ck_ref[...])
    if mode == "bs":
        s = s + _dot(amask_ref[...], hot_ref[...])
    if bias_ref is not None:
        e = jc - (ns - n_near)
        s = s + jnp.where(e >= 0, 1.0, 0.0) * bias_ref[jnp.maximum(e, 0)]
    update(s, vcat, True)

    @pl.when(jc == ns - 1)
    def _():
        s2 = _dot_nt(q, knew_ref[...]) + addnew_ref[...]
        if mode == "c":
            s2 = s2 + cq_ref[...] - rows_of_heads(cnew_ref[...])
        update(s2, vnew_ref[...], False)
        o = acc_s[...] / l_s[...]
        if mode == "a":
            lam = _diff_lambda(lam_ref, lam_init)
            half = R // 2
            pd = o[0:half] - lam * o[half:R]
            rowi = lax.broadcasted_iota(jnp.int32, pd.shape, 0)
            lanei = lax.broadcasted_iota(jnp.int32, pd.shape, 1)
            x = jnp.where((rowi >> 3) == (lanei >> 6), pd, 0.0)
            ss = jnp.sum(x * x, axis=-1, keepdims=True)
            y = x * lax.rsqrt(ss * (1.0 / HD) + EPS) * onorm_ref[...] * (1.0 - lam_init)
            o_ref[...] = sum(y[h * 8:(h + 1) * 8] for h in range(H_A))
        elif mode == "c":
            rowi = lax.broadcasted_iota(jnp.int32, o.shape, 0)
            lanei = lax.broadcasted_iota(jnp.int32, o.shape, 1)
            x = jnp.where((rowi >> 3) == (lanei >> 6), o, 0.0)
            o_ref[...] = sum(x[h * 8:(h + 1) * 8] for h in range(H_C))
        else:
            rowi = lax.broadcasted_iota(jnp.int32, o.shape, 0)
            lanei = lax.broadcasted_iota(jnp.int32, o.shape, 1)
            x = jnp.where(((rowi >> 3) & 1) == (lanei >> 6), o, 0.0)
            o_ref[...] = jnp.concatenate(
                [x[(2 * r) * 8:(2 * r) * 8 + 8] + x[(2 * r + 1) * 8:(2 * r + 1) * 8 + 8] for r in range(R_B)], axis=1)


def _decode_call(name, cache, page_maps, page_table, rowblk, kc0, KD, vc0, VD, mode, qb_rows, knew, vnew, addnew,
                 extras, extra_specs, n_near, out_w, lam_init=0.0):
    B, NP = page_table.shape
    PG = len(page_maps)
    assert NP % PG == 0
    R = qb_rows.shape[1]
    bh, bi = rowblk

    def page_spec(f):
        def index(b, j, pt):
            d0, d1, lane_blk = f(b, j, pt)
            return (d0, d1, bi, lane_blk)
        return pl.BlockSpec((None, None, bh, PAGE), index)

    in_specs = [page_spec(f) for f in page_maps]
    bspec = lambda a: pl.BlockSpec((None,) + a.shape[1:], lambda b, j, pt: (b,) + (0,) * (a.ndim - 1))
    in_specs += [bspec(qb_rows), bspec(knew), bspec(vnew), bspec(addnew)] + extra_specs
    return pl.pallas_call(
        functools.partial(_decode_kernel, PG=PG, kc0=kc0, KD=KD, vc0=vc0, VD=VD, mode=mode, n_near=n_near,
                          lam_init=lam_init),
        grid_spec=pltpu.PrefetchScalarGridSpec(
            num_scalar_prefetch=1, grid=(B, NP // PG), in_specs=in_specs,
            out_specs=pl.BlockSpec((None, 8, out_w), lambda b, j, pt: (b, 0, 0)),
            scratch_shapes=[pltpu.VMEM((R, 1), F32), pltpu.VMEM((R, 1), F32), pltpu.VMEM((R, VD), F32)]),
        out_shape=jax.ShapeDtypeStruct((B, 8, out_w), F32),
        compiler_params=_cparams(2), name=name,
    )(page_table, *([cache] * PG), qb_rows, knew, vnew, addnew, *extras)


def _prep_layer(l, norm_g, w_in, w_out, qk_a, qk_b, qk_c, onorm_a, lam_a, cmp_w, b_f):
    wl = w_in[l]
    slab_cols = lambda base: [wl[:, base + h * HD:base + (h + 1) * HD] for h in SLAB_HEADS]
    w = jnp.concatenate(
        [wl[:, 0:O_QB]] + slab_cols(O_QB) + [wl[:, O_KVB:O_GB]] + slab_cols(O_ZB)
        + [wl[:, O_QC:O_FC], wl[:, O_ZC:PROJ_W], wl[:, O_GB:O_GB + 18], wl[:, O_FC:O_FC + H_C],
           jnp.zeros((wl.shape[0], P_END - PROJ_W), wl.dtype)], axis=1).astype(BF16)
    wol = w_out[l]
    wo = jnp.concatenate([wol[0:256]] + [wol[256 + h * HD:256 + (h + 1) * HD] for h in SLAB_HEADS]
                         + [wol[640:1024]], axis=0).astype(BF16)
    tile = lambda v, n: jnp.tile(v.astype(F32), n)[None, :]
    bf_row = jnp.zeros((1, LANES), F32).at[0, 18:18 + H_C].set(b_f[l].astype(F32))
    bd = lambda n, seg: jnp.asarray(np.kron(np.eye(n // seg), np.ones((seg, seg))), BF16)
    eg = np.zeros((3, LANES, 384), np.float32)
    for g in range(G_B):
        for r in range(R_B):
            for c in range(3):
                eg[c, g * 9 + r * 3 + c, (r * G_B + g) * HD:(r * G_B + g + 1) * HD] = 1.0
    return dict(
        w_in=w, w_out=wo, g_in=norm_g[l].astype(F32)[None, :],
        ga_q=tile(qk_a[l, 0], 8) * (DA ** -0.5 * LOG2E), ga_k=tile(qk_a[l, 1], 8),
        gb_q=tile(qk_b[l, 0], 6) * (HD ** -0.5 * LOG2E), gb_k=tile(qk_b[l, 1], 2),
        gc_q=tile(qk_c[l, 0], 6) * (HD ** -0.5 * LOG2E), gc_k=tile(qk_c[l, 1], 6),
        bf_row=bf_row, bd32=bd(256, DA), bd64=bd(384, HD),
        cmpw_in=jnp.tile(cmp_w[l].astype(F32)[:, None, :], (1, 8, 8)),
        cmpw_s=jnp.tile(cmp_w[l].astype(F32)[:, None, :], (1, 1, 2)),
        onorm2=tile(onorm_a[l], 2), onorm4=tile(onorm_a[l], 4), lam=lam_a[l].astype(F32),
        eg=jnp.asarray(eg, BF16),
    )


def _toeplitz(tab_h, noff, T, mask):
    H, nt = tab_h.shape
    seg = 2 * T - 1
    padded = jnp.pad(tab_h, ((0, 0), (T - 1, max(0, noff * T - nt))))
    tiles = []
    for d in range(noff):
        u = padded[:, d * T:d * T + seg][:, ::-1]
        flat = jnp.broadcast_to(u[:, None, :], (H, T, seg)).reshape(H, T * seg)
        tiles.append(flat[:, T - 1:T - 1 + T * (seg - 1)].reshape(H, T, seg - 1)[:, :, :T])
    body = jnp.where(jnp.asarray(mask)[None], jnp.stack(tiles, axis=1), NEG)
    return jnp.concatenate([jnp.full((H, 1, T, T), NEG, F32), body, jnp.zeros((H, 1, T, T), F32)], axis=1)


def _shifted_rows(tab_h, width, n_t):
    return jnp.stack([tab_h[:, t + 1:t + 1 + width][:, ::-1] for t in range(n_t)], axis=1)


def _prompt_layer(x3, wp, tabs, lam_init, T):
    B, S, D = x3.shape
    pj = _inproj(x3.reshape(B * S, D), wp, do_cmp=True, seq=(B, S))
    r3 = lambda a: a.reshape(B, S, a.shape[-1])
    rows_out = lambda a, *dims: jnp.moveaxis(a.reshape((B,) + dims + (S,)), -1, 1)
    o_a = _flash_call(
        functools.partial(_flash_a_kernel, T=T, noff=tabs["noff"], lam_init=lam_init), "flash_a", B, S, T, 2,
        [r3(pj["qa"]), r3(pj["ka"]), r3(pj["va"]), tabs["bias_a"], wp["lam"], wp["onorm2"]],
        [_qspec(T), _kvspec(S, LANES, True), _kvspec(S, 2 * LANES, True), _biasspec(tabs["noff"], T),
         _small(wp["lam"]), _small(wp["onorm2"])], 256)
    logf = pj["misc"][:, 18:18 + H_C].reshape(B, S, H_C)
    nr = S // LANES
    lf4 = jnp.swapaxes(logf, 1, 2).reshape(B, H_C, nr, LANES)
    mw = jnp.asarray(np.triu(np.ones((LANES, LANES))), BF16)
    mr = jnp.asarray(np.tril(np.ones((nr, nr)), -1), BF16)
    cparts = _cumsum_call(lf4, mw, mr, 1.0, split=True)
    c_split = jnp.stack([p.reshape(B, 3, 2, S) for p in cparts], axis=-1)
    c_split = jnp.transpose(c_split, (0, 3, 1, 2, 4)).reshape(B, S, 3, 6)
    ones6 = jnp.ones((B, S, 3, 6), BF16)
    zpad = jnp.zeros((B, S, 3, LANES - 12), BF16)
    q_extra = jnp.concatenate([-ones6, c_split, zpad], axis=-1).reshape(B, S, 3 * LANES)
    k_extra = jnp.concatenate([c_split, ones6, zpad], axis=-1).reshape(B, S, 3 * LANES)
    o_c = _flash_call(
        functools.partial(_flash_c_kernel, T=T), "flash_c", B, S, T, 3,
        [r3(pj["qc"]), q_extra, r3(pj["kc"]), k_extra, r3(pj["vc"]), tabs["causal"]],
        [_qspec(T), _qspec(T), _kvspec(S, LANES, True), _kvspec(S, LANES, True), _kvspec(S, 2 * LANES, True),
         _small(tabs["causal"])], 384)
    NB = S // CMP_BLOCK
    NBP = -(-NB // LANES) * LANES
    padb = lambda a: jnp.pad(a.reshape(B, NB, LANES), ((0, 0), (0, NBP - NB), (0, 0))).astype(BF16)
    o_bc, selt = _cmp_call(r3(pj["qb"]), padb(pj["kcmp"]), padb(pj["vcmp"]), T)
    sel = jnp.swapaxes(selt, 2, 3)
    sel = jnp.swapaxes(sel, 1, 2).reshape(B, S, G_B * NBP).astype(BF16)
    o_bs = _flash_call(
        functools.partial(_flash_b_kernel, T=T, noff=tabs["noff"], selected=True), "flash_bsel", B, S, T, 3,
        [r3(pj["qb"]), sel, r3(pj["ksel"]), tabs["blockhot"], r3(pj["vsel"]), tabs["bias_b"]],
        [_qspec(T), pl.BlockSpec((None, T, G_B * NBP), lambda b, g, i: (b, i, 0)),
         _kvspec(S, LANES, False), _small(tabs["blockhot"]), _kvspec(S, 2 * LANES, False),
         _biasspec(tabs["noff"], T)], 384)
    o_bw = _flash_call(
        functools.partial(_flash_b_kernel, T=T, noff=2, selected=False), "flash_bwin", B, S, T, 3,
        [r3(pj["qb"]), r3(pj["kwin"]), r3(pj["vwin"]), tabs["bias_w"]],
        [_qspec(T), _kvspec(S, LANES, False), _kvspec(S, 2 * LANES, False), _biasspec(2, T)], 384)
    f2 = lambda a: a.reshape(B * S, a.shape[-1])
    y = _merge_call(x3.reshape(B * S, D), f2(o_a), f2(o_bc), f2(o_bs), f2(o_bw), f2(o_c), pj["misc"], pj["zs"],
                    wp["w_out"], wp["eg"])
    w = min(WINDOW, S)
    return (y.reshape(B, S, D), rows_out(pj["arow"], 2, H_A, HD), rows_out(pj["kvb"], 4, G_B, HD),
            pj["win"].reshape(B, S, 2 * G_B * HD)[:, S - w:].reshape(B, w, 2, G_B, HD),
            rows_out(pj["crow"], 2, H_C, HD), logf)


def _prompt_tables(rel_bias, T, S):
    noff = -(-(REL_MAX_DIST - 1) // T) + 1
    noff = min(noff, S // T)
    nt = noff * T + 1
    bucket = _rel_bucket(jnp.arange(nt, dtype=jnp.int32))
    tab = jnp.take(rel_bias.astype(F32), bucket, axis=0).T
    far = rel_bias.astype(F32)[N_BUCKETS - 1][:, None]
    i = np.arange(T)[:, None]
    j = np.arange(T)[None, :]
    rel = np.stack([d * T + i - j for d in range(noff)])
    tab_s = (tab - far) * LOG2E
    heads_b = np.asarray([H_A + h for h in SLAB_HEADS])
    tab_b = jnp.stack([tab_s[h] for h in heads_b])
    tab_w = jnp.stack([tab[h] for h in heads_b]) * LOG2E
    bias_a = _toeplitz(tab_s[:H_A], noff, T, rel >= 0)
    bias_b = _toeplitz(tab_b, noff, T, rel >= 0)
    relw = np.stack([d * T + i - j for d in range(2)])
    bias_w = _toeplitz(tab_w, 2, T, (relw >= 0) & (relw < WINDOW))
    causal = jnp.asarray(np.stack([np.full((T, T), NEG), np.where(i >= j, 0.0, NEG), np.zeros((T, T))]), F32)
    NB = S // CMP_BLOCK
    NBP = -(-NB // LANES) * LANES
    hot = (np.arange(S)[:, None] // CMP_BLOCK == np.arange(NBP)[None, :]) * 1e30
    return dict(noff=noff, bias_a=bias_a, bias_b=bias_b, bias_w=bias_w, causal=causal,
                blockhot=jnp.asarray(hot, BF16))


def _sample_tables(rel_bias, P, PG):
    NP = P // PAGE
    n_near_pages = min(NP, -(-REL_MAX_DIST // (PAGE * PG)) * PG)
    assert n_near_pages % PG == 0 or NP == n_near_pages
    nt = n_near_pages * PAGE + 16
    bucket = _rel_bucket(jnp.arange(nt, dtype=jnp.int32))
    rb = rel_bias.astype(F32)
    tab = jnp.take(rb, bucket, axis=0).T
    far = rb[N_BUCKETS - 1][:, None]
    tab_s = (tab - far) * LOG2E
    tab_a = tab_s[:H_A]
    tab_b = jnp.stack([tab_s[H_A + h] for h in SLAB_HEADS])
    wn = n_near_pages * PAGE
    flat = lambda a: a.reshape(a.shape[0] * a.shape[1], a.shape[2])
    near_a = flat(_shifted_rows(tab_a, wn, 8))
    new_a = flat(_new_rows(tab_a))
    return dict(n_near_pages=n_near_pages,
                near_a=jnp.concatenate([near_a, near_a], axis=0), new_a=jnp.concatenate([new_a, new_a], axis=0),
                near_b=flat(_shifted_rows(tab_b, wn, 8)), new_b=flat(_new_rows(tab_b)),
                tab_w=jnp.stack([tab[H_A + h] for h in SLAB_HEADS]) * LOG2E)


def _new_rows(tab_h):
    rows = [jnp.pad(tab_h[:, :t + 1][:, ::-1], ((0, 0), (0, PAGE - t - 1)), constant_values=NEG) for t in range(8)]
    return jnp.stack(rows, axis=1)


def _rows_from_lanes(q, bands, width):
    lane = np.arange(q.shape[-1])
    m = np.stack([(lane >= lo) & (lane < lo + width) for lo in bands])
    return jnp.where(jnp.asarray(m)[None, :, None, :], q[:, None], jnp.zeros((), q.dtype)).reshape(
        q.shape[0], len(bands) * 8, q.shape[-1])


def _pad_rows(a, n):
    return jnp.pad(a, ((0, 0), (0, n - a.shape[1]), (0, 0)))


def _sample_layer(l, x3, caches, page_table, wp, stabs, wtabs, lam_init):
    cache_a, cache_b, cache_c, cache_lf4, state_win = caches
    B, S8, D = x3.shape
    NP = page_table.shape[1]
    P = NP * PAGE
    pj = _inproj(x3.reshape(B * S8, D), wp, do_cmp=False)
    r3 = lambda a: a.reshape(B, S8, a.shape[-1])
    PG = min(DECODE_PAGES, NP)
    n_near = stabs["n_near_pages"] // PG if stabs["n_near_pages"] >= PG else 1
    split_steps = lambda a: jnp.swapaxes(a.reshape(a.shape[0], -1, PG * PAGE), 0, 1)
    bcast = lambda a: jnp.broadcast_to(a[None], (B,) + a.shape)
    pool_maps = [functools.partial(lambda b, j, pt, i: (l, pt[b, j * PG + i], 0), i=i) for i in range(PG)]

    qa_rows = _rows_from_lanes(r3(pj["qa"]), [h * HD + mp * DA for mp in range(2) for h in range(H_A)], DA)
    arow = r3(pj["arow"])
    near_a = split_steps(stabs["near_a"])
    o_a = _decode_call(
        "decode_a", cache_a, pool_maps, page_table, (512, 0), 0, 256, 256, 256, "a", qa_rows,
        _pad_rows(arow[..., 0:256], PAGE).astype(BF16), _pad_rows(arow[..., 256:512], PAGE).astype(BF16),
        bcast(stabs["new_a"]), [near_a, wp["lam"], wp["onorm4"]],
        [pl.BlockSpec(near_a.shape, lambda b, j, pt: (0, 0, 0)), pl.BlockSpec(wp["lam"].shape, lambda b, j, pt: (0, 0)),
         pl.BlockSpec(wp["onorm4"].shape, lambda b, j, pt: (0, 0))], n_near, 256, lam_init)

    lf_pages = _gather_logf(cache_lf4, l, page_table)
    lf4 = jnp.swapaxes(lf_pages.reshape(B, P, H_C), 1, 2).reshape(B, H_C, NP, PAGE)
    msu = jnp.asarray(np.tril(np.ones((LANES, LANES)), -1), BF16)
    mpu = jnp.asarray(np.triu(np.ones((NP, NP)), 1), BF16)
    c_past = _cumsum_call(lf4, msu, mpu, -1.0).reshape(B, H_C, P)
    c_past = jnp.pad(c_past, ((0, 0), (0, 8 - H_C), (0, 0)))
    logf = pj["misc"][:, 18:18 + H_C].reshape(B, S8, H_C)
    lfn = jnp.pad(jnp.swapaxes(logf, 1, 2), ((0, 0), (0, 8 - H_C), (0, LANES - S8)))[:, None]
    mw = jnp.asarray(np.triu(np.ones((LANES, LANES))), BF16)
    c_new = _cumsum_call(lfn, mw, jnp.zeros((8, 8), BF16), 1.0)[:, 0]
    cq_rows = c_new[:, :H_C, :S8].reshape(B, H_C * S8, 1)
    qc_rows = _rows_from_lanes(r3(pj["qc"]), [h * HD for h in range(H_C)], HD)
    crow = r3(pj["crow"])
    jj = np.arange(PAGE)
    tq = np.tile(np.arange(8), H_C)
    causal_new = jnp.asarray(np.where((jj[None, :] <= tq[:, None]) & (jj[None, :] < 8), 0.0, NEG), F32)
    o_c = _decode_call(
        "decode_c", cache_c, pool_maps, page_table, (768, 0), 0, 384, 384, 384, "c", qc_rows,
        _pad_rows(crow[..., 0:384], PAGE).astype(BF16), _pad_rows(crow[..., 384:768], PAGE).astype(BF16),
        bcast(causal_new), [c_past, c_new, cq_rows],
        [pl.BlockSpec((None, 8, PG * PAGE), lambda b, j, pt: (b, 0, j)),
         pl.BlockSpec((None, 8, LANES), lambda b, j, pt: (b, 0, 0)),
         pl.BlockSpec((None, H_C * S8, 1), lambda b, j, pt: (b, 0, 0))], 0, 384)

    gmask = jnp.asarray(np.arange(LANES)[None, :] // HD == np.arange(G_B)[:, None])
    qb5 = jnp.where(gmask[None, None, None], r3(pj["qb"]).reshape(B, S8, R_B, 1, LANES), jnp.zeros((), BF16))
    qb_rows = jnp.transpose(qb5, (0, 2, 3, 1, 4)).reshape(B, H_B * 8, LANES)
    kvb = r3(pj["kvb"])
    kvb_pad = _pad_rows(kvb, PAGE)
    o_bc_rows, selm1 = _scmp_call(cache_b, l, page_table, wp["cmpw_s"], kvb_pad[..., 0:256], qb_rows, P)
    rowi = np.arange(H_B * 8)
    lanei = np.arange(LANES)
    keep = jnp.asarray(((rowi[:, None] >> 3) & 1) == (lanei[None, :] >> 6))
    x = jnp.where(keep[None], o_bc_rows, 0.0).reshape(B, R_B, G_B, 8, LANES)
    o_bc = jnp.transpose(x[:, :, 0] + x[:, :, 1], (0, 2, 1, 3)).reshape(B, 8, R_B * LANES)
    selrows = jnp.broadcast_to(selm1[:, None], (B, R_B, G_B, 8, selm1.shape[-1])).reshape(B, H_B * 8, -1)
    nbs = 2 * PG
    sel_steps = jnp.swapaxes(selrows[..., :2 * NP].reshape(B, H_B * 8, NP // PG, nbs), 1, 2)
    sel_steps = jnp.pad(sel_steps, ((0, 0), (0, 0), (0, 0), (0, LANES - nbs))).astype(BF16)
    hot = (np.arange(LANES)[:, None] == np.arange(PG * PAGE)[None, :] // CMP_BLOCK) * 1e30
    hot = jnp.asarray(hot, BF16)
    amask_new = jnp.broadcast_to(selrows[..., 2 * NP:2 * NP + 1], (B, H_B * 8, PAGE)) * 1e30
    near_b = split_steps(stabs["near_b"])
    o_bs = _decode_call(
        "decode_bsel", cache_b, pool_maps, page_table, (256, 1), 0, 128, 128, 128, "bs", qb_rows,
        kvb_pad[..., 256:384].astype(BF16), kvb_pad[..., 384:512].astype(BF16),
        stabs["new_b"][None] + amask_new, [near_b, sel_steps, hot],
        [pl.BlockSpec(near_b.shape, lambda b, j, pt: (0, 0, 0)),
         pl.BlockSpec((None, None, H_B * 8, LANES), lambda b, j, pt: (b, j, 0, 0)),
         pl.BlockSpec(hot.shape, lambda b, j, pt: (0, 0))], n_near, 384)
    wb = state_win.shape[2]
    npw = wb // PAGE
    win_t = jnp.transpose(state_win, (0, 1, 3, 4, 5, 2)).reshape(state_win.shape[0], B, 2 * G_B * HD, wb)
    pt_w = jnp.zeros((B, npw), jnp.int32)
    win_maps = [functools.partial(lambda b, j, pt, i: (l, b, i), i=i) for i in range(npw)]
    win = r3(pj["win"])
    win_pad = _pad_rows(win, PAGE)
    near_w = wtabs["near_w"][None]
    o_bw = _decode_call(
        "decode_bwin", win_t, win_maps, pt_w, (256, 0), 0, 128, 128, 128, "bw", qb_rows,
        win_pad[..., 0:128].astype(BF16), win_pad[..., 128:256].astype(BF16),
        bcast(wtabs["new_w"]), [near_w], [pl.BlockSpec(near_w.shape, lambda b, j, pt: (0, 0, 0))], 1, 384)

    f2 = lambda a: a.reshape(B * S8, a.shape[-1])
    y = _merge_call(x3.reshape(B * S8, D), f2(o_a), f2(o_bc), f2(o_bs), f2(o_bw), f2(o_c), pj["misc"], pj["zs"],
                    wp["w_out"], wp["eg"])
    win_all = jnp.concatenate([state_win[l].reshape(B, wb, 2 * G_B * HD), win], axis=1)[:, -wb:]
    return (y.reshape(B, S8, D), arow.reshape(B, S8, 2, H_A, HD), kvb.reshape(B, S8, 4, G_B, HD),
            win_all.reshape(B, wb, 2, G_B, HD), crow.reshape(B, S8, 2, H_C, HD), logf)


def _window_tables(stabs, wb):
    tab_w = stabs["tab_w"]
    assert tab_w.shape[1] >= wb + 8
    rel = wb + np.arange(8)[:, None] - np.arange(wb)[None, :]
    near_w = jnp.where(jnp.asarray(rel < WINDOW)[None], _shifted_rows(tab_w, wb, 8), NEG)
    new_w = _new_rows(tab_w)
    flat = lambda a: a.reshape(a.shape[0] * a.shape[1], a.shape[2])
    return dict(near_w=flat(near_w), new_w=flat(new_w))


def kernel(x_prompt, x_sample, cache_a_kv, cache_b_kv, cache_c_kv, cache_c_logf, state_b_win, page_table,
           rel_bias, norm_g, w_in, w_out, qk_a, qk_b, qk_c, onorm_a, lam_a, cmp_w, b_f):
    depth = w_in.shape[0]
    B, S, _ = x_prompt.shape
    DB, S8, _ = x_sample.shape
    assert S8 == 8 and S % 512 == 0 and state_b_win.shape[2] == WINDOW
    T = 512
    NP = page_table.shape[1]
    P = NP * PAGE
    n_pool = cache_a_kv.shape[1]
    ptabs = _prompt_tables(rel_bias, T, S)
    stabs = _sample_tables(rel_bias, P, min(DECODE_PAGES, NP))
    wtabs = _window_tables(stabs, state_b_win.shape[2])
    fmajor = lambda c: jnp.transpose(c, (0, 1, 3, 4, 5, 2)).reshape(depth, n_pool, -1, PAGE)
    caches = (fmajor(cache_a_kv), fmajor(cache_b_kv), fmajor(cache_c_kv),
              cache_c_logf.reshape(depth, n_pool, 1, PAGE * H_C), state_b_win)
    page_table = page_table.astype(jnp.int32)
    yp, ys = x_prompt, x_sample
    outs = [[] for _ in range(10)]
    for l in range(depth):
        lam_init = 0.8 - 0.6 * math.exp(-0.3 * l)
        wp = _prep_layer(l, norm_g, w_in, w_out, qk_a, qk_b, qk_c, onorm_a, lam_a, cmp_w, b_f)
        yp, a_r, b_r, w_r, c_r, l_r = _prompt_layer(yp, wp, ptabs, lam_init, T)
        for k, v in zip((0, 2, 4, 6, 8), (a_r, b_r, w_r, c_r, l_r)):
            outs[k].append(v)
        ys, a_r, b_r, w_r, c_r, l_r = _sample_layer(l, ys, caches, page_table, wp, stabs, wtabs, lam_init)
        for k, v in zip((1, 3, 5, 7, 9), (a_r, b_r, w_r, c_r, l_r)):
            outs[k].append(v)
    return (yp, ys) + tuple(jnp.stack(o) for o in outs)
```
